```python
import jax, jax.numpy as jnp
from jax import lax
import numpy as np

D_MODEL = 4096
BATCH = 2
SEQ = 4096
DEPTH = 2

CTX_LEN = 256
GRID_W = 64

GLA_WIDTH = 1536
GLA_HEADS = 6
GLA_DK = 128
GLA_DV = GLA_WIDTH // GLA_HEADS
GLA_RANK = 16
GLA_TAU = 16.0
RET_WIDTH = 1280
RET_HEADS = 5
RET_DK = 128
RET_DV = RET_WIDTH // RET_HEADS
ROPE_BASE = 10000.0
RWKV_WIDTH = D_MODEL - GLA_WIDTH - RET_WIDTH
RWKV_HEAD = 64
RWKV_HEADS = RWKV_WIDTH // RWKV_HEAD
RWKV_DECAY_RANK = 128
RWKV_ICLR_RANK = 128
RWKV_GATE_RANK = 480
GN_EPS = 64e-5
CHUNK = 64
N_EXPERTS = 64
TOP_K = 8
D_EXPERT = 384
ROUTED_SCALE = 2.5
MOE_BLOCK = 128
RMS_EPS = 1e-6

GLA_COLS = (GLA_HEADS * GLA_DK, GLA_HEADS * GLA_DK, GLA_WIDTH, GLA_WIDTH, GLA_RANK)
RET_COLS = (RET_HEADS * RET_DK, RET_HEADS * RET_DK, RET_WIDTH, RET_WIDTH)
RWKV_COLS = (RWKV_WIDTH, RWKV_WIDTH, RWKV_WIDTH, RWKV_DECAY_RANK, RWKV_ICLR_RANK, RWKV_GATE_RANK)
GLA_N = 4624
RET_N = 3840
RWKV_N = 4576
GROUP_COLS = (GLA_N, RET_N, RWKV_N)
N_IN = GLA_N + RET_N + RWKV_N

kernel_name = 'hybrid_gla_retnet_rwkv7_moe_dit'


def rmsnorm(x, g):
    x32 = x.astype(jnp.float32)
    y = x32 * lax.rsqrt(jnp.mean(x32 * x32, -1, keepdims=True) + RMS_EPS)
    return (y * g.astype(jnp.float32)).astype(x.dtype)


def head_rmsnorm(o):
    o32 = o.astype(jnp.float32)
    return o32 * lax.rsqrt(jnp.mean(o32 * o32, -1, keepdims=True) + RMS_EPS)


def group_norm(o, w, b, n_groups):
    bsz, t, ch = o.shape
    o32 = o.astype(jnp.float32).reshape(bsz, t, n_groups, ch // n_groups)
    mean = jnp.mean(o32, -1, keepdims=True)
    var = jnp.mean(jnp.square(o32 - mean), -1, keepdims=True)
    return ((o32 - mean) * lax.rsqrt(var + GN_EPS)).reshape(bsz, t, ch) * w + b


def split_cols(z, sizes):
    return jnp.split(z, [int(i) for i in np.cumsum(sizes)[:-1]], axis=-1)


def heads(z, n):
    bsz, t, ch = z.shape
    return z.reshape(bsz, t, n, ch // n).transpose(0, 2, 1, 3)


def unheads(o):
    bsz, n, t, d = o.shape
    return o.transpose(0, 2, 1, 3).reshape(bsz, t, n * d)


def two_dirs(z, axis=1):
    return jnp.concatenate([z, jnp.flip(z, axis)], 0)


def merge_dirs(o, axis=1):
    n = o.shape[0] // 2
    return o[:n] + jnp.flip(o[n:], axis)


def rope(x, pos):
    d = x.shape[-1]
    inv = ROPE_BASE ** (-jnp.linspace(0.0, 1.0, d // 2, dtype=jnp.float32))
    ang = pos.astype(jnp.float32)[:, None] * inv[None, :]
    cos, sin = jnp.cos(ang), jnp.sin(ang)
    x1, x2 = x[..., : d // 2].astype(jnp.float32), x[..., d // 2:].astype(jnp.float32)
    return jnp.concatenate([x1 * cos - x2 * sin, x1 * sin + x2 * cos], -1)


def shift_seq(u):
    h = u.shape[-1] // 2
    prev = jnp.pad(u[:, :-1, :h], ((0, 0), (1, 0), (0, 0)))
    nxt = jnp.pad(u[:, 1:, h:], ((0, 0), (0, 1), (0, 0)))
    return jnp.concatenate([prev, nxt], -1)


def shift_grid(u):
    bsz, t, ch = u.shape
    rows = t // GRID_W
    g = u.reshape(bsz, rows, GRID_W, ch)
    q = ch // 4
    left = jnp.pad(g[:, :, :-1, :q], ((0, 0), (0, 0), (1, 0), (0, 0)))
    right = jnp.pad(g[:, :, 1:, q:2 * q], ((0, 0), (0, 0), (0, 1), (0, 0)))
    up = jnp.pad(g[:, :-1, :, 2 * q:3 * q], ((0, 0), (1, 0), (0, 0), (0, 0)))
    down = jnp.pad(g[:, 1:, :, 3 * q:], ((0, 0), (0, 1), (0, 0), (0, 0)))
    return jnp.concatenate([left, right, up, down], -1).reshape(bsz, t, ch)


def chunked_gated_scan(q, k, v, log_a, s0):
    n_b, n_h, t, _ = q.shape
    n_chunks = t // CHUNK

    def chunks(z):
        return jnp.moveaxis(z.astype(jnp.float32).reshape(n_b, n_h, n_chunks, CHUNK, z.shape[-1]), 2, 0)

    lower = jnp.tril(jnp.ones((CHUNK, CHUNK), bool))[:, :, None]
    scalar_decay = log_a.shape[-1] == 1

    def step(s, inp):
        qc, kc, vc, lc = inp
        b = jnp.cumsum(lc, axis=-2)
        diff = jnp.where(lower, b[..., :, None, :] - b[..., None, :, :], -jnp.inf)
        decay = jnp.exp(diff)
        if scalar_decay:
            scores = jnp.einsum('nhid,nhjd->nhij', qc, kc) * decay[..., 0]
        else:
            scores = jnp.einsum('nhid,nhjd,nhijd->nhij', qc, kc, decay)
        o = jnp.einsum('nhij,nhjv->nhiv', scores, vc) + jnp.einsum('nhid,nhdv->nhiv', qc * jnp.exp(b), s)
        b_end = b[..., -1:, :]
        s = jnp.exp(b_end[..., 0, :])[..., None] * s + jnp.einsum('nhjd,nhjv->nhdv', kc * jnp.exp(b_end - b), vc)
        return s, o

    s, o = lax.scan(step, s0.astype(jnp.float32), (chunks(q), chunks(k), chunks(v), chunks(log_a)))
    o = jnp.moveaxis(o, 0, 2).reshape(n_b, n_h, t, v.shape[-1])
    return o, s


def prefix_bidir_chunked(ctx_in, lat_in):
    qc, vc = ctx_in[0], ctx_in[2]
    s0 = jnp.zeros(qc.shape[:2] + (qc.shape[-1], vc.shape[-1]), jnp.float32)
    oc, s_ctx = chunked_gated_scan(*ctx_in, s0)
    ox, _ = chunked_gated_scan(*lat_in, s_ctx)
    return merge_dirs(oc, 2), merge_dirs(ox, 2)


def gla_mixer(zc, zx, a2, a_bias, norm_w, ctx_out):
    def prep(z):
        q, k, v, _, lr = split_cols(z, GLA_COLS)
        la = jax.nn.log_sigmoid(jnp.einsum('btr,xrc->xbtc', lr.astype(jnp.float32), a2) + a_bias[:, None, None, :]) / GLA_TAU
        la = jnp.concatenate([la[0], jnp.flip(la[1], 1)], 0)
        q = q * GLA_DK ** -0.5
        return tuple(heads(t, GLA_HEADS) for t in (two_dirs(q), two_dirs(k), two_dirs(v), la))

    oc, ox = prefix_bidir_chunked(prep(zc), prep(zx))
    o, z = (jnp.concatenate([oc, ox], 2), jnp.concatenate([zc, zx], 1)) if ctx_out else (ox, zx)
    g = split_cols(z, GLA_COLS)[3]
    return unheads(head_rmsnorm(o) * norm_w) * jax.nn.silu(g.astype(jnp.float32))


def retnet_mixer(zc, zx, decay_logit, ctx_out):
    lc = zc.shape[1]
    log_gamma = jax.nn.log_sigmoid(decay_logit.astype(jnp.float32))

    def prep(z, pos0):
        q, k, v, _ = split_cols(z, RET_COLS)
        n, t = z.shape[0], z.shape[1]
        pos = pos0 + jnp.arange(t)
        q = rope(heads(q, RET_HEADS), pos)
        k = rope(heads(k, RET_HEADS), pos) * RET_DK ** -0.5
        la = jnp.broadcast_to(log_gamma[:, None, :, None, None], (2, n, RET_HEADS, t, 1)).reshape(2 * n, RET_HEADS, t, 1)
        return two_dirs(q, 2), two_dirs(k, 2), two_dirs(heads(v, RET_HEADS), 2), la

    oc, ox = prefix_bidir_chunked(prep(zc, 0), prep(zx, lc))
    o, z = (jnp.concatenate([oc, ox], 2), jnp.concatenate([zc, zx], 1)) if ctx_out else (ox, zx)
    g = split_cols(z, RET_COLS)[3]
    return unheads(head_rmsnorm(o)) * jax.nn.silu(g.astype(jnp.float32))


def rwkv_scan(r, k, v, kk, kka, logw, s0):
    def step(s, inp):
        r_t, k_t, v_t, kk_t, kka_t, w_t = inp
        s = (s * jnp.exp(w_t)[..., None, :]
             - jnp.einsum('nhvk,nhk->nhv', s, kk_t)[..., None] * kka_t[..., None, :]
             + v_t[..., None] * k_t[..., None, :])
        return s, jnp.einsum('nhvk,nhk->nhv', s, r_t)
    s, o = lax.scan(step, s0, (r, k, v, kk, kka, logw))
    return o, s


def rwkv_mixer(zc, zx, mu, w0, w2, a0, a2, g2, k_k, k_a, r_k, lnx_w, lnx_b, ctx_out):
    def per_head(t):
        return t.reshape(t.shape[0], t.shape[1], RWKV_HEADS, RWKV_HEAD)

    def prep(z, shift):
        z = z.astype(jnp.float32)
        z = z + mu * (shift(z) - z)
        r, k, v, xw, xa, xg = split_cols(z, RWKV_COLS)
        w_pre = w0[:, None, None, :] + jnp.einsum('btr,xrc->xbtc', jnp.tanh(xw), w2)
        logw = -jnp.exp(-jax.nn.softplus(-w_pre) - 0.5)
        logw = jnp.concatenate([logw[0], jnp.flip(logw[1], 1)], 0)
        a = jax.nn.sigmoid(a0 + xa @ a2)
        g = jax.nn.sigmoid(xg) @ g2
        kk = per_head(k * k_k)
        kk = (kk * lax.rsqrt(jnp.sum(kk * kk, -1, keepdims=True) + 1e-12)).reshape(k.shape)
        k = k * (1.0 + (a - 1.0) * k_a)
        bonus = (jnp.sum(per_head(r * k * r_k.reshape(-1)), -1, keepdims=True) * per_head(v)).reshape(v.shape)
        streams = tuple(two_dirs(t) for t in (r, k, v, kk, kk * a)) + (logw,)
        return streams, g, bonus

    def run(streams, s0):
        tm = [jnp.moveaxis(per_head(t), 1, 0) for t in streams]
        o, s = rwkv_scan(*tm, s0)
        o = jnp.moveaxis(o, 0, 1)
        return merge_dirs(o.reshape(o.shape[0], o.shape[1], RWKV_WIDTH), 1), s

    (sc, gc, bc), (sx, gx, bx) = prep(zc, shift_seq), prep(zx, shift_grid)
    oc, s_ctx = run(sc, jnp.zeros((2 * zc.shape[0], RWKV_HEADS, RWKV_HEAD, RWKV_HEAD), jnp.float32))
    ox, _ = run(sx, s_ctx)
    if ctx_out:
        o, g, bonus = jnp.concatenate([oc, ox], 1), jnp.concatenate([gc, gx], 1), jnp.concatenate([bc, bx], 1)
    else:
        o, g, bonus = ox, gx, bx
    return (group_norm(o, lnx_w, lnx_b, RWKV_HEADS) + bonus) * g


def token_mixer(hc, hx, w_in, w_out, gla_a2, gla_a_bias, gla_norm_w, ret_decay_logit,
                rwkv_mu, rwkv_w0, rwkv_w2, rwkv_a0, rwkv_a2, rwkv_g2, rwkv_k_k, rwkv_k_a,
                rwkv_r_k, rwkv_lnx_w, rwkv_lnx_b, ctx_out):
    lc = hc.shape[1]
    z = jnp.concatenate([hc, hx], 1) @ w_in
    zc = split_cols(z[:, :lc], GROUP_COLS)
    zx = split_cols(z[:, lc:], GROUP_COLS)
    y = jnp.concatenate([
        gla_mixer(zc[0], zx[0], gla_a2, gla_a_bias, gla_norm_w, ctx_out),
        retnet_mixer(zc[1], zx[1], ret_decay_logit, ctx_out),
        rwkv_mixer(zc[2], zx[2], rwkv_mu, rwkv_w0, rwkv_w2, rwkv_a0, rwkv_a2, rwkv_g2,
                   rwkv_k_k, rwkv_k_a, rwkv_r_k, rwkv_lnx_w, rwkv_lnx_b, ctx_out),
    ], -1).astype(hx.dtype) @ w_out
    return (y[:, :lc], y[:, lc:]) if ctx_out else (None, y)


def swiglu(h, wg, wu, wd):
    return (jax.nn.silu(h @ wg) * (h @ wu)) @ wd


def moe_ffn(h, router_w, router_bias, exp_gate, exp_up, exp_down, sh_gate, sh_up, sh_down):
    t, d = h.shape
    scores = jax.nn.sigmoid(h.astype(jnp.float32) @ router_w.astype(jnp.float32))
    _, idx = lax.top_k(scores + router_bias.astype(jnp.float32), TOP_K)
    sel = jnp.take_along_axis(scores, idx, 1)
    gates = ROUTED_SCALE * sel / jnp.sum(sel, -1, keepdims=True)
    flat_e = idx.reshape(-1)
    flat_tok = jnp.repeat(jnp.arange(t, dtype=jnp.int32), TOP_K)
    order = jnp.argsort(flat_e)
    e_sorted = flat_e[order]
    counts = jnp.bincount(flat_e, length=N_EXPERTS)
    padded = (counts + MOE_BLOCK - 1) // MOE_BLOCK * MOE_BLOCK
    ends = jnp.cumsum(padded)
    dest = (ends - padded)[e_sorted] + jnp.arange(t * TOP_K) - (jnp.cumsum(counts) - counts)[e_sorted]
    n_blocks = -(-(t * TOP_K) // MOE_BLOCK) + N_EXPERTS
    buf_tok = jnp.full((n_blocks * MOE_BLOCK,), t, jnp.int32).at[dest].set(flat_tok[order])
    buf_gate = jnp.zeros((n_blocks * MOE_BLOCK,), jnp.float32).at[dest].set(gates.reshape(-1)[order])
    block_expert = jnp.minimum(jnp.searchsorted(ends, jnp.arange(n_blocks) * MOE_BLOCK, side='right'), N_EXPERTS - 1)
    h_pad = jnp.concatenate([h, jnp.zeros((1, d), h.dtype)], 0)

    def block(acc, inp):
        tok, gate, e = inp
        y = swiglu(h_pad[tok], exp_gate[e], exp_up[e], exp_down[e])
        return acc.at[tok].add(y.astype(jnp.float32) * gate[:, None]), None

    acc, _ = lax.scan(block, jnp.zeros((t + 1, d), jnp.float32),
                      (buf_tok.reshape(n_blocks, MOE_BLOCK), buf_gate.reshape(n_blocks, MOE_BLOCK), block_expert))
    return (acc[:t] + swiglu(h, sh_gate, sh_up, sh_down).astype(jnp.float32)).astype(h.dtype)


def setup_inputs(seed: int = 0) -> dict:
    key = jax.random.key(seed)
    keys = iter(jax.random.split(key, 40))

    def nrm(shape, scale):
        return jax.random.normal(next(keys), shape, jnp.float32) * scale

    L, D, E, F, C = DEPTH, D_MODEL, N_EXPERTS, D_EXPERT, RWKV_WIDTH
    ret_logit = jnp.log(2.0 ** (5.0 + jnp.arange(RET_HEADS, dtype=jnp.float32)) - 1.0)
    return {
        'x': nrm((BATCH, SEQ, D), 1.0),
        'c': nrm((BATCH, D), 1.0),
        'ctx': nrm((BATCH, CTX_LEN, D), 1.0),
        'c_ctx': nrm((D,), 1.0),
        'ada_w': nrm((L, D, 6 * D), 0.5 * D ** -0.5),
        'ada_b': nrm((L, 6 * D), 0.02),
        'pre_mix': 1.0 + nrm((L, D), 0.05),
        'post_mix': 1.0 + nrm((L, D), 0.05),
        'pre_ffn': 1.0 + nrm((L, D), 0.05),
        'post_ffn': 1.0 + nrm((L, D), 0.05),
        'w_in': nrm((L, D, N_IN), D ** -0.5),
        'w_out': nrm((L, D, D), D ** -0.5),
        'gla_a2': nrm((L, 2, GLA_RANK, GLA_HEADS * GLA_DK), GLA_RANK ** -0.5),
        'gla_a_bias': nrm((L, 2, GLA_HEADS * GLA_DK), 0.5),
        'gla_norm_w': 1.0 + nrm((L, GLA_DV), 0.05),
        'ret_decay_logit': ret_logit + nrm((L, 2, RET_HEADS), 0.1),
        'rwkv_mu': jax.random.uniform(next(keys), (L, RWKV_N), jnp.float32),
        'rwkv_w0': jnp.linspace(-6.0, -1.0, C, dtype=jnp.float32) + nrm((L, 2, C), 0.1),
        'rwkv_w2': nrm((L, 2, RWKV_DECAY_RANK, C), 0.1 * RWKV_DECAY_RANK ** -0.5),
        'rwkv_a0': nrm((L, C), 0.5),
        'rwkv_a2': nrm((L, RWKV_ICLR_RANK, C), RWKV_ICLR_RANK ** -0.5),
        'rwkv_g2': nrm((L, RWKV_GATE_RANK, C), RWKV_GATE_RANK ** -0.5),
        'rwkv_k_k': 0.85 + nrm((L, C), 0.05),
        'rwkv_k_a': 1.0 + nrm((L, C), 0.05),
        'rwkv_r_k': nrm((L, RWKV_HEADS, RWKV_HEAD), 0.1),
        'rwkv_lnx_w': 1.0 + nrm((L, C), 0.05),
        'rwkv_lnx_b': nrm((L, C), 0.02),
        'router_w': nrm((L, D, E), D ** -0.5),
        'router_bias': nrm((L, E), 0.01),
        'exp_gate': nrm((L, E, D, F), D ** -0.5),
        'exp_up': nrm((L, E, D, F), D ** -0.5),
        'exp_down': nrm((L, E, F, D), F ** -0.5),
        'shared_gate': nrm((L, D, F), D ** -0.5),
        'shared_up': nrm((L, D, F), D ** -0.5),
        'shared_down': nrm((L, F, D), F ** -0.5),
    }


def reference(x, c, ctx, c_ctx, ada_w, ada_b, pre_mix, post_mix, pre_ffn, post_ffn, w_in, w_out,
              gla_a2, gla_a_bias, gla_norm_w, ret_decay_logit, rwkv_mu, rwkv_w0, rwkv_w2, rwkv_a0,
              rwkv_a2, rwkv_g2, rwkv_k_k, rwkv_k_a, rwkv_r_k, rwkv_lnx_w, rwkv_lnx_b, router_w,
              router_bias, exp_gate, exp_up, exp_down, shared_gate, shared_up, shared_down):
    d = x.shape[-1]
    lc = ctx.shape[1]
    for l in range(DEPTH):
        ctx_out = l < DEPTH - 1
        mx = jnp.split((jax.nn.silu(c) @ ada_w[l] + ada_b[l])[:, None, :], 6, axis=-1)
        mc = jnp.split(jax.nn.silu(c_ctx) @ ada_w[l] + ada_b[l], 6, axis=-1)

        hx = rmsnorm(x, pre_mix[l]) * (1.0 + mx[1]) + mx[0]
        hc = rmsnorm(ctx, pre_mix[l]) * (1.0 + mc[1]) + mc[0]
        yc, yx = token_mixer(hc, hx, w_in[l], w_out[l], gla_a2[l], gla_a_bias[l], gla_norm_w[l],
                             ret_decay_logit[l], rwkv_mu[l], rwkv_w0[l], rwkv_w2[l], rwkv_a0[l],
                             rwkv_a2[l], rwkv_g2[l], rwkv_k_k[l], rwkv_k_a[l], rwkv_r_k[l],
                             rwkv_lnx_w[l], rwkv_lnx_b[l], ctx_out)
        x = x + mx[2] * rmsnorm(yx, post_mix[l])
        if ctx_out:
            ctx = ctx + mc[2] * rmsnorm(yc, post_mix[l])

        def ffn(h):
            flat = moe_ffn(h.reshape(-1, d), router_w[l], router_bias[l], exp_gate[l], exp_up[l],
                           exp_down[l], shared_gate[l], shared_up[l], shared_down[l])
            return flat.reshape(h.shape)

        hx = rmsnorm(x, pre_ffn[l]) * (1.0 + mx[4]) + mx[3]
        if ctx_out:
            hc = rmsnorm(ctx, pre_ffn[l]) * (1.0 + mc[4]) + mc[3]
            f = ffn(jnp.concatenate([hc, hx], 1))
            ctx = ctx + mc[5] * rmsnorm(f[:, :lc], post_ffn[l])
            fx = f[:, lc:]
        else:
            fx = ffn(hx)
        x = x + mx[5] * rmsnorm(fx, post_ffn[l])
    return x
```

```python
import functools

import numpy as np
import jax
import jax.numpy as jnp
from jax import lax
from jax.experimental import pallas as pl
from jax.experimental.pallas import tpu as pltpu

F32 = jnp.float32
BF16 = jnp.bfloat16

CHUNK = 64
LEVELS = (32, 16, 8, 4, 2, 1)

GLA_HEADS, GLA_DK, GLA_DV, GLA_RANK, GLA_TAU = 6, 128, 256, 16, 16.0
RET_HEADS, RET_DK, RET_DV = 5, 128, 256
ROPE_BASE = 10000.0
RWKV_HEADS, RWKV_HEAD = 20, 64
RWKV_WIDTH = RWKV_HEADS * RWKV_HEAD
RWKV_DECAY_RANK, RWKV_ICLR_RANK, RWKV_GATE_RANK = 128, 128, 480
GN_EPS = 64e-5
RMS_EPS = 1e-6
N_EXPERTS, TOP_K, D_EXPERT, ROUTED_SCALE = 64, 8, 384, 2.5
GRID_W = 64

VMEM_LIMIT = 56 * 1024 * 1024


def _cparams(sem):
    return pltpu.CompilerParams(dimension_semantics=sem, vmem_limit_bytes=VMEM_LIMIT)


def _mm_body(a_ref, b_ref, o_ref):
    o_ref[...] = jnp.dot(a_ref[...].astype(BF16), b_ref[...].astype(BF16),
                         preferred_element_type=F32).astype(o_ref.dtype)


def matmul(a, b, *, tm, tn, out_dtype=F32, name="mm"):
    m, k = a.shape
    _, n = b.shape
    assert m % tm == 0 and n % tn == 0, (a.shape, b.shape, tm, tn)
    return pl.pallas_call(
        _mm_body,
        out_shape=jax.ShapeDtypeStruct((m, n), out_dtype),
        grid=(m // tm, n // tn),
        in_specs=[pl.BlockSpec((tm, k), lambda i, j: (i, 0)),
                  pl.BlockSpec((k, tn), lambda i, j: (0, j))],
        out_specs=pl.BlockSpec((tm, tn), lambda i, j: (i, j)),
        compiler_params=_cparams(("parallel", "arbitrary")),
        name=name,
    )(a, b)


def _scan_consts():
    c = CHUNK
    cum = np.zeros((2, 2 * c + 8, c), np.float32)
    masks = np.zeros((2, len(LEVELS), c, c), np.float32)
    tri = np.tril(np.ones((c, c), np.float32))
    cum[0, :c] = tri
    cum[0, c:2 * c] = tri
    cum[1, :c] = tri.T
    cum[1, c:2 * c] = np.triu(np.ones((c, c), np.float32), 1)
    cum[:, 2 * c:] = 1.0
    for li, s in enumerate(LEVELS):
        for i in range(c):
            for j in range(c):
                if (i & s) and not (j & s) and i // (2 * s) == j // (2 * s):
                    masks[0, li, i, j] = 1.0
        masks[1, li] = masks[0, li].T
    return jnp.asarray(cum, BF16), jnp.asarray(masks)


def _split3(x):
    hi = x.astype(BF16)
    r1 = x - hi.astype(F32)
    mid = r1.astype(BF16)
    lo = (r1 - mid.astype(F32)).astype(BF16)
    return hi, mid, lo


def _cumsums(cum, la):
    d = la.shape[1]
    r = jnp.dot(cum, jnp.concatenate(_split3(la), axis=1), preferred_element_type=F32)
    return r[:, :d] + r[:, d:2 * d] + r[:, 2 * d:]


def _boundary(cc, s):
    c, d = cc.shape
    if s >= 8:
        parts = [jnp.broadcast_to(cc[r0 + s - 1:r0 + s], (2 * s, d)) for r0 in range(0, c, 2 * s)]
        return parts[0] if len(parts) == 1 else jnp.concatenate(parts, axis=0)
    cc3 = cc.reshape(c // 8, 8, d)
    if s == 4:
        return jnp.broadcast_to(cc3[:, 3:4], cc3.shape).reshape(c, d)
    sub = lax.broadcasted_iota(jnp.int32, cc3.shape, 1)
    lo = jnp.broadcast_to(cc3[:, 1:2], cc3.shape)
    hi = jnp.broadcast_to(cc3[:, 5:6], cc3.shape)
    return jnp.where(sub < 4, lo, hi).reshape(c, d)


def _nt(a, b):
    return lax.dot_general(a, b, (((1,), (1,)), ((), ())), preferred_element_type=F32)


def _tn(a, b):
    return lax.dot_general(a, b, (((0,), (0,)), ((), ())), preferred_element_type=F32)


def _gla_body(cum_ref, mask_ref, q_ref, k_ref, v_ref, la_ref, o_ref, st_ref, *, heads, dk, dv, scale):
    @pl.when(pl.program_id(2) == 0)
    def _():
        st_ref[...] = jnp.zeros_like(st_ref)

    cum = cum_ref[0]
    c = CHUNK
    for h in range(heads):
        q = q_ref[0, :, h * dk:(h + 1) * dk] * scale
        k = k_ref[0, :, h * dk:(h + 1) * dk]
        v = v_ref[0, :, h * dv:(h + 1) * dv]
        la = la_ref[0, 0, :, h * dk:(h + 1) * dk]
        vb = v.astype(BF16)
        rs = _cumsums(cum, la)
        bq, cc, tot = rs[:c], rs[c:2 * c], rs[2 * c:2 * c + 1]
        sc = None
        for li, s in enumerate(LEVELS):
            if s == 1:
                qs, ks = q * jnp.exp(la), k
            else:
                g = bq - _boundary(cc, s)
                qs = q * jnp.exp(jnp.minimum(g, 0.0))
                ks = k * jnp.exp(jnp.minimum(-g, 0.0))
            p = _nt(qs.astype(BF16), ks.astype(BF16)) * mask_ref[0, li]
            sc = p if sc is None else sc + p
        dg = jnp.sum(q * k, axis=1, keepdims=True)
        st = st_ref[h]
        o = (jnp.dot(sc.astype(BF16), vb, preferred_element_type=F32) + dg * v
             + _nt((q * jnp.exp(bq)).astype(BF16), st.astype(BF16)))
        o_ref[0, 0, :, h * dv:(h + 1) * dv] = o
        kbar = k * jnp.exp(tot - bq)
        st_ref[h] = st * jnp.exp(tot) + _tn(vb, kbar.astype(BF16))


def _ret_body(lam_ref, q_ref, k_ref, v_ref, o_ref, st_ref, *, heads, dk, dv):
    d = pl.program_id(0)

    @pl.when(pl.program_id(2) == 0)
    def _():
        st_ref[...] = jnp.zeros_like(st_ref)

    c = CHUNK
    ri = lax.broadcasted_iota(jnp.int32, (c, c), 0)
    ci = lax.broadcasted_iota(jnp.int32, (c, c), 1)
    dist = jnp.where(d == 0, ri - ci, ci - ri)
    live = dist >= 0
    distf = jnp.maximum(dist, 0).astype(F32)
    row = lax.broadcasted_iota(jnp.int32, (c, 1), 0)
    pos = jnp.where(d == 0, row + 1, c - row).astype(F32)
    for h in range(heads):
        lam = lam_ref[d * heads + h]
        q = q_ref[0, :, h * dk:(h + 1) * dk]
        k = k_ref[0, :, h * dk:(h + 1) * dk]
        v = v_ref[0, :, h * dv:(h + 1) * dv]
        vb = v.astype(BF16)
        decay = jnp.where(live, jnp.exp(lam * distf), 0.0)
        sc = _nt(q.astype(BF16), k.astype(BF16)) * decay
        st = st_ref[h]
        o = (jnp.dot(sc.astype(BF16), vb, preferred_element_type=F32)
             + _nt((q * jnp.exp(lam * pos)).astype(BF16), st.astype(BF16)))
        o_ref[0, 0, :, h * dv:(h + 1) * dv] = o
        kbar = k * jnp.exp(lam * (c - pos))
        st_ref[h] = st * jnp.exp(lam * c) + _tn(vb, kbar.astype(BF16))


def _chunk_index(d, c, n_ctx, n_all):
    back = jnp.where(c < n_ctx, n_ctx - 1 - c, n_ctx + n_all - 1 - c)
    return jnp.where(d == 0, c, back)


def gla_scan(q, k, v, la2, n_ctx):
    bsz, t, _ = q.shape
    heads, dk, dv = GLA_HEADS, GLA_DK, GLA_DV
    n_all = t // CHUNK
    cum, masks = _scan_consts()
    ch = functools.partial(_chunk_index, n_ctx=n_ctx, n_all=n_all)
    body = functools.partial(_gla_body, heads=heads, dk=dk, dv=dv, scale=dk ** -0.5)
    return pl.pallas_call(
        body,
        out_shape=jax.ShapeDtypeStruct((2, bsz, t, heads * dv), F32),
        grid=(2, bsz, n_all),
        in_specs=[
            pl.BlockSpec((1,) + cum.shape[1:], lambda d, b, c: (d, 0, 0)),
            pl.BlockSpec((1,) + masks.shape[1:], lambda d, b, c: (d, 0, 0, 0)),
            pl.BlockSpec((1, CHUNK, heads * dk), lambda d, b, c: (b, ch(d, c), 0)),
            pl.BlockSpec((1, CHUNK, heads * dk), lambda d, b, c: (b, ch(d, c), 0)),
            pl.BlockSpec((1, CHUNK, heads * dv), lambda d, b, c: (b, ch(d, c), 0)),
            pl.BlockSpec((1, 1, CHUNK, heads * dk), lambda d, b, c: (d, b, ch(d, c), 0)),
        ],
        out_specs=pl.BlockSpec((1, 1, CHUNK, heads * dv), lambda d, b, c: (d, b, ch(d, c), 0)),
        scratch_shapes=[pltpu.VMEM((heads, dv, dk), F32)],
        compiler_params=_cparams(("parallel", "parallel", "arbitrary")),
        name="gla_scan",
    )(cum, masks, q, k, v, la2)


def ret_scan(q, k, v, lam, n_ctx):
    bsz, t, _ = q.shape
    heads, dk, dv = RET_HEADS, RET_DK, RET_DV
    n_all = t // CHUNK
    ch = functools.partial(_chunk_index, n_ctx=n_ctx, n_all=n_all)
    body = functools.partial(_ret_body, heads=heads, dk=dk, dv=dv)
    grid_spec = pltpu.PrefetchScalarGridSpec(
        num_scalar_prefetch=1,
        grid=(2, bsz, n_all),
        in_specs=[
            pl.BlockSpec((1, CHUNK, heads * dk), lambda d, b, c, lam: (b, ch(d, c), 0)),
            pl.BlockSpec((1, CHUNK, heads * dk), lambda d, b, c, lam: (b, ch(d, c), 0)),
            pl.BlockSpec((1, CHUNK, heads * dv), lambda d, b, c, lam: (b, ch(d, c), 0)),
        ],
        out_specs=pl.BlockSpec((1, 1, CHUNK, heads * dv), lambda d, b, c, lam: (d, b, ch(d, c), 0)),
        scratch_shapes=[pltpu.VMEM((heads, dv, dk), F32)],
    )
    return pl.pallas_call(
        body,
        out_shape=jax.ShapeDtypeStruct((2, bsz, t, heads * dv), F32),
        grid_spec=grid_spec,
        compiler_params=_cparams(("parallel", "parallel", "arbitrary")),
        name="ret_scan",
    )(lam.reshape(-1).astype(F32), q, k, v)


RWKV_GROUP = 4
RWKV_GW = RWKV_GROUP * RWKV_HEAD


def _rwkv_consts():
    c = CHUNK
    gw = RWKV_GW
    cum = np.zeros((2, 2 * c + 16, c), np.float32)
    tri = np.tril(np.ones((c, c), np.float32))
    mid = c // 2
    cum[0, :c] = tri - tri[mid - 1][None, :]
    cum[0, c:2 * c] = 1.0 - tri
    cum[0, 2 * c + 8:] = tri[mid - 1][None, :]
    cum[1, :c] = tri.T - tri.T[mid][None, :]
    cum[1, c:2 * c] = 1.0 - tri.T
    cum[1, 2 * c + 8:] = tri.T[mid][None, :]
    cum[:, 2 * c:2 * c + 8] = 1.0
    t = np.arange(c)[:, None]
    s = np.tile(np.arange(c), RWKV_GROUP)[None, :]
    masks = np.zeros((2, 3, c, gw), np.float32)
    masks[0, 0], masks[0, 1] = s < t, s <= t
    masks[1, 0], masks[1, 1] = s > t, s >= t
    masks[:, 2] = s == t
    return jnp.asarray(cum, BF16), jnp.asarray(masks)


def _rwkv_body(cum_ref, mask_ref, r_ref, k_ref, v_ref, kk_ref, kka_ref, lw_ref, o_ref, st_ref, *, groups):
    @pl.when(pl.program_id(2) == 0)
    def _():
        st_ref[...] = jnp.zeros_like(st_ref)

    c, gw = CHUNK, RWKV_GW
    cum = cum_ref[0]
    strict, incl, eye = mask_ref[0, 0], mask_ref[0, 1], mask_ref[0, 2]
    rb = lax.broadcasted_iota(jnp.int32, (gw, gw), 0) // RWKV_HEAD
    cb = lax.broadcasted_iota(jnp.int32, (gw, gw), 1) // RWKV_HEAD
    same_head = rb == cb

    def bd(x):
        xb = x.astype(BF16)
        return jnp.where(same_head, jnp.concatenate([xb] * RWKV_GROUP, axis=0), jnp.zeros((), BF16))

    def mm(a, b):
        return jnp.dot(a.astype(BF16), b, preferred_element_type=F32)

    for g in range(groups):
        sl = slice(g * gw, (g + 1) * gw)
        r, k, v = r_ref[0, :, sl], k_ref[0, :, sl], v_ref[0, :, sl]
        kk, kka, lw = kk_ref[0, :, sl], kka_ref[0, :, sl], lw_ref[0, 0, :, sl]
        rs = _cumsums(cum, lw)
        gm, ge, tot, gmid = rs[:c], rs[c:2 * c], rs[2 * c:2 * c + 1], rs[2 * c + 8:2 * c + 9]
        e_in, e_out, e_mid = jnp.exp(gm), jnp.exp(-gm), jnp.exp(gmid)
        kt_rel = kk * (e_in * jnp.exp(-lw))
        r_rel = r * e_in
        lhs = jnp.concatenate([kt_rel, r_rel], axis=0).astype(BF16)
        s_b = _nt(lhs, bd(kka * e_out))
        s_k = _nt(lhs, bd(k * e_out))
        n = -(s_b[:c] * strict)
        p_b, l_k, p_k = s_b[c:] * incl, s_k[:c] * strict, s_k[c:] * incl
        x, p = eye + n, mm(n, bd(n))
        for _ in range(4):
            pp = mm(jnp.concatenate([p, x], axis=0), bd(p))
            p, x = pp[:c], x + pp[c:]
        x = x + mm(x, bd(p))
        st = st_ref[g]
        hs = _nt((lhs.astype(F32) * e_mid).astype(BF16), st.astype(BF16))
        vs = mm(jnp.concatenate([l_k, p_k], axis=0), bd(v))
        u = mm(x, bd(hs[:c] + vs[:c]))
        o_ref[0, 0, :, sl] = hs[c:] + vs[c:] - mm(p_b, bd(u))
        e_end = jnp.exp(ge)
        upd = _tn(jnp.concatenate([v, -u], axis=0).astype(BF16),
                  jnp.concatenate([k * e_end, kka * e_end], axis=0).astype(BF16))
        st_ref[g] = st * jnp.exp(tot) + jnp.where(same_head, upd, 0.0)


def rwkv_scan(r, k, v, kk, kka, lw2, n_ctx):
    bsz, t, ch_w = r.shape
    groups = ch_w // RWKV_GW
    n_all = t // CHUNK
    cum, masks = _rwkv_consts()
    ch = functools.partial(_chunk_index, n_ctx=n_ctx, n_all=n_all)
    tok = pl.BlockSpec((1, CHUNK, ch_w), lambda d, b, c: (b, ch(d, c), 0))
    dir_tok = pl.BlockSpec((1, 1, CHUNK, ch_w), lambda d, b, c: (d, b, ch(d, c), 0))
    return pl.pallas_call(
        functools.partial(_rwkv_body, groups=groups),
        out_shape=jax.ShapeDtypeStruct((2, bsz, t, ch_w), F32),
        grid=(2, bsz, n_all),
        in_specs=[
            pl.BlockSpec((1,) + cum.shape[1:], lambda d, b, c: (d, 0, 0)),
            pl.BlockSpec((1,) + masks.shape[1:], lambda d, b, c: (d, 0, 0, 0)),
            tok, tok, tok, tok, tok, dir_tok,
        ],
        out_specs=dir_tok,
        scratch_shapes=[pltpu.VMEM((groups, RWKV_GW, RWKV_GW), F32)],
        compiler_params=_cparams(("parallel", "parallel", "arbitrary")),
        name="rwkv_scan",
    )(cum, masks, r, k, v, kk, kka, lw2)


def _split2(x):
    hi = x.astype(BF16)
    return hi, (x - hi.astype(F32)).astype(BF16)


def _mm3_body(a_ref, b_ref, o_ref):
    a_hi, a_mid = _split2(a_ref[...])
    b_hi, b_mid = _split2(b_ref[...])
    dot = functools.partial(jnp.dot, preferred_element_type=F32)
    o_ref[...] = dot(a_hi, b_hi) + (dot(a_hi, b_mid) + dot(a_mid, b_hi))


def matmul3(a, b, *, tm, name="mm3"):
    m, k = a.shape
    _, n = b.shape
    return pl.pallas_call(
        _mm3_body,
        out_shape=jax.ShapeDtypeStruct((m, n), F32),
        grid=(m // tm,),
        in_specs=[pl.BlockSpec((tm, k), lambda i: (i, 0)), pl.BlockSpec((k, n), lambda i: (0, 0))],
        out_specs=pl.BlockSpec((tm, n), lambda i: (i, 0)),
        compiler_params=_cparams(("parallel",)),
        name=name,
    )(a, b)


MOE_BM = 256


def _ffn_body(be_ref, bv_ref, x_ref, wg_ref, wu_ref, wd_ref, o_ref):
    i = pl.program_id(0)

    @pl.when(bv_ref[i] != 0)
    def _():
        x = x_ref[...]
        h1 = jnp.dot(x, wg_ref[0], preferred_element_type=F32)
        h2 = jnp.dot(x, wu_ref[0], preferred_element_type=F32)
        a = (h1 * jax.nn.sigmoid(h1) * h2).astype(BF16)
        o_ref[...] = jnp.dot(a, wd_ref[0], preferred_element_type=F32).astype(o_ref.dtype)

    @pl.when(bv_ref[i] == 0)
    def _():
        o_ref[...] = jnp.zeros_like(o_ref)


def swiglu_blocks(x, wg, wu, wd, block_expert, block_valid, *, bm, out_dtype, name):
    n, d = x.shape
    f = wg.shape[2]
    grid_spec = pltpu.PrefetchScalarGridSpec(
        num_scalar_prefetch=2,
        grid=(n // bm,),
        in_specs=[
            pl.BlockSpec((bm, d), lambda i, be, bv: (i, 0)),
            pl.BlockSpec((1, d, f), lambda i, be, bv: (be[i], 0, 0)),
            pl.BlockSpec((1, d, f), lambda i, be, bv: (be[i], 0, 0)),
            pl.BlockSpec((1, f, d), lambda i, be, bv: (be[i], 0, 0)),
        ],
        out_specs=pl.BlockSpec((bm, d), lambda i, be, bv: (i, 0)),
    )
    return pl.pallas_call(
        _ffn_body,
        out_shape=jax.ShapeDtypeStruct((n, d), out_dtype),
        grid_spec=grid_spec,
        compiler_params=_cparams(("arbitrary",)),
        name=name,
    )(block_expert, block_valid, x, wg, wu, wd)


def _tile(m, cap, mult=16):
    return max(t for t in range(mult, cap + 1, mult) if m % t == 0)


def moe_ffn(h, router_w, router_bias, exp_gate, exp_up, exp_down, sh_gate, sh_up, sh_down):
    t, d = h.shape
    e, k, bm = N_EXPERTS, TOP_K, MOE_BM
    rw = jnp.pad(router_w.astype(F32), ((0, 0), (0, 128 - e)))
    scores = jax.nn.sigmoid(matmul3(h, rw, tm=_tile(t, 600), name="router")[:, :e])
    _, idx = lax.top_k(scores + router_bias.astype(F32), k)
    sel = jnp.take_along_axis(scores, idx, 1)
    gates = ROUTED_SCALE * sel / jnp.sum(sel, -1, keepdims=True)
    flat_e = idx.reshape(-1)
    order = jnp.argsort(flat_e)
    e_sorted = flat_e[order]
    counts = jnp.bincount(flat_e, length=e)
    padded = (counts + bm - 1) // bm * bm
    ends = jnp.cumsum(padded)
    dest = (ends - padded)[e_sorted] + jnp.arange(t * k) - (jnp.cumsum(counts) - counts)[e_sorted]
    n_blocks = -(-(t * k) // bm) + e
    buf_tok = jnp.full((n_blocks * bm,), t, jnp.int32).at[dest].set((order // k).astype(jnp.int32))
    pos = jnp.zeros((t * k,), jnp.int32).at[order].set(dest.astype(jnp.int32)).reshape(t, k)
    starts = jnp.arange(n_blocks) * bm
    block_expert = jnp.minimum(jnp.searchsorted(ends, starts, side='right'), e - 1).astype(jnp.int32)
    block_valid = (starts < ends[-1]).astype(jnp.int32)

    hb = h.astype(BF16)
    h_pad = jnp.concatenate([hb, jnp.zeros((1, d), BF16)], 0)
    y = swiglu_blocks(h_pad[buf_tok], exp_gate.astype(BF16), exp_up.astype(BF16), exp_down.astype(BF16),
                      block_expert, block_valid, bm=bm, out_dtype=BF16, name="experts")
    routed = jnp.sum(y[pos].astype(F32) * gates[..., None], axis=1)
    tm = _tile(t, 600)
    ones = jnp.ones((t // tm,), jnp.int32)
    shared = swiglu_blocks(hb, sh_gate.astype(BF16)[None], sh_up.astype(BF16)[None], sh_down.astype(BF16)[None],
                           jnp.zeros((t // tm,), jnp.int32), ones, bm=tm, out_dtype=F32, name="shared")
    return routed + shared


GLA_QK = GLA_HEADS * GLA_DK
GLA_W = GLA_HEADS * GLA_DV
RET_QK = RET_HEADS * RET_DK
RET_W = RET_HEADS * RET_DV
GLA_MAIN = 2 * GLA_QK + 2 * GLA_W
RET_N = 2 * RET_QK + 2 * RET_W
RWKV_N = 3 * RWKV_WIDTH + RWKV_DECAY_RANK + RWKV_ICLR_RANK + RWKV_GATE_RANK
N_IN_PAD = 13056


def _rmsnorm(x, g):
    return x * lax.rsqrt(jnp.mean(x * x, -1, keepdims=True) + RMS_EPS) * g


def _shift_seq(u):
    h = u.shape[-1] // 2
    prev = jnp.pad(u[:, :-1, :h], ((0, 0), (1, 0), (0, 0)))
    nxt = jnp.pad(u[:, 1:, h:], ((0, 0), (0, 1), (0, 0)))
    return jnp.concatenate([prev, nxt], -1)


def _shift_grid(u):
    bsz, t, ch = u.shape
    g = u.reshape(bsz, t // GRID_W, GRID_W, ch)
    q = ch // 4
    left = jnp.pad(g[:, :, :-1, :q], ((0, 0), (0, 0), (1, 0), (0, 0)))
    right = jnp.pad(g[:, :, 1:, q:2 * q], ((0, 0), (0, 0), (0, 1), (0, 0)))
    up = jnp.pad(g[:, :-1, :, 2 * q:3 * q], ((0, 0), (1, 0), (0, 0), (0, 0)))
    down = jnp.pad(g[:, 1:, :, 3 * q:], ((0, 0), (0, 1), (0, 0), (0, 0)))
    return jnp.concatenate([left, right, up, down], -1).reshape(bsz, t, ch)


def _rope(x, pos):
    d = x.shape[-1]
    inv = ROPE_BASE ** (-jnp.linspace(0.0, 1.0, d // 2, dtype=F32))
    ang = pos.astype(F32)[:, None] * inv[None, :]
    cos, sin = jnp.cos(ang)[:, None, :], jnp.sin(ang)[:, None, :]
    x1, x2 = x[..., :d // 2], x[..., d // 2:]
    return jnp.concatenate([x1 * cos - x2 * sin, x1 * sin + x2 * cos], -1)


def _head_rms(o, heads):
    bsz, t, ch = o.shape
    o4 = o.reshape(bsz, t, heads, ch // heads)
    return o4 * lax.rsqrt(jnp.mean(o4 * o4, -1, keepdims=True) + RMS_EPS)


def _small_mm(a, b, name):
    bsz, t, k = a.shape
    m = bsz * t
    return matmul(a.reshape(m, k), b, tm=_tile(m, 1100), tn=b.shape[1], name=name).reshape(bsz, t, -1)


def gla_mixer(zg, lr, a2, a_bias, norm_w, n_ctx, out_from):
    q, k, v, g = (zg[..., :GLA_QK], zg[..., GLA_QK:2 * GLA_QK],
                  zg[..., 2 * GLA_QK:2 * GLA_QK + GLA_W], zg[..., 2 * GLA_QK + GLA_W:])
    lr_pad = jnp.pad(lr, ((0, 0), (0, 0), (0, 128 - GLA_RANK)))
    la2 = jnp.stack([_small_mm(lr_pad, jnp.pad(a2[x], ((0, 128 - GLA_RANK), (0, 0))), "gla_gate")
                     for x in range(2)])
    la2 = jax.nn.log_sigmoid(la2 + a_bias[:, None, None, :]) / GLA_TAU
    o2 = gla_scan(q, k, v, la2, n_ctx)
    o = (o2[0] + o2[1])[:, out_from:]
    bsz, t, _ = o.shape
    o = (_head_rms(o, GLA_HEADS) * norm_w).reshape(bsz, t, GLA_W)
    return o * jax.nn.silu(g[:, out_from:])


def ret_mixer(zr, decay_logit, n_ctx, out_from):
    bsz, t, _ = zr.shape
    q, k, v, g = (zr[..., :RET_QK], zr[..., RET_QK:2 * RET_QK],
                  zr[..., 2 * RET_QK:2 * RET_QK + RET_W], zr[..., 2 * RET_QK + RET_W:])
    pos = jnp.arange(t)
    q = _rope(q.reshape(bsz, t, RET_HEADS, RET_DK), pos).reshape(bsz, t, RET_QK)
    k = (_rope(k.reshape(bsz, t, RET_HEADS, RET_DK), pos) * RET_DK ** -0.5).reshape(bsz, t, RET_QK)
    lam = jax.nn.log_sigmoid(decay_logit.astype(F32))
    o2 = ret_scan(q, k, v, lam, n_ctx)
    o = (o2[0] + o2[1])[:, out_from:]
    o = _head_rms(o, RET_HEADS).reshape(bsz, t - out_from, RET_W)
    return o * jax.nn.silu(g[:, out_from:])


def rwkv_mixer(zw, lc, mu, w0, w2, a0, a2, g2, k_k, k_a, r_k, lnx_w, lnx_b, n_ctx, out_from):
    bsz, t, _ = zw.shape
    cw = RWKV_WIDTH
    shifted = jnp.concatenate([_shift_seq(zw[:, :lc]), _shift_grid(zw[:, lc:])], 1)
    z = zw + mu * (shifted - zw)
    r, k, v = z[..., :cw], z[..., cw:2 * cw], z[..., 2 * cw:3 * cw]
    o3 = 3 * cw
    xw = z[..., o3:o3 + RWKV_DECAY_RANK]
    xa = z[..., o3 + RWKV_DECAY_RANK:o3 + RWKV_DECAY_RANK + RWKV_ICLR_RANK]
    xg = z[..., o3 + RWKV_DECAY_RANK + RWKV_ICLR_RANK:]
    txw = jnp.tanh(xw)
    w_pre = w0[:, None, None, :] + jnp.stack([_small_mm(txw, w2[x], "rwkv_w") for x in range(2)])
    lw2 = -jnp.exp(-jax.nn.softplus(-w_pre) - 0.5)
    a = jax.nn.sigmoid(a0 + _small_mm(xa, a2, "rwkv_a"))
    xg_pad = jnp.pad(jax.nn.sigmoid(xg), ((0, 0), (0, 0), (0, 512 - RWKV_GATE_RANK)))
    g = _small_mm(xg_pad, jnp.pad(g2, ((0, 512 - RWKV_GATE_RANK), (0, 0))), "rwkv_g")

    def per_head(u):
        return u.reshape(bsz, t, RWKV_HEADS, RWKV_HEAD)

    kk = per_head(k * k_k)
    kk = (kk * lax.rsqrt(jnp.sum(kk * kk, -1, keepdims=True) + 1e-12)).reshape(bsz, t, cw)
    k = k * (1.0 + (a - 1.0) * k_a)
    bonus = (jnp.sum(per_head(r * k * r_k.reshape(-1)), -1, keepdims=True) * per_head(v)).reshape(bsz, t, cw)
    o2 = rwkv_scan(r, k, v, kk, kk * a, lw2, n_ctx)
    o = (o2[0] + o2[1])[:, out_from:]
    o4 = o.reshape(bsz, t - out_from, RWKV_HEADS, RWKV_HEAD)
    mean = jnp.mean(o4, -1, keepdims=True)
    var = jnp.mean(jnp.square(o4 - mean), -1, keepdims=True)
    gn = ((o4 - mean) * lax.rsqrt(var + GN_EPS)).reshape(bsz, t - out_from, cw) * lnx_w + lnx_b
    return (gn + bonus[:, out_from:]) * g[:, out_from:]


def kernel(x, c, ctx, c_ctx, ada_w, ada_b, pre_mix, post_mix, pre_ffn, post_ffn, w_in, w_out, gla_a2, gla_a_bias, gla_norm_w, ret_decay_logit, rwkv_mu, rwkv_w0, rwkv_w2, rwkv_a0, rwkv_a2, rwkv_g2, rwkv_k_k, rwkv_k_a, rwkv_r_k, rwkv_lnx_w, rwkv_lnx_b, router_w, router_bias, exp_gate, exp_up, exp_down, shared_gate, shared_up, shared_down):
    bsz, s, d = x.shape
    lc = ctx.shape[1]
    t = lc + s
    depth = ada_w.shape[0]
    n_ctx = lc // CHUNK
    gla_n = GLA_MAIN + GLA_RANK
    for l in range(depth):
        ctx_out = l < depth - 1
        out_from = 0 if ctx_out else lc
        cvec = jnp.concatenate([jax.nn.silu(c), jax.nn.silu(c_ctx)[None], jnp.zeros((8 - bsz - 1, d), F32)], 0)
        mod = matmul(cvec, ada_w[l], tm=8, tn=1024, name="adaln") + ada_b[l]
        mx = [m[:, None, :] for m in jnp.split(mod[:bsz], 6, axis=-1)]
        mc = jnp.split(mod[bsz], 6, axis=-1)

        hx = _rmsnorm(x, pre_mix[l]) * (1.0 + mx[1]) + mx[0]
        hc = _rmsnorm(ctx, pre_mix[l]) * (1.0 + mc[1]) + mc[0]
        h_all = jnp.concatenate([hc, hx], 1).reshape(bsz * t, d).astype(BF16)
        wi = w_in[l]
        wi = jnp.concatenate([wi[:, :GLA_MAIN], wi[:, gla_n:], wi[:, GLA_MAIN:gla_n],
                              jnp.zeros((d, N_IN_PAD - wi.shape[1]), F32)], 1).astype(BF16)
        z = matmul(h_all, wi, tm=_tile(bsz * t, 1100), tn=768, name="w_in").reshape(bsz, t, N_IN_PAD)
        o_ret = GLA_MAIN
        o_rwkv = GLA_MAIN + RET_N
        o_lr = o_rwkv + RWKV_N
        y = jnp.concatenate([
            gla_mixer(z[..., :GLA_MAIN], z[..., o_lr:o_lr + GLA_RANK], gla_a2[l], gla_a_bias[l],
                      gla_norm_w[l], n_ctx, out_from),
            ret_mixer(z[..., o_ret:o_rwkv], ret_decay_logit[l], n_ctx, out_from),
            rwkv_mixer(z[..., o_rwkv:o_lr], lc, rwkv_mu[l], rwkv_w0[l], rwkv_w2[l], rwkv_a0[l], rwkv_a2[l],
                       rwkv_g2[l], rwkv_k_k[l], rwkv_k_a[l], rwkv_r_k[l], rwkv_lnx_w[l], rwkv_lnx_b[l],
                       n_ctx, out_from),
        ], -1)
        t_out = t - out_from
        y = matmul(y.reshape(bsz * t_out, d).astype(BF16), w_out[l], tm=_tile(bsz * t_out, 1100), tn=512,
                   name="w_out").reshape(bsz, t_out, d)
        if ctx_out:
            ctx = ctx + mc[2] * _rmsnorm(y[:, :lc], post_mix[l])
            x = x + mx[2] * _rmsnorm(y[:, lc:], post_mix[l])
        else:
            x = x + mx[2] * _rmsnorm(y, post_mix[l])

        hx = _rmsnorm(x, pre_ffn[l]) * (1.0 + mx[4]) + mx[3]
        if ctx_out:
            hc = _rmsnorm(ctx, pre_ffn[l]) * (1.0 + mc[4]) + mc[3]
            h = jnp.concatenate([hc, hx], 1)
        else:
            h = hx
        f = moe_ffn(h.reshape(-1, d), router_w[l], router_bias[l], exp_gate[l], exp_up[l], exp_down[l],
                    shared_gate[l], shared_up[l], shared_down[l]).reshape(h.shape)
        if ctx_out:
            ctx = ctx + mc[5] * _rmsnorm(f[:, :lc], post_ffn[l])
            x = x + mx[5] * _rmsnorm(f[:, lc:], post_ffn[l])
        else:
            x = x + mx[5] * _rmsnorm(f, post_ffn[l])
    return x
```

```python
import functools

import numpy as np
import jax
import jax.numpy as jnp
from jax import lax
from jax.experimental import pallas as pl
from jax.experimental.pallas import tpu as pltpu

F32 = jnp.float32
BF16 = jnp.bfloat16

CHUNK = 64
LEVELS = (32, 16, 8, 4, 2, 1)

GLA_HEADS, GLA_DK, GLA_DV, GLA_RANK, GLA_TAU = 6, 128, 256, 16, 16.0
RET_HEADS, RET_DK, RET_DV = 5, 128, 256
ROPE_BASE = 10000.0
RWKV_HEADS, RWKV_HEAD = 20, 64
RWKV_WIDTH = RWKV_HEADS * RWKV_HEAD
RWKV_DECAY_RANK, RWKV_ICLR_RANK, RWKV_GATE_RANK = 128, 128, 480
GN_EPS = 64e-5
RMS_EPS = 1e-6
N_EXPERTS, TOP_K, D_EXPERT, ROUTED_SCALE = 64, 8, 384, 2.5
GRID_W = 64

GLA_QK, GLA_W = GLA_HEADS * GLA_DK, GLA_HEADS * GLA_DV
RET_QK, RET_W = RET_HEADS * RET_DK, RET_HEADS * RET_DV
GLA_N = 2 * GLA_QK + 2 * GLA_W + GLA_RANK
RET_N = 2 * RET_QK + 2 * RET_W
RWKV_N = 3 * RWKV_WIDTH + RWKV_DECAY_RANK + RWKV_ICLR_RANK + RWKV_GATE_RANK

COL_GLA_Q, COL_GLA_K, COL_GLA_V, COL_GLA_G = 0, 768, 1536, 3072
COL_RW_G1 = 4608
COL_RET_Q, COL_RET_K, COL_RET_V, COL_RET_G = 5120, 5760, 6400, 7680
COL_RW_R, COL_RW_K, COL_RW_V = 8960, 10240, 11520
COL_RW_W1, COL_RW_A1, COL_GLA_LR = 12800, 12928, 13056
N_IN_PAD = 13312

VMEM_LIMIT = 56 * 1024 * 1024


def _cparams(sem):
    return pltpu.CompilerParams(dimension_semantics=sem, vmem_limit_bytes=VMEM_LIMIT)


def _mm_body(a_ref, b_ref, o_ref):
    o_ref[...] = jnp.dot(a_ref[...].astype(BF16), b_ref[...].astype(BF16),
                         preferred_element_type=F32).astype(o_ref.dtype)


def matmul(a, b, *, tm, tn, out_dtype=F32, name="mm", layer=None):
    m, k = a.shape
    n = b.shape[-1]
    assert m % tm == 0 and n % tn == 0, (a.shape, b.shape, tm, tn)
    if layer is None:
        b_spec = pl.BlockSpec((k, tn), lambda i, j: (0, j))
    else:
        b_spec = pl.BlockSpec((None, k, tn), lambda i, j: (layer, 0, j))
    return pl.pallas_call(
        _mm_body,
        out_shape=jax.ShapeDtypeStruct((m, n), out_dtype),
        grid=(m // tm, n // tn),
        in_specs=[pl.BlockSpec((tm, k), lambda i, j: (i, 0)), b_spec],
        out_specs=pl.BlockSpec((tm, tn), lambda i, j: (i, j)),
        compiler_params=_cparams(("parallel", "arbitrary")),
        name=name,
    )(a, b)


def _scan_consts():
    c = CHUNK
    cum = np.zeros((2, 2 * c + 8, c), np.float32)
    masks = np.zeros((2, len(LEVELS), c, c), np.float32)
    tri = np.tril(np.ones((c, c), np.float32))
    cum[0, :c] = tri
    cum[0, c:2 * c] = tri
    cum[1, :c] = tri.T
    cum[1, c:2 * c] = np.triu(np.ones((c, c), np.float32), 1)
    cum[:, 2 * c:] = 1.0
    for li, s in enumerate(LEVELS):
        for i in range(c):
            for j in range(c):
                if (i & s) and not (j & s) and i // (2 * s) == j // (2 * s):
                    masks[0, li, i, j] = 1.0
        masks[1, li] = masks[0, li].T
    return jnp.asarray(cum, BF16), jnp.asarray(masks)


def _split3(x):
    hi = x.astype(BF16)
    r1 = x - hi.astype(F32)
    mid = r1.astype(BF16)
    lo = (r1 - mid.astype(F32)).astype(BF16)
    return hi, mid, lo


def _cumsums(cum, la):
    d = la.shape[1]
    r = jnp.dot(cum, jnp.concatenate(_split3(la), axis=1), preferred_element_type=F32)
    return r[:, :d] + r[:, d:2 * d] + r[:, 2 * d:]


def _boundary(cc, s):
    c, d = cc.shape
    if s >= 8:
        parts = [jnp.broadcast_to(cc[r0 + s - 1:r0 + s], (2 * s, d)) for r0 in range(0, c, 2 * s)]
        return parts[0] if len(parts) == 1 else jnp.concatenate(parts, axis=0)
    cc3 = cc.reshape(c // 8, 8, d)
    if s == 4:
        return jnp.broadcast_to(cc3[:, 3:4], cc3.shape).reshape(c, d)
    sub = lax.broadcasted_iota(jnp.int32, cc3.shape, 1)
    lo = jnp.broadcast_to(cc3[:, 1:2], cc3.shape)
    hi = jnp.broadcast_to(cc3[:, 5:6], cc3.shape)
    return jnp.where(sub < 4, lo, hi).reshape(c, d)


def _nt(a, b):
    return lax.dot_general(a, b, (((1,), (1,)), ((), ())), preferred_element_type=F32)


def _tn(a, b):
    return lax.dot_general(a, b, (((0,), (0,)), ((), ())), preferred_element_type=F32)


def _gla_body(cum_ref, mask_ref, q_ref, k_ref, v_ref, la_ref, o_ref, st_ref, *, heads, dk, dv, scale):
    @pl.when(pl.program_id(2) == 0)
    def _():
        st_ref[...] = jnp.zeros_like(st_ref)

    cum = cum_ref[0]
    c = CHUNK
    for h in range(heads):
        q = q_ref[0, :, h * dk:(h + 1) * dk] * scale
        k = k_ref[0, :, h * dk:(h + 1) * dk]
        v = v_ref[0, :, h * dv:(h + 1) * dv]
        la = la_ref[0, 0, :, h * dk:(h + 1) * dk]
        vb = v.astype(BF16)
        rs = _cumsums(cum, la)
        bq, cc, tot = rs[:c], rs[c:2 * c], rs[2 * c:2 * c + 1]
        sc = None
        for li, s in enumerate(LEVELS):
            if s == 1:
                qs, ks = q * jnp.exp(la), k
            else:
                g = bq - _boundary(cc, s)
                qs = q * jnp.exp(jnp.minimum(g, 0.0))
                ks = k * jnp.exp(jnp.minimum(-g, 0.0))
            p = _nt(qs.astype(BF16), ks.astype(BF16)) * mask_ref[0, li]
            sc = p if sc is None else sc + p
        dg = jnp.sum(q * k, axis=1, keepdims=True)
        st = st_ref[h]
        o = (jnp.dot(sc.astype(BF16), vb, preferred_element_type=F32) + dg * v
             + _nt((q * jnp.exp(bq)).astype(BF16), st.astype(BF16)))
        o_ref[0, 0, :, h * dv:(h + 1) * dv] = o
        kbar = k * jnp.exp(tot - bq)
        st_ref[h] = st * jnp.exp(tot) + _tn(vb, kbar.astype(BF16))


def _ret_body(lam_ref, q_ref, k_ref, v_ref, o_ref, st_ref, *, heads, dk, dv):
    d = pl.program_id(0)

    @pl.when(pl.program_id(2) == 0)
    def _():
        st_ref[...] = jnp.zeros_like(st_ref)

    c = CHUNK
    ri = lax.broadcasted_iota(jnp.int32, (c, c), 0)
    ci = lax.broadcasted_iota(jnp.int32, (c, c), 1)
    dist = jnp.where(d == 0, ri - ci, ci - ri)
    live = dist >= 0
    distf = jnp.maximum(dist, 0).astype(F32)
    row = lax.broadcasted_iota(jnp.int32, (c, 1), 0)
    pos = jnp.where(d == 0, row + 1, c - row).astype(F32)
    for h in range(heads):
        lam = lam_ref[d * heads + h]
        q = q_ref[0, :, h * dk:(h + 1) * dk]
        k = k_ref[0, :, h * dk:(h + 1) * dk]
        v = v_ref[0, :, h * dv:(h + 1) * dv]
        vb = v.astype(BF16)
        decay = jnp.where(live, jnp.exp(lam * distf), 0.0)
        sc = _nt(q.astype(BF16), k.astype(BF16)) * decay
        st = st_ref[h]
        o = (jnp.dot(sc.astype(BF16), vb, preferred_element_type=F32)
             + _nt((q * jnp.exp(lam * pos)).astype(BF16), st.astype(BF16)))
        o_ref[0, 0, :, h * dv:(h + 1) * dv] = o
        kbar = k * jnp.exp(lam * (c - pos))
        st_ref[h] = st * jnp.exp(lam * c) + _tn(vb, kbar.astype(BF16))


def _chunk_index(d, c, n_ctx, n_all):
    back = jnp.where(c < n_ctx, n_ctx - 1 - c, n_ctx + n_all - 1 - c)
    return jnp.where(d == 0, c, back)


def _col_spec(width, col, ch):
    assert col % width == 0
    return pl.BlockSpec((1, CHUNK, width), lambda d, b, c, *_: (b, ch(d, c), col // width))


def gla_scan(z, la2, n_ctx):
    bsz, t, _ = z.shape
    heads, dk, dv = GLA_HEADS, GLA_DK, GLA_DV
    n_all = t // CHUNK
    cum, masks = _scan_consts()
    ch = functools.partial(_chunk_index, n_ctx=n_ctx, n_all=n_all)
    body = functools.partial(_gla_body, heads=heads, dk=dk, dv=dv, scale=dk ** -0.5)
    return pl.pallas_call(
        body,
        out_shape=jax.ShapeDtypeStruct((2, bsz, t, heads * dv), F32),
        grid=(2, bsz, n_all),
        in_specs=[
            pl.BlockSpec((1,) + cum.shape[1:], lambda d, b, c: (d, 0, 0)),
            pl.BlockSpec((1,) + masks.shape[1:], lambda d, b, c: (d, 0, 0, 0)),
            _col_spec(heads * dk, COL_GLA_Q, ch),
            _col_spec(heads * dk, COL_GLA_K, ch),
            _col_spec(heads * dv, COL_GLA_V, ch),
            pl.BlockSpec((1, 1, CHUNK, heads * dk), lambda d, b, c: (d, b, ch(d, c), 0)),
        ],
        out_specs=pl.BlockSpec((1, 1, CHUNK, heads * dv), lambda d, b, c: (d, b, ch(d, c), 0)),
        scratch_shapes=[pltpu.VMEM((heads, dv, dk), F32)],
        compiler_params=_cparams(("parallel", "parallel", "arbitrary")),
        name="gla_scan",
    )(cum, masks, z, z, z, la2)


def ret_scan(q, k, z, lam, n_ctx):
    bsz, t, _ = q.shape
    heads, dk, dv = RET_HEADS, RET_DK, RET_DV
    n_all = t // CHUNK
    ch = functools.partial(_chunk_index, n_ctx=n_ctx, n_all=n_all)
    body = functools.partial(_ret_body, heads=heads, dk=dk, dv=dv)
    grid_spec = pltpu.PrefetchScalarGridSpec(
        num_scalar_prefetch=1,
        grid=(2, bsz, n_all),
        in_specs=[
            pl.BlockSpec((1, CHUNK, heads * dk), lambda d, b, c, lam: (b, ch(d, c), 0)),
            pl.BlockSpec((1, CHUNK, heads * dk), lambda d, b, c, lam: (b, ch(d, c), 0)),
            _col_spec(heads * dv, COL_RET_V, ch),
        ],
        out_specs=pl.BlockSpec((1, 1, CHUNK, heads * dv), lambda d, b, c, lam: (d, b, ch(d, c), 0)),
        scratch_shapes=[pltpu.VMEM((heads, dv, dk), F32)],
    )
    return pl.pallas_call(
        body,
        out_shape=jax.ShapeDtypeStruct((2, bsz, t, heads * dv), F32),
        grid_spec=grid_spec,
        compiler_params=_cparams(("parallel", "parallel", "arbitrary")),
        name="ret_scan",
    )(lam.reshape(-1).astype(F32), q, k, z)


RWKV_GROUP = 4
RWKV_GW = RWKV_GROUP * RWKV_HEAD


def _rwkv_consts():
    c = CHUNK
    gw = RWKV_GW
    cum = np.zeros((2, 2 * c + 16, c), np.float32)
    tri = np.tril(np.ones((c, c), np.float32))
    mid = c // 2
    cum[0, :c] = tri - tri[mid - 1][None, :]
    cum[0, c:2 * c] = 1.0 - tri
    cum[0, 2 * c + 8:] = tri[mid - 1][None, :]
    cum[1, :c] = tri.T - tri.T[mid][None, :]
    cum[1, c:2 * c] = 1.0 - tri.T
    cum[1, 2 * c + 8:] = tri.T[mid][None, :]
    cum[:, 2 * c:2 * c + 8] = 1.0
    t = np.arange(c)[:, None]
    s = np.tile(np.arange(c), RWKV_GROUP)[None, :]
    masks = np.zeros((2, 3, c, gw), np.float32)
    masks[0, 0], masks[0, 1] = s < t, s <= t
    masks[1, 0], masks[1, 1] = s > t, s >= t
    masks[:, 2] = s == t
    return jnp.asarray(cum, BF16), jnp.asarray(masks)


def _rwkv_body(cum_ref, mask_ref, r_ref, k_ref, v_ref, kk_ref, kka_ref, lw_ref, o_ref, st_ref, *, groups):
    @pl.when(pl.program_id(2) == 0)
    def _():
        st_ref[...] = jnp.zeros_like(st_ref)

    c, gw = CHUNK, RWKV_GW
    cum = cum_ref[0]
    strict, incl, eye = mask_ref[0, 0], mask_ref[0, 1], mask_ref[0, 2]
    rb = lax.broadcasted_iota(jnp.int32, (gw, gw), 0) // RWKV_HEAD
    cb = lax.broadcasted_iota(jnp.int32, (gw, gw), 1) // RWKV_HEAD
    same_head = rb == cb

    def bd(x):
        xb = x.astype(BF16)
        return jnp.where(same_head, jnp.concatenate([xb] * RWKV_GROUP, axis=0), jnp.zeros((), BF16))

    def mm(a, b):
        return jnp.dot(a.astype(BF16), b, preferred_element_type=F32)

    for g in range(groups):
        sl = slice(g * gw, (g + 1) * gw)
        r, k, v = r_ref[0, :, sl], k_ref[0, :, sl], v_ref[0, :, sl]
        kk, kka, lw = kk_ref[0, :, sl], kka_ref[0, :, sl], lw_ref[0, 0, :, sl]
        rs = _cumsums(cum, lw)
        gm, ge, tot, gmid = rs[:c], rs[c:2 * c], rs[2 * c:2 * c + 1], rs[2 * c + 8:2 * c + 9]
        e_in, e_out, e_mid = jnp.exp(gm), jnp.exp(-gm), jnp.exp(gmid)
        kt_rel = kk * (e_in * jnp.exp(-lw))
        r_rel = r * e_in
        lhs = jnp.concatenate([kt_rel, r_rel], axis=0).astype(BF16)
        s_b = _nt(lhs, bd(kka * e_out))
        s_k = _nt(lhs, bd(k * e_out))
        n = -(s_b[:c] * strict)
        p_b, l_k, p_k = s_b[c:] * incl, s_k[:c] * strict, s_k[c:] * incl
        x, p = eye + n, mm(n, bd(n))
        for _ in range(4):
            pp = mm(jnp.concatenate([p, x], axis=0), bd(p))
            p, x = pp[:c], x + pp[c:]
        x = x + mm(x, bd(p))
        st = st_ref[g]
        hs = _nt((lhs.astype(F32) * e_mid).astype(BF16), st.astype(BF16))
        vs = mm(jnp.concatenate([l_k, p_k], axis=0), bd(v))
        u = mm(x, bd(hs[:c] + vs[:c]))
        o_ref[0, 0, :, sl] = hs[c:] + vs[c:] - mm(p_b, bd(u))
        e_end = jnp.exp(ge)
        upd = _tn(jnp.concatenate([v, -u], axis=0).astype(BF16),
                  jnp.concatenate([k * e_end, kka * e_end], axis=0).astype(BF16))
        st_ref[g] = st * jnp.exp(tot) + jnp.where(same_head, upd, 0.0)


def rwkv_scan(r, k, v, kk, kka, lw2, n_ctx):
    bsz, t, ch_w = r.shape
    groups = ch_w // RWKV_GW
    n_all = t // CHUNK
    cum, masks = _rwkv_consts()
    ch = functools.partial(_chunk_index, n_ctx=n_ctx, n_all=n_all)
    tok = pl.BlockSpec((1, CHUNK, ch_w), lambda d, b, c: (b, ch(d, c), 0))
    dir_tok = pl.BlockSpec((1, 1, CHUNK, ch_w), lambda d, b, c: (d, b, ch(d, c), 0))
    return pl.pallas_call(
        functools.partial(_rwkv_body, groups=groups),
        out_shape=jax.ShapeDtypeStruct((2, bsz, t, ch_w), F32),
        grid=(2, bsz, n_all),
        in_specs=[
            pl.BlockSpec((1,) + cum.shape[1:], lambda d, b, c: (d, 0, 0)),
            pl.BlockSpec((1,) + masks.shape[1:], lambda d, b, c: (d, 0, 0, 0)),
            tok, tok, tok, tok, tok, dir_tok,
        ],
        out_specs=dir_tok,
        scratch_shapes=[pltpu.VMEM((groups, RWKV_GW, RWKV_GW), F32)],
        compiler_params=_cparams(("parallel", "parallel", "arbitrary")),
        name="rwkv_scan",
    )(cum, masks, r, k, v, kk, kka, lw2)


def _split2(x):
    hi = x.astype(BF16)
    return hi, (x - hi.astype(F32)).astype(BF16)


def _mm3_body(a_ref, b_ref, o_ref):
    a_hi, a_mid = _split2(a_ref[...])
    b_hi, b_mid = _split2(b_ref[...])
    dot = functools.partial(jnp.dot, preferred_element_type=F32)
    o_ref[...] = dot(a_hi, b_hi) + (dot(a_hi, b_mid) + dot(a_mid, b_hi))


def matmul3(a, b, *, tm, name="mm3"):
    m, k = a.shape
    _, n = b.shape
    return pl.pallas_call(
        _mm3_body,
        out_shape=jax.ShapeDtypeStruct((m, n), F32),
        grid=(m // tm,),
        in_specs=[pl.BlockSpec((tm, k), lambda i: (i, 0)), pl.BlockSpec((k, n), lambda i: (0, 0))],
        out_specs=pl.BlockSpec((tm, n), lambda i: (i, 0)),
        compiler_params=_cparams(("parallel",)),
        name=name,
    )(a, b)


MOE_BM = 256


def _tile(m, cap, mult=16):
    return max(t for t in range(mult, cap + 1, mult) if m % t == 0)


def _swiglu(x, wg, wu, wd, gate=None):
    h1 = jnp.dot(x, wg, preferred_element_type=F32)
    h2 = jnp.dot(x, wu, preferred_element_type=F32)
    a = h1 * jax.nn.sigmoid(h1) * h2
    if gate is not None:
        a = a * gate
    return jnp.dot(a.astype(BF16), wd, preferred_element_type=F32)


def _cast_rows(src, dst, rows):
    def body(j, carry):
        r = pl.multiple_of(j * rows, rows)
        dst[pl.ds(r, rows), :] = src[pl.ds(r, rows), :].astype(BF16)
        return carry
    lax.fori_loop(0, src.shape[0] // rows, body, 0)


def _experts_body(be_ref, bv_ref, nx_ref, tok_ref, h_hbm, gate_ref, wg_hbm, wu_hbm, wd_hbm, o_ref,
                  xbuf, st_g, st_u, st_d, wg_b, wu_b, wd_b, xsem, wsem, *, layer, bm):
    i = pl.program_id(0)
    n = pl.num_programs(0)
    slot = lax.rem(i, 2)

    def gather(blk, s):
        base = blk * bm

        def body(r, carry):
            tok = tok_ref[base + r]
            pltpu.make_async_copy(h_hbm.at[pl.ds(tok, 1)], xbuf.at[s, pl.ds(r, 1)], xsem.at[s]).start()
            return carry
        lax.fori_loop(0, bm, body, 0, unroll=8)

    def weight_copies(e):
        return (pltpu.make_async_copy(wg_hbm.at[layer, e], st_g, wsem.at[0]),
                pltpu.make_async_copy(wu_hbm.at[layer, e], st_u, wsem.at[1]),
                pltpu.make_async_copy(wd_hbm.at[layer, e], st_d, wsem.at[2]))

    @pl.when(i == 0)
    def _():
        gather(0, 0)
        for cp in weight_copies(be_ref[0]):
            cp.start()

    @pl.when(i + 1 < n)
    def _():
        gather(i + 1, 1 - slot)

    @pl.when((i == 0) | (be_ref[i] != be_ref[jnp.maximum(i - 1, 0)]))
    def _():
        for cp in weight_copies(be_ref[i]):
            cp.wait()
        _cast_rows(st_g, wg_b, 256)
        _cast_rows(st_u, wu_b, 256)
        _cast_rows(st_d, wd_b, 32)

        @pl.when(nx_ref[i] >= 0)
        def _():
            for cp in weight_copies(nx_ref[i]):
                cp.start()

    pltpu.make_async_copy(h_hbm.at[pl.ds(0, bm)], xbuf.at[slot], xsem.at[slot]).wait()

    @pl.when(bv_ref[i] != 0)
    def _():
        x = xbuf[slot].astype(BF16)
        o_ref[...] = _swiglu(x, wg_b[...], wu_b[...], wd_b[...], gate_ref[...])

    @pl.when(bv_ref[i] == 0)
    def _():
        o_ref[...] = jnp.zeros_like(o_ref)


def experts(h_pad, gate_rows, exp_gate, exp_up, exp_down, layer, block_expert, block_valid, next_expert, buf_tok):
    bm = MOE_BM
    n = buf_tok.shape[0]
    d = h_pad.shape[1]
    f = exp_gate.shape[-1]
    any_spec = pl.BlockSpec(memory_space=pl.ANY)
    grid_spec = pltpu.PrefetchScalarGridSpec(
        num_scalar_prefetch=4,
        grid=(n // bm,),
        in_specs=[any_spec, pl.BlockSpec((bm, 1), lambda i, *_: (i, 0)), any_spec, any_spec, any_spec],
        out_specs=pl.BlockSpec((bm, d), lambda i, *_: (i, 0)),
        scratch_shapes=[
            pltpu.VMEM((2, bm, d), F32),
            pltpu.VMEM((d, f), F32), pltpu.VMEM((d, f), F32), pltpu.VMEM((f, d), F32),
            pltpu.VMEM((d, f), BF16), pltpu.VMEM((d, f), BF16), pltpu.VMEM((f, d), BF16),
            pltpu.SemaphoreType.DMA((2,)), pltpu.SemaphoreType.DMA((3,)),
        ],
    )
    return pl.pallas_call(
        functools.partial(_experts_body, layer=layer, bm=bm),
        out_shape=jax.ShapeDtypeStruct((n, d), F32),
        grid_spec=grid_spec,
        compiler_params=_cparams(("arbitrary",)),
        name="experts",
    )(block_expert, block_valid, next_expert, buf_tok, h_pad, gate_rows, exp_gate, exp_up, exp_down)


def _shared_body(x_ref, wg_ref, wu_ref, wd_ref, o_ref):
    o_ref[...] = _swiglu(x_ref[...].astype(BF16), wg_ref[...].astype(BF16), wu_ref[...].astype(BF16),
                         wd_ref[...].astype(BF16))


def shared_expert(h, sh_gate, sh_up, sh_down, layer):
    t, d = h.shape
    f = sh_gate.shape[-1]
    tm = _tile(t, 300)
    once = pl.Buffered(1)
    return pl.pallas_call(
        _shared_body,
        out_shape=jax.ShapeDtypeStruct((t, d), F32),
        grid=(t // tm,),
        in_specs=[pl.BlockSpec((tm, d), lambda i: (i, 0)),
                  pl.BlockSpec((None, d, f), lambda i: (layer, 0, 0), pipeline_mode=once),
                  pl.BlockSpec((None, d, f), lambda i: (layer, 0, 0), pipeline_mode=once),
                  pl.BlockSpec((None, f, d), lambda i: (layer, 0, 0), pipeline_mode=once)],
        out_specs=pl.BlockSpec((tm, d), lambda i: (i, 0)),
        compiler_params=_cparams(("arbitrary",)),
        name="shared",
    )(h, sh_gate, sh_up, sh_down)


COMBINE_TB = 64


def _combine_body(pos_ref, y_hbm, sh_ref, o_ref, buf, sem, *, tb, k):
    i = pl.program_id(0)
    n = pl.num_programs(0)
    slot = lax.rem(i, 2)

    def gather(blk, s):
        base = blk * tb * k

        def body(t, carry):
            for kk in range(k):
                p = pos_ref[base + t * k + kk]
                pltpu.make_async_copy(y_hbm.at[pl.ds(p, 1)], buf.at[s, pl.ds(kk * tb + t, 1)], sem.at[s]).start()
            return carry
        lax.fori_loop(0, tb, body, 0)

    @pl.when(i == 0)
    def _():
        gather(0, 0)

    @pl.when(i + 1 < n)
    def _():
        gather(i + 1, 1 - slot)

    pltpu.make_async_copy(y_hbm.at[pl.ds(0, tb * k)], buf.at[slot], sem.at[slot]).wait()
    acc = sh_ref[...]
    for kk in range(k):
        acc = acc + buf[slot, kk * tb:(kk + 1) * tb, :]
    o_ref[...] = acc


def combine(y, pos, shared):
    t, k = pos.shape
    d = y.shape[1]
    tb = COMBINE_TB
    grid_spec = pltpu.PrefetchScalarGridSpec(
        num_scalar_prefetch=1,
        grid=(t // tb,),
        in_specs=[pl.BlockSpec(memory_space=pl.ANY), pl.BlockSpec((tb, d), lambda i, *_: (i, 0))],
        out_specs=pl.BlockSpec((tb, d), lambda i, *_: (i, 0)),
        scratch_shapes=[pltpu.VMEM((2, tb * k, d), F32), pltpu.SemaphoreType.DMA((2,))],
    )
    return pl.pallas_call(
        functools.partial(_combine_body, tb=tb, k=k),
        out_shape=jax.ShapeDtypeStruct((t, d), F32),
        grid_spec=grid_spec,
        compiler_params=_cparams(("arbitrary",)),
        name="combine",
    )(pos.reshape(-1), y, shared)


def moe_ffn(h, layer, router_w, router_bias, exp_gate, exp_up, exp_down, sh_gate, sh_up, sh_down):
    t, d = h.shape
    e, k, bm = N_EXPERTS, TOP_K, MOE_BM
    i32 = jnp.int32
    rw = jnp.pad(router_w[layer].astype(F32), ((0, 0), (0, 128 - e)))
    scores = jax.nn.sigmoid(matmul3(h, rw, tm=_tile(t, 600), name="router")[:, :e])
    _, idx = lax.top_k(scores + router_bias[layer].astype(F32), k)
    sel = jnp.take_along_axis(scores, idx, 1)
    gates = ROUTED_SCALE * sel / jnp.sum(sel, -1, keepdims=True)
    flat_e = idx.reshape(-1).astype(i32)
    order = jnp.argsort(flat_e).astype(i32)
    rank = jnp.argsort(order).astype(i32)
    counts = jnp.bincount(flat_e, length=e).astype(i32)
    first = jnp.cumsum(counts) - counts
    padded = (counts + bm - 1) // bm * bm
    ends = jnp.cumsum(padded)
    starts = ends - padded
    pos = (starts[flat_e] + rank - first[flat_e]).reshape(t, k)
    n_blocks = -(-(t * k) // bm) + e
    blk_start = jnp.arange(n_blocks, dtype=i32) * bm
    block_valid = (blk_start < ends[-1]).astype(i32)
    last_e = jnp.max(jnp.where(counts > 0, jnp.arange(e, dtype=i32), 0))
    block_expert = jnp.where(block_valid > 0,
                             jnp.minimum(jnp.searchsorted(ends, blk_start, side='right').astype(i32), e - 1), last_e)
    present = jnp.where(counts > 0, jnp.arange(e, dtype=i32), e)
    nxt = lax.cummin(jnp.concatenate([present[1:], jnp.full((1,), e, i32)]), reverse=True)
    next_expert = jnp.where(nxt < e, nxt, -1)[block_expert]
    row = jnp.arange(n_blocks * bm, dtype=i32)
    row_e = jnp.repeat(block_expert, bm)
    row_r = row - starts[row_e]
    live = (row_r < counts[row_e]) & (jnp.repeat(block_valid, bm) > 0)
    src = order[jnp.clip(first[row_e] + row_r, 0, t * k - 1)]
    buf_tok = jnp.where(live, src // k, t).astype(i32)
    gate_rows = jnp.where(live, gates.reshape(-1)[src], 0.0)[:, None]

    h_pad = jnp.concatenate([h, jnp.zeros((1, d), F32)], 0)
    y = experts(h_pad, gate_rows, exp_gate, exp_up, exp_down, layer, block_expert, block_valid, next_expert, buf_tok)
    return combine(y, pos, shared_expert(h, sh_gate, sh_up, sh_down, layer))


def _permute_cols(w):
    gla, ret, rw = w[..., :GLA_N], w[..., GLA_N:GLA_N + RET_N], w[..., GLA_N + RET_N:]
    rkv = 3 * RWKV_WIDTH
    lora = rkv + RWKV_DECAY_RANK + RWKV_ICLR_RANK

    def zeros(n):
        return jnp.zeros(w.shape[:-1] + (n,), w.dtype)

    parts = [gla[..., :GLA_N - GLA_RANK],
             rw[..., lora:], zeros(COL_RET_Q - COL_RW_G1 - RWKV_GATE_RANK),
             ret,
             rw[..., :lora],
             gla[..., GLA_N - GLA_RANK:], zeros(N_IN_PAD - COL_GLA_LR - GLA_RANK)]
    return jnp.concatenate(parts, -1)


def _rmsnorm(x, g):
    return x * lax.rsqrt(jnp.mean(x * x, -1, keepdims=True) + RMS_EPS) * g


def _shift_seq(u):
    h = u.shape[-1] // 2
    prev = jnp.pad(u[:, :-1, :h], ((0, 0), (1, 0), (0, 0)))
    nxt = jnp.pad(u[:, 1:, h:], ((0, 0), (0, 1), (0, 0)))
    return jnp.concatenate([prev, nxt], -1)


def _shift_grid(u):
    bsz, t, ch = u.shape
    g = u.reshape(bsz, t // GRID_W, GRID_W, ch)
    q = ch // 4
    left = jnp.pad(g[:, :, :-1, :q], ((0, 0), (0, 0), (1, 0), (0, 0)))
    right = jnp.pad(g[:, :, 1:, q:2 * q], ((0, 0), (0, 0), (0, 1), (0, 0)))
    up = jnp.pad(g[:, :-1, :, 2 * q:3 * q], ((0, 0), (1, 0), (0, 0), (0, 0)))
    down = jnp.pad(g[:, 1:, :, 3 * q:], ((0, 0), (0, 1), (0, 0), (0, 0)))
    return jnp.concatenate([left, right, up, down], -1).reshape(bsz, t, ch)


def _rope(x, pos):
    d = x.shape[-1]
    inv = ROPE_BASE ** (-jnp.linspace(0.0, 1.0, d // 2, dtype=F32))
    ang = pos.astype(F32)[:, None] * inv[None, :]
    cos, sin = jnp.cos(ang)[:, None, :], jnp.sin(ang)[:, None, :]
    x1, x2 = x[..., :d // 2], x[..., d // 2:]
    return jnp.concatenate([x1 * cos - x2 * sin, x1 * sin + x2 * cos], -1)


def _head_rms(o, heads):
    bsz, t, ch = o.shape
    o4 = o.reshape(bsz, t, heads, ch // heads)
    return o4 * lax.rsqrt(jnp.mean(o4 * o4, -1, keepdims=True) + RMS_EPS)


def _small_mm(a, b, name):
    bsz, t, k = a.shape
    m = bsz * t
    return matmul(a.reshape(m, k), b, tm=_tile(m, 1100), tn=b.shape[1], name=name).reshape(bsz, t, -1)


def gla_mixer(z, a2, a_bias, norm_w, n_ctx, out_from):
    lr_pad = z[..., COL_GLA_LR:COL_GLA_LR + 128]
    la2 = jnp.stack([_small_mm(lr_pad, jnp.pad(a2[x], ((0, 128 - GLA_RANK), (0, 0))), "gla_gate")
                     for x in range(2)])
    la2 = jax.nn.log_sigmoid(la2 + a_bias[:, None, None, :]) / GLA_TAU
    o2 = gla_scan(z, la2, n_ctx)
    o = (o2[0] + o2[1])[:, out_from:]
    bsz, t, _ = o.shape
    o = (_head_rms(o, GLA_HEADS) * norm_w).reshape(bsz, t, GLA_W)
    return o * jax.nn.silu(z[:, out_from:, COL_GLA_G:COL_GLA_G + GLA_W])


def ret_mixer(z, decay_logit, n_ctx, out_from):
    bsz, t, _ = z.shape
    q, k = z[..., COL_RET_Q:COL_RET_Q + RET_QK], z[..., COL_RET_K:COL_RET_K + RET_QK]
    pos = jnp.arange(t)
    q = _rope(q.reshape(bsz, t, RET_HEADS, RET_DK), pos).reshape(bsz, t, RET_QK)
    k = (_rope(k.reshape(bsz, t, RET_HEADS, RET_DK), pos) * RET_DK ** -0.5).reshape(bsz, t, RET_QK)
    lam = jax.nn.log_sigmoid(decay_logit.astype(F32))
    o2 = ret_scan(q, k, z, lam, n_ctx)
    o = (o2[0] + o2[1])[:, out_from:]
    o = _head_rms(o, RET_HEADS).reshape(bsz, t - out_from, RET_W)
    return o * jax.nn.silu(z[:, out_from:, COL_RET_G:COL_RET_G + RET_W])


def rwkv_mixer(z_all, lc, mu, w0, w2, a0, a2, g2, k_k, k_a, r_k, lnx_w, lnx_b, n_ctx, out_from):
    bsz, t, _ = z_all.shape
    cw = RWKV_WIDTH
    zw = jnp.concatenate([z_all[..., COL_RW_R:COL_RW_A1 + RWKV_ICLR_RANK],
                          z_all[..., COL_RW_G1:COL_RW_G1 + RWKV_GATE_RANK]], -1)
    shifted = jnp.concatenate([_shift_seq(zw[:, :lc]), _shift_grid(zw[:, lc:])], 1)
    z = zw + mu * (shifted - zw)
    r, k, v = z[..., :cw], z[..., cw:2 * cw], z[..., 2 * cw:3 * cw]
    o3 = 3 * cw
    xw = z[..., o3:o3 + RWKV_DECAY_RANK]
    xa = z[..., o3 + RWKV_DECAY_RANK:o3 + RWKV_DECAY_RANK + RWKV_ICLR_RANK]
    xg = z[..., o3 + RWKV_DECAY_RANK + RWKV_ICLR_RANK:]
    txw = jnp.tanh(xw)
    w_pre = w0[:, None, None, :] + jnp.stack([_small_mm(txw, w2[x], "rwkv_w") for x in range(2)])
    lw2 = -jnp.exp(-jax.nn.softplus(-w_pre) - 0.5)
    a = jax.nn.sigmoid(a0 + _small_mm(xa, a2, "rwkv_a"))
    xg_pad = jnp.pad(jax.nn.sigmoid(xg), ((0, 0), (0, 0), (0, 512 - RWKV_GATE_RANK)))
    g = _small_mm(xg_pad, jnp.pad(g2, ((0, 512 - RWKV_GATE_RANK), (0, 0))), "rwkv_g")

    def per_head(u):
        return u.reshape(bsz, t, RWKV_HEADS, RWKV_HEAD)

    kk = per_head(k * k_k)
    kk = (kk * lax.rsqrt(jnp.sum(kk * kk, -1, keepdims=True) + 1e-12)).reshape(bsz, t, cw)
    k = k * (1.0 + (a - 1.0) * k_a)
    bonus = (jnp.sum(per_head(r * k * r_k.reshape(-1)), -1, keepdims=True) * per_head(v)).reshape(bsz, t, cw)
    o2 = rwkv_scan(r, k, v, kk, kk * a, lw2, n_ctx)
    o = (o2[0] + o2[1])[:, out_from:]
    o4 = o.reshape(bsz, t - out_from, RWKV_HEADS, RWKV_HEAD)
    mean = jnp.mean(o4, -1, keepdims=True)
    var = jnp.mean(jnp.square(o4 - mean), -1, keepdims=True)
    gn = ((o4 - mean) * lax.rsqrt(var + GN_EPS)).reshape(bsz, t - out_from, cw) * lnx_w + lnx_b
    return (gn + bonus[:, out_from:]) * g[:, out_from:]


def kernel(x, c, ctx, c_ctx, ada_w, ada_b, pre_mix, post_mix, pre_ffn, post_ffn, w_in, w_out, gla_a2, gla_a_bias, gla_norm_w, ret_decay_logit, rwkv_mu, rwkv_w0, rwkv_w2, rwkv_a0, rwkv_a2, rwkv_g2, rwkv_k_k, rwkv_k_a, rwkv_r_k, rwkv_lnx_w, rwkv_lnx_b, router_w, router_bias, exp_gate, exp_up, exp_down, shared_gate, shared_up, shared_down):
    bsz, s, d = x.shape
    lc = ctx.shape[1]
    t = lc + s
    depth = ada_w.shape[0]
    n_ctx = lc // CHUNK
    for l in range(depth):
        ctx_out = l < depth - 1
        out_from = 0 if ctx_out else lc
        cvec = jnp.concatenate([jax.nn.silu(c), jax.nn.silu(c_ctx)[None], jnp.zeros((8 - bsz - 1, d), F32)], 0)
        mod = matmul(cvec, ada_w, tm=8, tn=1024, name="adaln", layer=l) + ada_b[l]
        mx = [m[:, None, :] for m in jnp.split(mod[:bsz], 6, axis=-1)]
        mc = jnp.split(mod[bsz], 6, axis=-1)

        hx = _rmsnorm(x, pre_mix[l]) * (1.0 + mx[1]) + mx[0]
        hc = _rmsnorm(ctx, pre_mix[l]) * (1.0 + mc[1]) + mc[0]
        h_all = jnp.concatenate([hc, hx], 1).reshape(bsz * t, d).astype(BF16)
        wi = _permute_cols(w_in[l]).astype(BF16)
        z = matmul(h_all, wi, tm=_tile(bsz * t, 1100), tn=1024, name="w_in").reshape(bsz, t, N_IN_PAD)
        y = jnp.concatenate([
            gla_mixer(z, gla_a2[l], gla_a_bias[l], gla_norm_w[l], n_ctx, out_from),
            ret_mixer(z, ret_decay_logit[l], n_ctx, out_from),
            rwkv_mixer(z, lc, rwkv_mu[l], rwkv_w0[l], rwkv_w2[l], rwkv_a0[l], rwkv_a2[l],
                       rwkv_g2[l], rwkv_k_k[l], rwkv_k_a[l], rwkv_r_k[l], rwkv_lnx_w[l], rwkv_lnx_b[l],
                       n_ctx, out_from),
        ], -1)
        t_out = t - out_from
        y = matmul(y.reshape(bsz * t_out, d).astype(BF16), w_out, tm=_tile(bsz * t_out, 1100), tn=512,
                   name="w_out", layer=l).reshape(bsz, t_out, d)
        if ctx_out:
            ctx = ctx + mc[2] * _rmsnorm(y[:, :lc], post_mix[l])
            x = x + mx[2] * _rmsnorm(y[:, lc:], post_mix[l])
        else:
            x = x + mx[2] * _rmsnorm(y, post_mix[l])

        hx = _rmsnorm(x, pre_ffn[l]) * (1.0 + mx[4]) + mx[3]
        if ctx_out:
            hc = _rmsnorm(ctx, pre_ffn[l]) * (1.0 + mc[4]) + mc[3]
            h = jnp.concatenate([hc, hx], 1)
        else:
            h = hx
        f = moe_ffn(h.reshape(-1, d), l, router_w, router_bias, exp_gate, exp_up, exp_down,
                    shared_gate, shared_up, shared_down).reshape(h.shape)
        if ctx_out:
            ctx = ctx + mc[5] * _rmsnorm(f[:, :lc], post_ffn[l])
            x = x + mx[5] * _rmsnorm(f[:, lc:], post_ffn[l])
        else:
            x = x + mx[5] * _rmsnorm(f, post_ffn[l])
    return x
```

```python
import functools

import numpy as np
import jax
import jax.numpy as jnp
from jax import lax
from jax.experimental import pallas as pl
from jax.experimental.pallas import tpu as pltpu

F32 = jnp.float32
BF16 = jnp.bfloat16

CHUNK = 64
LEVELS = (32, 16, 8, 4, 2, 1)

GLA_HEADS, GLA_DK, GLA_DV, GLA_RANK, GLA_TAU = 6, 128, 256, 16, 16.0
RET_HEADS, RET_DK, RET_DV = 5, 128, 256
ROPE_BASE = 10000.0
RWKV_HEADS, RWKV_HEAD = 20, 64
RWKV_WIDTH = RWKV_HEADS * RWKV_HEAD
RWKV_DECAY_RANK, RWKV_ICLR_RANK, RWKV_GATE_RANK = 128, 128, 480
GN_EPS = 64e-5
RMS_EPS = 1e-6
N_EXPERTS, TOP_K, D_EXPERT, ROUTED_SCALE = 64, 8, 384, 2.5
GRID_W = 64

GLA_QK, GLA_W = GLA_HEADS * GLA_DK, GLA_HEADS * GLA_DV
RET_QK, RET_W = RET_HEADS * RET_DK, RET_HEADS * RET_DV
GLA_N = 2 * GLA_QK + 2 * GLA_W + GLA_RANK
RET_N = 2 * RET_QK + 2 * RET_W
RWKV_N = 3 * RWKV_WIDTH + RWKV_DECAY_RANK + RWKV_ICLR_RANK + RWKV_GATE_RANK

COL_GLA_Q, COL_GLA_K, COL_GLA_V, COL_GLA_G = 0, 768, 1536, 3072
COL_RW_G1 = 4608
COL_RET_Q, COL_RET_K, COL_RET_V, COL_RET_G = 5120, 5760, 6400, 7680
COL_RW_R, COL_RW_K, COL_RW_V = 8960, 10240, 11520
COL_RW_W1, COL_RW_A1, COL_GLA_LR = 12800, 12928, 13056
N_IN_PAD = 13312

VMEM_LIMIT = 56 * 1024 * 1024


def _cparams(sem):
    return pltpu.CompilerParams(dimension_semantics=sem, vmem_limit_bytes=VMEM_LIMIT)


def _mm_body(a_ref, b_ref, o_ref):
    o_ref[...] = jnp.dot(a_ref[...].astype(BF16), b_ref[...].astype(BF16),
                         preferred_element_type=F32).astype(o_ref.dtype)


def matmul(a, b, *, tm, tn, out_dtype=F32, name="mm", layer=None):
    m, k = a.shape
    n = b.shape[-1]
    assert m % tm == 0 and n % tn == 0, (a.shape, b.shape, tm, tn)
    if layer is None:
        b_spec = pl.BlockSpec((k, tn), lambda i, j: (0, j))
    else:
        b_spec = pl.BlockSpec((None, k, tn), lambda i, j: (layer, 0, j))
    return pl.pallas_call(
        _mm_body,
        out_shape=jax.ShapeDtypeStruct((m, n), out_dtype),
        grid=(m // tm, n // tn),
        in_specs=[pl.BlockSpec((tm, k), lambda i, j: (i, 0)), b_spec],
        out_specs=pl.BlockSpec((tm, tn), lambda i, j: (i, j)),
        compiler_params=_cparams(("parallel", "arbitrary")),
        name=name,
    )(a, b)


def _scan_consts():
    c = CHUNK
    cum = np.zeros((2, 2 * c + 8, c), np.float32)
    masks = np.zeros((2, len(LEVELS), c, c), np.float32)
    tri = np.tril(np.ones((c, c), np.float32))
    cum[0, :c] = tri
    cum[0, c:2 * c] = tri
    cum[1, :c] = tri.T
    cum[1, c:2 * c] = np.triu(np.ones((c, c), np.float32), 1)
    cum[:, 2 * c:] = 1.0
    for li, s in enumerate(LEVELS):
        for i in range(c):
            for j in range(c):
                if (i & s) and not (j & s) and i // (2 * s) == j // (2 * s):
                    masks[0, li, i, j] = 1.0
        masks[1, li] = masks[0, li].T
    return jnp.asarray(cum, BF16), jnp.asarray(masks)


def _split3(x):
    hi = x.astype(BF16)
    r1 = x - hi.astype(F32)
    mid = r1.astype(BF16)
    lo = (r1 - mid.astype(F32)).astype(BF16)
    return hi, mid, lo


def _cumsums(cum, la):
    d = la.shape[1]
    r = jnp.dot(cum, jnp.concatenate(_split3(la), axis=1), preferred_element_type=F32)
    return r[:, :d] + r[:, d:2 * d] + r[:, 2 * d:]


def _boundary(cc, s):
    c, d = cc.shape
    if s >= 8:
        parts = [jnp.broadcast_to(cc[r0 + s - 1:r0 + s], (2 * s, d)) for r0 in range(0, c, 2 * s)]
        return parts[0] if len(parts) == 1 else jnp.concatenate(parts, axis=0)
    cc3 = cc.reshape(c // 8, 8, d)
    if s == 4:
        return jnp.broadcast_to(cc3[:, 3:4], cc3.shape).reshape(c, d)
    sub = lax.broadcasted_iota(jnp.int32, cc3.shape, 1)
    lo = jnp.broadcast_to(cc3[:, 1:2], cc3.shape)
    hi = jnp.broadcast_to(cc3[:, 5:6], cc3.shape)
    return jnp.where(sub < 4, lo, hi).reshape(c, d)


def _nt(a, b):
    return lax.dot_general(a, b, (((1,), (1,)), ((), ())), preferred_element_type=F32)


def _tn(a, b):
    return lax.dot_general(a, b, (((0,), (0,)), ((), ())), preferred_element_type=F32)


def _gla_body(cum_ref, mask_ref, q_ref, k_ref, v_ref, la_ref, o_ref, st_ref, *, heads, dk, dv, scale):
    @pl.when(pl.program_id(2) == 0)
    def _():
        st_ref[...] = jnp.zeros_like(st_ref)

    cum = cum_ref[0]
    c = CHUNK
    hs = range(heads)
    qs_, ks_, vs_, las, bqs, ccs, tots = [], [], [], [], [], [], []
    for h in hs:
        la = la_ref[0, 0, :, h * dk:(h + 1) * dk]
        rs = _cumsums(cum, la)
        qs_.append(q_ref[0, :, h * dk:(h + 1) * dk] * scale)
        ks_.append(k_ref[0, :, h * dk:(h + 1) * dk])
        vs_.append(v_ref[0, :, h * dv:(h + 1) * dv])
        las.append(la)
        bqs.append(rs[:c])
        ccs.append(rs[c:2 * c])
        tots.append(rs[2 * c:2 * c + 1])
    sc = [None] * heads
    for li, s in enumerate(LEVELS):
        for h in hs:
            q, k = qs_[h], ks_[h]
            if s == 1:
                qd, kd = q * jnp.exp(las[h]), k
            else:
                g = bqs[h] - _boundary(ccs[h], s)
                qd = q * jnp.exp(jnp.minimum(g, 0.0))
                kd = k * jnp.exp(jnp.minimum(-g, 0.0))
            p = _nt(qd.astype(BF16), kd.astype(BF16)) * mask_ref[0, li]
            sc[h] = p if sc[h] is None else sc[h] + p
    for h in hs:
        q, k, v, bq, tot = qs_[h], ks_[h], vs_[h], bqs[h], tots[h]
        vb = v.astype(BF16)
        dg = jnp.sum(q * k, axis=1, keepdims=True)
        st = st_ref[h]
        o = (jnp.dot(sc[h].astype(BF16), vb, preferred_element_type=F32) + dg * v
             + _nt((q * jnp.exp(bq)).astype(BF16), st.astype(BF16)))
        o_ref[0, 0, :, h * dv:(h + 1) * dv] = o
        kbar = k * jnp.exp(tot - bq)
        st_ref[h] = st * jnp.exp(tot) + _tn(vb, kbar.astype(BF16))


def _ret_body(lam_ref, q_ref, k_ref, v_ref, o_ref, st_ref, *, heads, dk, dv):
    d = pl.program_id(0)

    @pl.when(pl.program_id(2) == 0)
    def _():
        st_ref[...] = jnp.zeros_like(st_ref)

    c = CHUNK
    ri = lax.broadcasted_iota(jnp.int32, (c, c), 0)
    ci = lax.broadcasted_iota(jnp.int32, (c, c), 1)
    dist = jnp.where(d == 0, ri - ci, ci - ri)
    live = dist >= 0
    distf = jnp.maximum(dist, 0).astype(F32)
    row = lax.broadcasted_iota(jnp.int32, (c, 1), 0)
    pos = jnp.where(d == 0, row + 1, c - row).astype(F32)
    for h in range(heads):
        lam = lam_ref[d * heads + h]
        q = q_ref[0, :, h * dk:(h + 1) * dk]
        k = k_ref[0, :, h * dk:(h + 1) * dk]
        v = v_ref[0, :, h * dv:(h + 1) * dv]
        vb = v.astype(BF16)
        decay = jnp.where(live, jnp.exp(lam * distf), 0.0)
        sc = _nt(q.astype(BF16), k.astype(BF16)) * decay
        st = st_ref[h]
        o = (jnp.dot(sc.astype(BF16), vb, preferred_element_type=F32)
             + _nt((q * jnp.exp(lam * pos)).astype(BF16), st.astype(BF16)))
        o_ref[0, 0, :, h * dv:(h + 1) * dv] = o
        kbar = k * jnp.exp(lam * (c - pos))
        st_ref[h] = st * jnp.exp(lam * c) + _tn(vb, kbar.astype(BF16))


def _chunk_index(d, c, n_ctx, n_all):
    back = jnp.where(c < n_ctx, n_ctx - 1 - c, n_ctx + n_all - 1 - c)
    return jnp.where(d == 0, c, back)


def _col_spec(width, col, ch):
    assert col % width == 0
    return pl.BlockSpec((1, CHUNK, width), lambda d, b, c, *_: (b, ch(d, c), col // width))


def gla_scan(z, la2, n_ctx):
    bsz, t, _ = z.shape
    heads, dk, dv = GLA_HEADS, GLA_DK, GLA_DV
    n_all = t // CHUNK
    cum, masks = _scan_consts()
    ch = functools.partial(_chunk_index, n_ctx=n_ctx, n_all=n_all)
    body = functools.partial(_gla_body, heads=heads, dk=dk, dv=dv, scale=dk ** -0.5)
    return pl.pallas_call(
        body,
        out_shape=jax.ShapeDtypeStruct((2, bsz, t, heads * dv), F32),
        grid=(2, bsz, n_all),
        in_specs=[
            pl.BlockSpec((1,) + cum.shape[1:], lambda d, b, c: (d, 0, 0)),
            pl.BlockSpec((1,) + masks.shape[1:], lambda d, b, c: (d, 0, 0, 0)),
            _col_spec(heads * dk, COL_GLA_Q, ch),
            _col_spec(heads * dk, COL_GLA_K, ch),
            _col_spec(heads * dv, COL_GLA_V, ch),
            pl.BlockSpec((1, 1, CHUNK, heads * dk), lambda d, b, c: (d, b, ch(d, c), 0)),
        ],
        out_specs=pl.BlockSpec((1, 1, CHUNK, heads * dv), lambda d, b, c: (d, b, ch(d, c), 0)),
        scratch_shapes=[pltpu.VMEM((heads, dv, dk), F32)],
        compiler_params=_cparams(("parallel", "parallel", "arbitrary")),
        name="gla_scan",
    )(cum, masks, z, z, z, la2)


def ret_scan(q, k, z, lam, n_ctx):
    bsz, t, _ = q.shape
    heads, dk, dv = RET_HEADS, RET_DK, RET_DV
    n_all = t // CHUNK
    ch = functools.partial(_chunk_index, n_ctx=n_ctx, n_all=n_all)
    body = functools.partial(_ret_body, heads=heads, dk=dk, dv=dv)
    grid_spec = pltpu.PrefetchScalarGridSpec(
        num_scalar_prefetch=1,
        grid=(2, bsz, n_all),
        in_specs=[
            pl.BlockSpec((1, CHUNK, heads * dk), lambda d, b, c, lam: (b, ch(d, c), 0)),
            pl.BlockSpec((1, CHUNK, heads * dk), lambda d, b, c, lam: (b, ch(d, c), 0)),
            _col_spec(heads * dv, COL_RET_V, ch),
        ],
        out_specs=pl.BlockSpec((1, 1, CHUNK, heads * dv), lambda d, b, c, lam: (d, b, ch(d, c), 0)),
        scratch_shapes=[pltpu.VMEM((heads, dv, dk), F32)],
    )
    return pl.pallas_call(
        body,
        out_shape=jax.ShapeDtypeStruct((2, bsz, t, heads * dv), F32),
        grid_spec=grid_spec,
        compiler_params=_cparams(("parallel", "parallel", "arbitrary")),
        name="ret_scan",
    )(lam.reshape(-1).astype(F32), q, k, z)


RWKV_GROUP = 4
RWKV_GW = RWKV_GROUP * RWKV_HEAD


def _rwkv_consts():
    c = CHUNK
    gw = RWKV_GW
    cum = np.zeros((2, 2 * c + 16, c), np.float32)
    tri = np.tril(np.ones((c, c), np.float32))
    mid = c // 2
    cum[0, :c] = tri - tri[mid - 1][None, :]
    cum[0, c:2 * c] = 1.0 - tri
    cum[0, 2 * c + 8:] = tri[mid - 1][None, :]
    cum[1, :c] = tri.T - tri.T[mid][None, :]
    cum[1, c:2 * c] = 1.0 - tri.T
    cum[1, 2 * c + 8:] = tri.T[mid][None, :]
    cum[:, 2 * c:2 * c + 8] = 1.0
    t = np.arange(c)[:, None]
    s = np.tile(np.arange(c), RWKV_GROUP)[None, :]
    masks = np.zeros((2, 3, c, gw), np.float32)
    masks[0, 0], masks[0, 1] = s < t, s <= t
    masks[1, 0], masks[1, 1] = s > t, s >= t
    masks[:, 2] = s == t
    return jnp.asarray(cum, BF16), jnp.asarray(masks)


def _rwkv_body(cum_ref, mask_ref, r_ref, k_ref, v_ref, kk_ref, kka_ref, lw_ref, o_ref, st_ref, *, groups):
    @pl.when(pl.program_id(2) == 0)
    def _():
        st_ref[...] = jnp.zeros_like(st_ref)

    c, gw = CHUNK, RWKV_GW
    cum = cum_ref[0]
    strict, incl, eye = mask_ref[0, 0], mask_ref[0, 1], mask_ref[0, 2]
    rb = lax.broadcasted_iota(jnp.int32, (gw, gw), 0) // RWKV_HEAD
    cb = lax.broadcasted_iota(jnp.int32, (gw, gw), 1) // RWKV_HEAD
    same_head = rb == cb

    def bd(x):
        xb = x.astype(BF16)
        return jnp.where(same_head, jnp.concatenate([xb] * RWKV_GROUP, axis=0), jnp.zeros((), BF16))

    def mm(a, b):
        return jnp.dot(a.astype(BF16), b, preferred_element_type=F32)

    gs = range(groups)
    sls = [slice(g * gw, (g + 1) * gw) for g in gs]
    pre = []
    for sl in sls:
        r, k, v = r_ref[0, :, sl], k_ref[0, :, sl], v_ref[0, :, sl]
        kk, kka, lw = kk_ref[0, :, sl], kka_ref[0, :, sl], lw_ref[0, 0, :, sl]
        rs = _cumsums(cum, lw)
        gm, ge, tot, gmid = rs[:c], rs[c:2 * c], rs[2 * c:2 * c + 1], rs[2 * c + 8:2 * c + 9]
        e_in, e_out, e_end = jnp.exp(gm), jnp.exp(-gm), jnp.exp(ge)
        kt_rel = kk * (e_in * jnp.exp(-lw))
        lhs = jnp.concatenate([kt_rel, r * e_in], axis=0)
        pre.append(dict(lhs=lhs.astype(BF16), lhs_abs=(lhs * jnp.exp(gmid)).astype(BF16),
                        kka_o=kka * e_out, k_o=k * e_out, v=v, tot=tot,
                        kv_end=jnp.concatenate([k * e_end, kka * e_end], axis=0).astype(BF16)))
    s_b = [_nt(q["lhs"], bd(q["kka_o"])) for q in pre]
    s_k = [_nt(q["lhs"], bd(q["k_o"])) for q in pre]
    ns = [-(s[:c] * strict) for s in s_b]
    xs = [eye + n for n in ns]
    ps = [mm(n, bd(n)) for n in ns]
    for _ in range(4):
        pps = [mm(jnp.concatenate([p, x], axis=0), bd(p)) for p, x in zip(ps, xs)]
        ps = [pp[:c] for pp in pps]
        xs = [x + pp[c:] for x, pp in zip(xs, pps)]
    xs = [x + mm(x, bd(p)) for p, x in zip(ps, xs)]
    sts = [st_ref[g] for g in gs]
    hs = [_nt(q["lhs_abs"], st.astype(BF16)) for q, st in zip(pre, sts)]
    vs = [mm(jnp.concatenate([sk[:c] * strict, sk[c:] * incl], axis=0), bd(q["v"])) for sk, q in zip(s_k, pre)]
    us = [mm(x, bd(h[:c] + w[:c])) for x, h, w in zip(xs, hs, vs)]
    for g in gs:
        o_ref[0, 0, :, sls[g]] = hs[g][c:] + vs[g][c:] - mm(s_b[g][c:] * incl, bd(us[g]))
    for g in gs:
        upd = _tn(jnp.concatenate([pre[g]["v"], -us[g]], axis=0).astype(BF16), pre[g]["kv_end"])
        st_ref[g] = sts[g] * jnp.exp(pre[g]["tot"]) + jnp.where(same_head, upd, 0.0)


def rwkv_scan(r, k, v, kk, kka, lw2, n_ctx):
    bsz, t, ch_w = r.shape
    groups = ch_w // RWKV_GW
    n_all = t // CHUNK
    cum, masks = _rwkv_consts()
    ch = functools.partial(_chunk_index, n_ctx=n_ctx, n_all=n_all)
    tok = pl.BlockSpec((1, CHUNK, ch_w), lambda d, b, c: (b, ch(d, c), 0))
    dir_tok = pl.BlockSpec((1, 1, CHUNK, ch_w), lambda d, b, c: (d, b, ch(d, c), 0))
    return pl.pallas_call(
        functools.partial(_rwkv_body, groups=groups),
        out_shape=jax.ShapeDtypeStruct((2, bsz, t, ch_w), F32),
        grid=(2, bsz, n_all),
        in_specs=[
            pl.BlockSpec((1,) + cum.shape[1:], lambda d, b, c: (d, 0, 0)),
            pl.BlockSpec((1,) + masks.shape[1:], lambda d, b, c: (d, 0, 0, 0)),
            tok, tok, tok, tok, tok, dir_tok,
        ],
        out_specs=dir_tok,
        scratch_shapes=[pltpu.VMEM((groups, RWKV_GW, RWKV_GW), F32)],
        compiler_params=_cparams(("parallel", "parallel", "arbitrary")),
        name="rwkv_scan",
    )(cum, masks, r, k, v, kk, kka, lw2)


def _split2(x):
    hi = x.astype(BF16)
    return hi, (x - hi.astype(F32)).astype(BF16)


ROUTER_LANES = 128


def _router_body(h_ref, w_ref, bias_ref, idx_ref, gate_ref, *, n_experts, top_k):
    a_hi, a_mid = _split2(h_ref[...])
    b_hi, b_mid = _split2(w_ref[...])
    dot = functools.partial(jnp.dot, preferred_element_type=F32)
    scores = jax.nn.sigmoid(dot(a_hi, b_hi) + (dot(a_hi, b_mid) + dot(a_mid, b_hi)))
    lane = lax.broadcasted_iota(jnp.int32, scores.shape, 1)
    cand = jnp.where(lane < n_experts, scores + bias_ref[...], -jnp.inf)
    idx_out = jnp.zeros(scores.shape, jnp.int32)
    val_out = jnp.zeros(scores.shape, F32)
    total = jnp.zeros((scores.shape[0], 1), F32)
    for j in range(top_k):
        best = jnp.max(cand, axis=1, keepdims=True)
        arg = jnp.min(jnp.where(cand == best, lane, ROUTER_LANES), axis=1, keepdims=True)
        pick = lane == arg
        val = jnp.sum(jnp.where(pick, scores, 0.0), axis=1, keepdims=True)
        cand = jnp.where(pick, -jnp.inf, cand)
        idx_out = jnp.where(lane == j, arg, idx_out)
        val_out = jnp.where(lane == j, val, val_out)
        total = total + val
    idx_ref[...] = idx_out
    gate_ref[...] = ROUTED_SCALE * val_out / total


def router(h, router_w, router_bias, layer):
    t, d = h.shape
    e = router_w.shape[-1]
    tm = _tile(t, 600)
    w = jnp.pad(router_w[layer].astype(F32), ((0, 0), (0, ROUTER_LANES - e)))
    bias = jnp.pad(router_bias[layer].astype(F32), (0, ROUTER_LANES - e))[None]
    idx, gates = pl.pallas_call(
        functools.partial(_router_body, n_experts=e, top_k=TOP_K),
        out_shape=(jax.ShapeDtypeStruct((t, ROUTER_LANES), jnp.int32), jax.ShapeDtypeStruct((t, ROUTER_LANES), F32)),
        grid=(t // tm,),
        in_specs=[pl.BlockSpec((tm, d), lambda i: (i, 0)), pl.BlockSpec((d, ROUTER_LANES), lambda i: (0, 0)),
                  pl.BlockSpec((1, ROUTER_LANES), lambda i: (0, 0))],
        out_specs=(pl.BlockSpec((tm, ROUTER_LANES), lambda i: (i, 0)), pl.BlockSpec((tm, ROUTER_LANES), lambda i: (i, 0))),
        compiler_params=_cparams(("parallel",)),
        name="router",
    )(h, w, bias)
    return idx[:, :TOP_K], gates[:, :TOP_K]


MOE_BM = 256


def _tile(m, cap, mult=16):
    return max(t for t in range(mult, cap + 1, mult) if m % t == 0)


def _pack_pairs(x):
    n = x.shape[-1] // 2
    bits = lax.bitcast_convert_type(x, jnp.uint32)
    bits = (bits + jnp.uint32(0x7FFF) + ((bits >> 16) & jnp.uint32(1))) & jnp.uint32(0xFFFF0000)
    return bits[..., :n] | (bits[..., n:] >> 16)


def _unpack_pairs(w):
    hi = lax.bitcast_convert_type(w & jnp.uint32(0xFFFF0000), F32)
    lo = lax.bitcast_convert_type(w << 16, F32)
    return jnp.concatenate([hi, lo], axis=-1)


def _swiglu(x, wg, wu, wd, gate=None):
    h1 = jnp.dot(x, wg, preferred_element_type=F32)
    h2 = jnp.dot(x, wu, preferred_element_type=F32)
    a = h1 * jax.nn.sigmoid(h1) * h2
    if gate is not None:
        a = a * gate
    return jnp.dot(a.astype(BF16), wd, preferred_element_type=F32)


def _cast_rows(src, dst, rows):
    def body(j, carry):
        r = pl.multiple_of(j * rows, rows)
        dst[pl.ds(r, rows), :] = src[pl.ds(r, rows), :].astype(BF16)
        return carry
    lax.fori_loop(0, src.shape[0] // rows, body, 0)


def _experts_body(be_ref, bv_ref, nx_ref, tok_ref, h_hbm, gate_ref, wg_hbm, wu_hbm, wd_hbm, o_ref,
                  xbuf, st_g, st_u, st_d, wg_b, wu_b, wd_b, xsem, wsem, *, layer, bm):
    i = pl.program_id(0)
    n = pl.num_programs(0)
    slot = lax.rem(i, 2)

    def gather(blk, s):
        base = blk * bm

        def body(r, carry):
            tok = tok_ref[base + r]
            pltpu.make_async_copy(h_hbm.at[pl.ds(tok, 1)], xbuf.at[s, pl.ds(r, 1)], xsem.at[s]).start()
            return carry
        lax.fori_loop(0, bm, body, 0, unroll=8)

    def weight_copies(e):
        return (pltpu.make_async_copy(wg_hbm.at[layer, e], st_g, wsem.at[0]),
                pltpu.make_async_copy(wu_hbm.at[layer, e], st_u, wsem.at[1]),
                pltpu.make_async_copy(wd_hbm.at[layer, e], st_d, wsem.at[2]))

    @pl.when(i == 0)
    def _():
        gather(0, 0)
        for cp in weight_copies(be_ref[0]):
            cp.start()

    @pl.when(i + 1 < n)
    def _():
        gather(i + 1, 1 - slot)

    @pl.when((i == 0) | (be_ref[i] != be_ref[jnp.maximum(i - 1, 0)]))
    def _():
        for cp in weight_copies(be_ref[i]):
            cp.wait()
        _cast_rows(st_g, wg_b, 256)
        _cast_rows(st_u, wu_b, 256)
        _cast_rows(st_d, wd_b, 32)

        @pl.when(nx_ref[i] >= 0)
        def _():
            for cp in weight_copies(nx_ref[i]):
                cp.start()

    pltpu.make_async_copy(h_hbm.at[pl.ds(0, bm)], xbuf.at[slot], xsem.at[slot]).wait()

    @pl.when(bv_ref[i] != 0)
    def _():
        x = _unpack_pairs(xbuf[slot]).astype(BF16)
        o_ref[...] = _pack_pairs(_swiglu(x, wg_b[...], wu_b[...], wd_b[...], gate_ref[...]))

    @pl.when(bv_ref[i] == 0)
    def _():
        o_ref[...] = jnp.zeros_like(o_ref)


def experts(h_pad, gate_rows, exp_gate, exp_up, exp_down, layer, block_expert, block_valid, next_expert, buf_tok):
    bm = MOE_BM
    n = buf_tok.shape[0]
    d = exp_gate.shape[-2]
    f = exp_gate.shape[-1]
    dp = h_pad.shape[1]
    any_spec = pl.BlockSpec(memory_space=pl.ANY)
    grid_spec = pltpu.PrefetchScalarGridSpec(
        num_scalar_prefetch=4,
        grid=(n // bm,),
        in_specs=[any_spec, pl.BlockSpec((bm, 1), lambda i, *_: (i, 0)), any_spec, any_spec, any_spec],
        out_specs=pl.BlockSpec((bm, dp), lambda i, *_: (i, 0)),
        scratch_shapes=[
            pltpu.VMEM((2, bm, dp), jnp.uint32),
            pltpu.VMEM((d, f), F32), pltpu.VMEM((d, f), F32), pltpu.VMEM((f, d), F32),
            pltpu.VMEM((d, f), BF16), pltpu.VMEM((d, f), BF16), pltpu.VMEM((f, d), BF16),
            pltpu.SemaphoreType.DMA((2,)), pltpu.SemaphoreType.DMA((3,)),
        ],
    )
    return pl.pallas_call(
        functools.partial(_experts_body, layer=layer, bm=bm),
        out_shape=jax.ShapeDtypeStruct((n, dp), jnp.uint32),
        grid_spec=grid_spec,
        compiler_params=_cparams(("arbitrary",)),
        name="experts",
    )(block_expert, block_valid, next_expert, buf_tok, h_pad, gate_rows, exp_gate, exp_up, exp_down)


def _shared_body(x_ref, wg_ref, wu_ref, wd_ref, o_ref):
    o_ref[...] = _swiglu(x_ref[...].astype(BF16), wg_ref[...].astype(BF16), wu_ref[...].astype(BF16),
                         wd_ref[...].astype(BF16))


def shared_expert(h, sh_gate, sh_up, sh_down, layer):
    t, d = h.shape
    f = sh_gate.shape[-1]
    tm = _tile(t, 300)
    once = pl.Buffered(1)
    return pl.pallas_call(
        _shared_body,
        out_shape=jax.ShapeDtypeStruct((t, d), F32),
        grid=(t // tm,),
        in_specs=[pl.BlockSpec((tm, d), lambda i: (i, 0)),
                  pl.BlockSpec((None, d, f), lambda i: (layer, 0, 0), pipeline_mode=once),
                  pl.BlockSpec((None, d, f), lambda i: (layer, 0, 0), pipeline_mode=once),
                  pl.BlockSpec((None, f, d), lambda i: (layer, 0, 0), pipeline_mode=once)],
        out_specs=pl.BlockSpec((tm, d), lambda i: (i, 0)),
        compiler_params=_cparams(("arbitrary",)),
        name="shared",
    )(h, sh_gate, sh_up, sh_down)


COMBINE_TB = 64


def _combine_body(pos_ref, y_hbm, sh_ref, o_ref, buf, sem, *, tb, k):
    i = pl.program_id(0)
    n = pl.num_programs(0)
    slot = lax.rem(i, 2)

    def gather(blk, s):
        base = blk * tb * k

        def body(t, carry):
            for kk in range(k):
                p = pos_ref[base + t * k + kk]
                pltpu.make_async_copy(y_hbm.at[pl.ds(p, 1)], buf.at[s, pl.ds(kk * tb + t, 1)], sem.at[s]).start()
            return carry
        lax.fori_loop(0, tb, body, 0)

    @pl.when(i == 0)
    def _():
        gather(0, 0)

    @pl.when(i + 1 < n)
    def _():
        gather(i + 1, 1 - slot)

    pltpu.make_async_copy(y_hbm.at[pl.ds(0, tb * k)], buf.at[slot], sem.at[slot]).wait()
    acc = sh_ref[...]
    for kk in range(k):
        acc = acc + _unpack_pairs(buf[slot, kk * tb:(kk + 1) * tb, :])
    o_ref[...] = acc


def combine(y, pos, shared):
    t, k = pos.shape
    d = shared.shape[1]
    dp = y.shape[1]
    tb = COMBINE_TB
    grid_spec = pltpu.PrefetchScalarGridSpec(
        num_scalar_prefetch=1,
        grid=(t // tb,),
        in_specs=[pl.BlockSpec(memory_space=pl.ANY), pl.BlockSpec((tb, d), lambda i, *_: (i, 0))],
        out_specs=pl.BlockSpec((tb, d), lambda i, *_: (i, 0)),
        scratch_shapes=[pltpu.VMEM((2, tb * k, dp), jnp.uint32), pltpu.SemaphoreType.DMA((2,))],
    )
    return pl.pallas_call(
        functools.partial(_combine_body, tb=tb, k=k),
        out_shape=jax.ShapeDtypeStruct((t, d), F32),
        grid_spec=grid_spec,
        compiler_params=_cparams(("arbitrary",)),
        name="combine",
    )(pos.reshape(-1), y, shared)


def moe_ffn(h, layer, router_w, router_bias, exp_gate, exp_up, exp_down, sh_gate, sh_up, sh_down):
    t, d = h.shape
    e, k, bm = N_EXPERTS, TOP_K, MOE_BM
    i32 = jnp.int32
    idx, gates = router(h, router_w, router_bias, layer)
    flat_e = idx.reshape(-1)
    iota = jnp.arange(t * k, dtype=i32)
    e_sorted, order = lax.sort_key_val(flat_e, iota)
    _, rank = lax.sort_key_val(order, iota)
    bounds = jnp.searchsorted(e_sorted, jnp.arange(e + 1, dtype=i32), side='left').astype(i32)
    first, counts = bounds[:-1], bounds[1:] - bounds[:-1]
    padded = (counts + bm - 1) // bm * bm
    ends = jnp.cumsum(padded)
    starts = ends - padded
    pos = (iota + (starts - first)[e_sorted])[rank].reshape(t, k)
    n_blocks = -(-(t * k) // bm) + e
    blk_start = jnp.arange(n_blocks, dtype=i32) * bm
    block_valid = (blk_start < ends[-1]).astype(i32)
    last_e = jnp.max(jnp.where(counts > 0, jnp.arange(e, dtype=i32), 0))
    block_expert = jnp.where(block_valid > 0,
                             jnp.minimum(jnp.searchsorted(ends, blk_start, side='right').astype(i32), e - 1), last_e)
    present = jnp.where(counts > 0, jnp.arange(e, dtype=i32), e)
    nxt = lax.cummin(jnp.concatenate([present[1:], jnp.full((1,), e, i32)]), reverse=True)
    next_expert = jnp.where(nxt < e, nxt, -1)[block_expert]
    blk_off = blk_start - starts[block_expert]
    blk_live = jnp.where(block_valid > 0, jnp.clip(counts[block_expert] - blk_off, 0, bm), 0)
    lane = jnp.arange(bm, dtype=i32)[None, :]
    live = (lane < blk_live[:, None]).reshape(-1)
    src = order[jnp.clip((first[block_expert] + blk_off)[:, None] + lane, 0, t * k - 1).reshape(-1)]
    buf_tok = jnp.where(live, src // k, t).astype(i32)
    gate_rows = jnp.where(live, gates.reshape(-1)[src], 0.0)[:, None]

    h_pad = jnp.concatenate([_pack_pairs(h), jnp.zeros((1, d // 2), jnp.uint32)], 0)
    y = experts(h_pad, gate_rows, exp_gate, exp_up, exp_down, layer, block_expert, block_valid, next_expert, buf_tok)
    return combine(y, pos, shared_expert(h, sh_gate, sh_up, sh_down, layer))


def _permute_cols(w):
    gla, ret, rw = w[..., :GLA_N], w[..., GLA_N:GLA_N + RET_N], w[..., GLA_N + RET_N:]
    rkv = 3 * RWKV_WIDTH
    lora = rkv + RWKV_DECAY_RANK + RWKV_ICLR_RANK

    def zeros(n):
        return jnp.zeros(w.shape[:-1] + (n,), w.dtype)

    parts = [gla[..., :GLA_N - GLA_RANK],
             rw[..., lora:], zeros(COL_RET_Q - COL_RW_G1 - RWKV_GATE_RANK),
             ret,
             rw[..., :lora],
             gla[..., GLA_N - GLA_RANK:], zeros(N_IN_PAD - COL_GLA_LR - GLA_RANK)]
    return jnp.concatenate(parts, -1)


def _rmsnorm(x, g):
    return x * lax.rsqrt(jnp.mean(x * x, -1, keepdims=True) + RMS_EPS) * g


def _shift_seq(u):
    h = u.shape[-1] // 2
    prev = jnp.pad(u[:, :-1, :h], ((0, 0), (1, 0), (0, 0)))
    nxt = jnp.pad(u[:, 1:, h:], ((0, 0), (0, 1), (0, 0)))
    return jnp.concatenate([prev, nxt], -1)


def _shift_grid(u):
    bsz, t, ch = u.shape
    g = u.reshape(bsz, t // GRID_W, GRID_W, ch)
    q = ch // 4
    left = jnp.pad(g[:, :, :-1, :q], ((0, 0), (0, 0), (1, 0), (0, 0)))
    right = jnp.pad(g[:, :, 1:, q:2 * q], ((0, 0), (0, 0), (0, 1), (0, 0)))
    up = jnp.pad(g[:, :-1, :, 2 * q:3 * q], ((0, 0), (1, 0), (0, 0), (0, 0)))
    down = jnp.pad(g[:, 1:, :, 3 * q:], ((0, 0), (0, 1), (0, 0), (0, 0)))
    return jnp.concatenate([left, right, up, down], -1).reshape(bsz, t, ch)


def _rope(x, pos):
    d = x.shape[-1]
    inv = ROPE_BASE ** (-jnp.linspace(0.0, 1.0, d // 2, dtype=F32))
    ang = pos.astype(F32)[:, None] * inv[None, :]
    cos, sin = jnp.cos(ang)[:, None, :], jnp.sin(ang)[:, None, :]
    x1, x2 = x[..., :d // 2], x[..., d // 2:]
    return jnp.concatenate([x1 * cos - x2 * sin, x1 * sin + x2 * cos], -1)


def _head_rms(o, heads):
    bsz, t, ch = o.shape
    o4 = o.reshape(bsz, t, heads, ch // heads)
    return o4 * lax.rsqrt(jnp.mean(o4 * o4, -1, keepdims=True) + RMS_EPS)


def _small_mm(a, b, name):
    bsz, t, k = a.shape
    m = bsz * t
    return matmul(a.reshape(m, k), b, tm=_tile(m, 1100), tn=b.shape[1], name=name).reshape(bsz, t, -1)


def gla_mixer(z, a2, a_bias, norm_w, n_ctx, out_from):
    lr_pad = z[..., COL_GLA_LR:COL_GLA_LR + 128]
    la2 = jnp.stack([_small_mm(lr_pad, jnp.pad(a2[x], ((0, 128 - GLA_RANK), (0, 0))), "gla_gate")
                     for x in range(2)])
    la2 = jax.nn.log_sigmoid(la2 + a_bias[:, None, None, :]) / GLA_TAU
    o2 = gla_scan(z, la2, n_ctx)
    o = (o2[0] + o2[1])[:, out_from:]
    bsz, t, _ = o.shape
    o = (_head_rms(o, GLA_HEADS) * norm_w).reshape(bsz, t, GLA_W)
    return o * jax.nn.silu(z[:, out_from:, COL_GLA_G:COL_GLA_G + GLA_W])


def ret_mixer(z, decay_logit, n_ctx, out_from):
    bsz, t, _ = z.shape
    q, k = z[..., COL_RET_Q:COL_RET_Q + RET_QK], z[..., COL_RET_K:COL_RET_K + RET_QK]
    pos = jnp.arange(t)
    q = _rope(q.reshape(bsz, t, RET_HEADS, RET_DK), pos).reshape(bsz, t, RET_QK)
    k = (_rope(k.reshape(bsz, t, RET_HEADS, RET_DK), pos) * RET_DK ** -0.5).reshape(bsz, t, RET_QK)
    lam = jax.nn.log_sigmoid(decay_logit.astype(F32))
    o2 = ret_scan(q, k, z, lam, n_ctx)
    o = (o2[0] + o2[1])[:, out_from:]
    o = _head_rms(o, RET_HEADS).reshape(bsz, t - out_from, RET_W)
    return o * jax.nn.silu(z[:, out_from:, COL_RET_G:COL_RET_G + RET_W])


def rwkv_mixer(z_all, lc, mu, w0, w2, a0, a2, g2, k_k, k_a, r_k, lnx_w, lnx_b, n_ctx, out_from):
    bsz, t, _ = z_all.shape
    cw = RWKV_WIDTH
    zw = jnp.concatenate([z_all[..., COL_RW_R:COL_RW_A1 + RWKV_ICLR_RANK],
                          z_all[..., COL_RW_G1:COL_RW_G1 + RWKV_GATE_RANK]], -1)
    shifted = jnp.concatenate([_shift_seq(zw[:, :lc]), _shift_grid(zw[:, lc:])], 1)
    z = zw + mu * (shifted - zw)
    r, k, v = z[..., :cw], z[..., cw:2 * cw], z[..., 2 * cw:3 * cw]
    o3 = 3 * cw
    xw = z[..., o3:o3 + RWKV_DECAY_RANK]
    xa = z[..., o3 + RWKV_DECAY_RANK:o3 + RWKV_DECAY_RANK + RWKV_ICLR_RANK]
    xg = z[..., o3 + RWKV_DECAY_RANK + RWKV_ICLR_RANK:]
    txw = jnp.tanh(xw)
    w_pre = w0[:, None, None, :] + jnp.stack([_small_mm(txw, w2[x], "rwkv_w") for x in range(2)])
    lw2 = -jnp.exp(-jax.nn.softplus(-w_pre) - 0.5)
    a = jax.nn.sigmoid(a0 + _small_mm(xa, a2, "rwkv_a"))
    xg_pad = jnp.pad(jax.nn.sigmoid(xg), ((0, 0), (0, 0), (0, 512 - RWKV_GATE_RANK)))
    g = _small_mm(xg_pad, jnp.pad(g2, ((0, 512 - RWKV_GATE_RANK), (0, 0))), "rwkv_g")

    def per_head(u):
        return u.reshape(bsz, t, RWKV_HEADS, RWKV_HEAD)

    kk = per_head(k * k_k)
    kk = (kk * lax.rsqrt(jnp.sum(kk * kk, -1, keepdims=True) + 1e-12)).reshape(bsz, t, cw)
    k = k * (1.0 + (a - 1.0) * k_a)
    bonus = (jnp.sum(per_head(r * k * r_k.reshape(-1)), -1, keepdims=True) * per_head(v)).reshape(bsz, t, cw)
    o2 = rwkv_scan(r, k, v, kk, kk * a, lw2, n_ctx)
    o = (o2[0] + o2[1])[:, out_from:]
    o4 = o.reshape(bsz, t - out_from, RWKV_HEADS, RWKV_HEAD)
    mean = jnp.mean(o4, -1, keepdims=True)
    var = jnp.mean(jnp.square(o4 - mean), -1, keepdims=True)
    gn = ((o4 - mean) * lax.rsqrt(var + GN_EPS)).reshape(bsz, t - out_from, cw) * lnx_w + lnx_b
    return (gn + bonus[:, out_from:]) * g[:, out_from:]


def kernel(x, c, ctx, c_ctx, ada_w, ada_b, pre_mix, post_mix, pre_ffn, post_ffn, w_in, w_out, gla_a2, gla_a_bias, gla_norm_w, ret_decay_logit, rwkv_mu, rwkv_w0, rwkv_w2, rwkv_a0, rwkv_a2, rwkv_g2, rwkv_k_k, rwkv_k_a, rwkv_r_k, rwkv_lnx_w, rwkv_lnx_b, router_w, router_bias, exp_gate, exp_up, exp_down, shared_gate, shared_up, shared_down):
    bsz, s, d = x.shape
    lc = ctx.shape[1]
    t = lc + s
    depth = ada_w.shape[0]
    n_ctx = lc // CHUNK
    for l in range(depth):
        ctx_out = l < depth - 1
        out_from = 0 if ctx_out else lc
        cvec = jnp.concatenate([jax.nn.silu(c), jax.nn.silu(c_ctx)[None], jnp.zeros((8 - bsz - 1, d), F32)], 0)
        mod = matmul(cvec, ada_w, tm=8, tn=1024, name="adaln", layer=l) + ada_b[l]
        mx = [m[:, None, :] for m in jnp.split(mod[:bsz], 6, axis=-1)]
        mc = jnp.split(mod[bsz], 6, axis=-1)

        hx = _rmsnorm(x, pre_mix[l]) * (1.0 + mx[1]) + mx[0]
        hc = _rmsnorm(ctx, pre_mix[l]) * (1.0 + mc[1]) + mc[0]
        h_all = jnp.concatenate([hc, hx], 1).reshape(bsz * t, d).astype(BF16)
        wi = _permute_cols(w_in[l]).astype(BF16)
        z = matmul(h_all, wi, tm=_tile(bsz * t, 1100), tn=1024, name="w_in").reshape(bsz, t, N_IN_PAD)
        y = jnp.concatenate([
            gla_mixer(z, gla_a2[l], gla_a_bias[l], gla_norm_w[l], n_ctx, out_from),
            ret_mixer(z, ret_decay_logit[l], n_ctx, out_from),
            rwkv_mixer(z, lc, rwkv_mu[l], rwkv_w0[l], rwkv_w2[l], rwkv_a0[l], rwkv_a2[l],
                       rwkv_g2[l], rwkv_k_k[l], rwkv_k_a[l], rwkv_r_k[l], rwkv_lnx_w[l], rwkv_lnx_b[l],
                       n_ctx, out_from),
        ], -1)
        t_out = t - out_from
        y = matmul(y.reshape(bsz * t_out, d).astype(BF16), w_out, tm=_tile(bsz * t_out, 1100), tn=512,
                   name="w_out", layer=l).reshape(bsz, t_out, d)
        if ctx_out:
            ctx = ctx + mc[2] * _rmsnorm(y[:, :lc], post_mix[l])
            x = x + mx[2] * _rmsnorm(y[:, lc:], post_mix[l])
        else:
            x = x + mx[2] * _rmsnorm(y, post_mix[l])

        hx = _rmsnorm(x, pre_ffn[l]) * (1.0 + mx[4]) + mx[3]
        if ctx_out:
            hc = _rmsnorm(ctx, pre_ffn[l]) * (1.0 + mc[4]) + mc[3]
            h = jnp.concatenate([hc, hx], 1)
        else:
            h = hx
        f = moe_ffn(h.reshape(-1, d), l, router_w, router_bias, exp_gate, exp_up, exp_down,
                    shared_gate, shared_up, shared_down).reshape(h.shape)
        if ctx_out:
            ctx = ctx + mc[5] * _rmsnorm(f[:, :lc], post_ffn[l])
            x = x + mx[5] * _rmsnorm(f[:, lc:], post_ffn[l])
        else:
            x = x + mx[5] * _rmsnorm(f, post_ffn[l])
    return x
```

```python
import functools

import numpy as np
import jax
import jax.numpy as jnp
from jax import lax
from jax.experimental import pallas as pl
from jax.experimental.pallas import tpu as pltpu

F32 = jnp.float32
BF16 = jnp.bfloat16

CHUNK = 64
LEVELS = (32, 16, 8, 4, 2, 1)

GLA_HEADS, GLA_DK, GLA_DV, GLA_RANK, GLA_TAU = 6, 128, 256, 16, 16.0
RET_HEADS, RET_DK, RET_DV = 5, 128, 256
ROPE_BASE = 10000.0
RWKV_HEADS, RWKV_HEAD = 20, 64
RWKV_WIDTH = RWKV_HEADS * RWKV_HEAD
RWKV_DECAY_RANK, RWKV_ICLR_RANK, RWKV_GATE_RANK = 128, 128, 480
GN_EPS = 64e-5
RMS_EPS = 1e-6
N_EXPERTS, TOP_K, D_EXPERT, ROUTED_SCALE = 64, 8, 384, 2.5
GRID_W = 64

GLA_QK, GLA_W = GLA_HEADS * GLA_DK, GLA_HEADS * GLA_DV
RET_QK, RET_W = RET_HEADS * RET_DK, RET_HEADS * RET_DV
GLA_N = 2 * GLA_QK + 2 * GLA_W + GLA_RANK
RET_N = 2 * RET_QK + 2 * RET_W
RWKV_N = 3 * RWKV_WIDTH + RWKV_DECAY_RANK + RWKV_ICLR_RANK + RWKV_GATE_RANK

COL_GLA_Q, COL_GLA_K, COL_GLA_V, COL_GLA_G = 0, 768, 1536, 3072
COL_RW_G1 = 4608
COL_RET_Q, COL_RET_K, COL_RET_V, COL_RET_G = 5120, 5760, 6400, 7680
COL_RW_R, COL_RW_K, COL_RW_V = 8960, 10240, 11520
COL_RW_W1, COL_RW_A1, COL_GLA_LR = 12800, 12928, 13056
N_IN_PAD = 13312

VMEM_LIMIT = 56 * 1024 * 1024


def _cparams(sem):
    return pltpu.CompilerParams(dimension_semantics=sem, vmem_limit_bytes=VMEM_LIMIT)


def _mm_body(a_ref, b_ref, o_ref):
    o_ref[...] = jnp.dot(a_ref[...].astype(BF16), b_ref[...].astype(BF16),
                         preferred_element_type=F32).astype(o_ref.dtype)


def matmul(a, b, *, tm, tn, out_dtype=F32, name="mm", layer=None):
    m, k = a.shape
    n = b.shape[-1]
    assert m % tm == 0 and n % tn == 0, (a.shape, b.shape, tm, tn)
    if layer is None:
        b_spec = pl.BlockSpec((k, tn), lambda i, j: (0, j))
    else:
        b_spec = pl.BlockSpec((None, k, tn), lambda i, j: (layer, 0, j))
    return pl.pallas_call(
        _mm_body,
        out_shape=jax.ShapeDtypeStruct((m, n), out_dtype),
        grid=(m // tm, n // tn),
        in_specs=[pl.BlockSpec((tm, k), lambda i, j: (i, 0)), b_spec],
        out_specs=pl.BlockSpec((tm, tn), lambda i, j: (i, j)),
        compiler_params=_cparams(("parallel", "arbitrary")),
        name=name,
    )(a, b)


def _scan_consts():
    c = CHUNK
    cum = np.zeros((2, 2 * c + 8, c), np.float32)
    masks = np.zeros((2, len(LEVELS), c, c), np.float32)
    tri = np.tril(np.ones((c, c), np.float32))
    cum[0, :c] = tri
    cum[0, c:2 * c] = tri
    cum[1, :c] = tri.T
    cum[1, c:2 * c] = np.triu(np.ones((c, c), np.float32), 1)
    cum[:, 2 * c:] = 1.0
    for li, s in enumerate(LEVELS):
        for i in range(c):
            for j in range(c):
                if (i & s) and not (j & s) and i // (2 * s) == j // (2 * s):
                    masks[0, li, i, j] = 1.0
        masks[1, li] = masks[0, li].T
    return jnp.asarray(cum, BF16), jnp.asarray(masks)


def _split3(x):
    hi = x.astype(BF16)
    r1 = x - hi.astype(F32)
    mid = r1.astype(BF16)
    lo = (r1 - mid.astype(F32)).astype(BF16)
    return hi, mid, lo


def _cumsums(cum, la):
    d = la.shape[1]
    r = jnp.dot(cum, jnp.concatenate(_split3(la), axis=1), preferred_element_type=F32)
    return r[:, :d] + r[:, d:2 * d] + r[:, 2 * d:]


def _boundary(cc, s):
    c, d = cc.shape
    if s >= 8:
        parts = [jnp.broadcast_to(cc[r0 + s - 1:r0 + s], (2 * s, d)) for r0 in range(0, c, 2 * s)]
        return parts[0] if len(parts) == 1 else jnp.concatenate(parts, axis=0)
    cc3 = cc.reshape(c // 8, 8, d)
    if s == 4:
        return jnp.broadcast_to(cc3[:, 3:4], cc3.shape).reshape(c, d)
    sub = lax.broadcasted_iota(jnp.int32, cc3.shape, 1)
    lo = jnp.broadcast_to(cc3[:, 1:2], cc3.shape)
    hi = jnp.broadcast_to(cc3[:, 5:6], cc3.shape)
    return jnp.where(sub < 4, lo, hi).reshape(c, d)


def _nt(a, b):
    return lax.dot_general(a, b, (((1,), (1,)), ((), ())), preferred_element_type=F32)


def _tn(a, b):
    return lax.dot_general(a, b, (((0,), (0,)), ((), ())), preferred_element_type=F32)


def _gla_body(cum_ref, mask_ref, q_ref, k_ref, v_ref, la_ref, o_ref, st_ref, *, heads, dk, dv, scale):
    @pl.when(pl.program_id(2) == 0)
    def _():
        st_ref[...] = jnp.zeros_like(st_ref)

    cum = cum_ref[0]
    c = CHUNK
    hs = range(heads)
    qs_, ks_, vs_, las, bqs, ccs, tots = [], [], [], [], [], [], []
    for h in hs:
        la = la_ref[0, 0, :, h * dk:(h + 1) * dk]
        rs = _cumsums(cum, la)
        qs_.append(q_ref[0, :, h * dk:(h + 1) * dk] * scale)
        ks_.append(k_ref[0, :, h * dk:(h + 1) * dk])
        vs_.append(v_ref[0, :, h * dv:(h + 1) * dv])
        las.append(la)
        bqs.append(rs[:c])
        ccs.append(rs[c:2 * c])
        tots.append(rs[2 * c:2 * c + 1])
    sc = [None] * heads
    for li, s in enumerate(LEVELS):
        for h in hs:
            q, k = qs_[h], ks_[h]
            if s == 1:
                qd, kd = q * jnp.exp(las[h]), k
            else:
                g = bqs[h] - _boundary(ccs[h], s)
                qd = q * jnp.exp(jnp.minimum(g, 0.0))
                kd = k * jnp.exp(jnp.minimum(-g, 0.0))
            p = _nt(qd.astype(BF16), kd.astype(BF16)) * mask_ref[0, li]
            sc[h] = p if sc[h] is None else sc[h] + p
    for h in hs:
        q, k, v, bq, tot = qs_[h], ks_[h], vs_[h], bqs[h], tots[h]
        vb = v.astype(BF16)
        dg = jnp.sum(q * k, axis=1, keepdims=True)
        st = st_ref[h]
        o = (jnp.dot(sc[h].astype(BF16), vb, preferred_element_type=F32) + dg * v
             + _nt((q * jnp.exp(bq)).astype(BF16), st.astype(BF16)))
        o_ref[0, 0, :, h * dv:(h + 1) * dv] = o
        kbar = k * jnp.exp(tot - bq)
        st_ref[h] = st * jnp.exp(tot) + _tn(vb, kbar.astype(BF16))


def _ret_body(lam_ref, q_ref, k_ref, v_ref, o_ref, st_ref, *, heads, dk, dv):
    d = pl.program_id(0)

    @pl.when(pl.program_id(2) == 0)
    def _():
        st_ref[...] = jnp.zeros_like(st_ref)

    c = CHUNK
    ri = lax.broadcasted_iota(jnp.int32, (c, c), 0)
    ci = lax.broadcasted_iota(jnp.int32, (c, c), 1)
    dist = jnp.where(d == 0, ri - ci, ci - ri)
    live = dist >= 0
    distf = jnp.maximum(dist, 0).astype(F32)
    row = lax.broadcasted_iota(jnp.int32, (c, 1), 0)
    pos = jnp.where(d == 0, row + 1, c - row).astype(F32)
    for h in range(heads):
        lam = lam_ref[d * heads + h]
        q = q_ref[0, :, h * dk:(h + 1) * dk]
        k = k_ref[0, :, h * dk:(h + 1) * dk]
        v = v_ref[0, :, h * dv:(h + 1) * dv]
        vb = v.astype(BF16)
        decay = jnp.where(live, jnp.exp(lam * distf), 0.0)
        sc = _nt(q.astype(BF16), k.astype(BF16)) * decay
        st = st_ref[h]
        o = (jnp.dot(sc.astype(BF16), vb, preferred_element_type=F32)
             + _nt((q * jnp.exp(lam * pos)).astype(BF16), st.astype(BF16)))
        o_ref[0, 0, :, h * dv:(h + 1) * dv] = o
        kbar = k * jnp.exp(lam * (c - pos))
        st_ref[h] = st * jnp.exp(lam * c) + _tn(vb, kbar.astype(BF16))


def _chunk_index(d, c, n_ctx, n_all):
    back = jnp.where(c < n_ctx, n_ctx - 1 - c, n_ctx + n_all - 1 - c)
    return jnp.where(d == 0, c, back)


def _col_spec(width, col, ch):
    assert col % width == 0
    return pl.BlockSpec((1, CHUNK, width), lambda d, b, c, *_: (b, ch(d, c), col // width))


def gla_scan(z, la2, n_ctx):
    bsz, t, _ = z.shape
    heads, dk, dv = GLA_HEADS, GLA_DK, GLA_DV
    n_all = t // CHUNK
    cum, masks = _scan_consts()
    ch = functools.partial(_chunk_index, n_ctx=n_ctx, n_all=n_all)
    body = functools.partial(_gla_body, heads=heads, dk=dk, dv=dv, scale=dk ** -0.5)
    return pl.pallas_call(
        body,
        out_shape=jax.ShapeDtypeStruct((2, bsz, t, heads * dv), F32),
        grid=(2, bsz, n_all),
        in_specs=[
            pl.BlockSpec((1,) + cum.shape[1:], lambda d, b, c: (d, 0, 0)),
            pl.BlockSpec((1,) + masks.shape[1:], lambda d, b, c: (d, 0, 0, 0)),
            _col_spec(heads * dk, COL_GLA_Q, ch),
            _col_spec(heads * dk, COL_GLA_K, ch),
            _col_spec(heads * dv, COL_GLA_V, ch),
            pl.BlockSpec((1, 1, CHUNK, heads * dk), lambda d, b, c: (d, b, ch(d, c), 0)),
        ],
        out_specs=pl.BlockSpec((1, 1, CHUNK, heads * dv), lambda d, b, c: (d, b, ch(d, c), 0)),
        scratch_shapes=[pltpu.VMEM((heads, dv, dk), F32)],
        compiler_params=_cparams(("parallel", "parallel", "arbitrary")),
        name="gla_scan",
    )(cum, masks, z, z, z, la2)


def ret_scan(q, k, z, lam, n_ctx):
    bsz, t, _ = q.shape
    heads, dk, dv = RET_HEADS, RET_DK, RET_DV
    n_all = t // CHUNK
    ch = functools.partial(_chunk_index, n_ctx=n_ctx, n_all=n_all)
    body = functools.partial(_ret_body, heads=heads, dk=dk, dv=dv)
    grid_spec = pltpu.PrefetchScalarGridSpec(
        num_scalar_prefetch=1,
        grid=(2, bsz, n_all),
        in_specs=[
            pl.BlockSpec((1, CHUNK, heads * dk), lambda d, b, c, lam: (b, ch(d, c), 0)),
            pl.BlockSpec((1, CHUNK, heads * dk), lambda d, b, c, lam: (b, ch(d, c), 0)),
            _col_spec(heads * dv, COL_RET_V, ch),
        ],
        out_specs=pl.BlockSpec((1, 1, CHUNK, heads * dv), lambda d, b, c, lam: (d, b, ch(d, c), 0)),
        scratch_shapes=[pltpu.VMEM((heads, dv, dk), F32)],
    )
    return pl.pallas_call(
        body,
        out_shape=jax.ShapeDtypeStruct((2, bsz, t, heads * dv), F32),
        grid_spec=grid_spec,
        compiler_params=_cparams(("parallel", "parallel", "arbitrary")),
        name="ret_scan",
    )(lam.reshape(-1).astype(F32), q, k, z)


RWKV_GROUP = 4
RWKV_GW = RWKV_GROUP * RWKV_HEAD


def _rwkv_consts():
    c = CHUNK
    gw = RWKV_GW
    cum = np.zeros((2, 2 * c + 16, c), np.float32)
    tri = np.tril(np.ones((c, c), np.float32))
    mid = c // 2
    cum[0, :c] = tri - tri[mid - 1][None, :]
    cum[0, c:2 * c] = 1.0 - tri
    cum[0, 2 * c + 8:] = tri[mid - 1][None, :]
    cum[1, :c] = tri.T - tri.T[mid][None, :]
    cum[1, c:2 * c] = 1.0 - tri.T
    cum[1, 2 * c + 8:] = tri.T[mid][None, :]
    cum[:, 2 * c:2 * c + 8] = 1.0
    t = np.arange(c)[:, None]
    s = np.tile(np.arange(c), RWKV_GROUP)[None, :]
    masks = np.zeros((2, 3, c, gw), np.float32)
    masks[0, 0], masks[0, 1] = s < t, s <= t
    masks[1, 0], masks[1, 1] = s > t, s >= t
    masks[:, 2] = s == t
    return jnp.asarray(cum, BF16), jnp.asarray(masks)


def _rwkv_body(cum_ref, mask_ref, r_ref, k_ref, v_ref, kk_ref, kka_ref, lw_ref, o_ref, st_ref, *, groups):
    @pl.when(pl.program_id(2) == 0)
    def _():
        st_ref[...] = jnp.zeros_like(st_ref)

    c, gw = CHUNK, RWKV_GW
    cum = cum_ref[0]
    strict, incl, eye = mask_ref[0, 0], mask_ref[0, 1], mask_ref[0, 2]
    rb = lax.broadcasted_iota(jnp.int32, (gw, gw), 0) // RWKV_HEAD
    cb = lax.broadcasted_iota(jnp.int32, (gw, gw), 1) // RWKV_HEAD
    same_head = rb == cb

    def bd(x):
        xb = x.astype(BF16)
        return jnp.where(same_head, jnp.concatenate([xb] * RWKV_GROUP, axis=0), jnp.zeros((), BF16))

    def mm(a, b):
        return jnp.dot(a.astype(BF16), b, preferred_element_type=F32)

    gs = range(groups)
    sls = [slice(g * gw, (g + 1) * gw) for g in gs]
    pre = []
    for sl in sls:
        r, k, v = r_ref[0, :, sl], k_ref[0, :, sl], v_ref[0, :, sl]
        kk, kka, lw = kk_ref[0, :, sl], kka_ref[0, :, sl], lw_ref[0, 0, :, sl]
        rs = _cumsums(cum, lw)
        gm, ge, tot, gmid = rs[:c], rs[c:2 * c], rs[2 * c:2 * c + 1], rs[2 * c + 8:2 * c + 9]
        e_in, e_out, e_end = jnp.exp(gm), jnp.exp(-gm), jnp.exp(ge)
        kt_rel = kk * (e_in * jnp.exp(-lw))
        lhs = jnp.concatenate([kt_rel, r * e_in], axis=0)
        pre.append(dict(lhs=lhs.astype(BF16), lhs_abs=(lhs * jnp.exp(gmid)).astype(BF16),
                        kka_o=kka * e_out, k_o=k * e_out, v=v, tot=tot,
                        kv_end=jnp.concatenate([k * e_end, kka * e_end], axis=0).astype(BF16)))
    s_b = [_nt(q["lhs"], bd(q["kka_o"])) for q in pre]
    s_k = [_nt(q["lhs"], bd(q["k_o"])) for q in pre]
    ns = [-(s[:c] * strict) for s in s_b]
    xs = [eye + n for n in ns]
    ps = [mm(n, bd(n)) for n in ns]
    for _ in range(4):
        pps = [mm(jnp.concatenate([p, x], axis=0), bd(p)) for p, x in zip(ps, xs)]
        ps = [pp[:c] for pp in pps]
        xs = [x + pp[c:] for x, pp in zip(xs, pps)]
    xs = [x + mm(x, bd(p)) for p, x in zip(ps, xs)]
    sts = [st_ref[g] for g in gs]
    hs = [_nt(q["lhs_abs"], st.astype(BF16)) for q, st in zip(pre, sts)]
    vs = [mm(jnp.concatenate([sk[:c] * strict, sk[c:] * incl], axis=0), bd(q["v"])) for sk, q in zip(s_k, pre)]
    us = [mm(x, bd(h[:c] + w[:c])) for x, h, w in zip(xs, hs, vs)]
    for g in gs:
        o_ref[0, 0, :, sls[g]] = hs[g][c:] + vs[g][c:] - mm(s_b[g][c:] * incl, bd(us[g]))
    for g in gs:
        upd = _tn(jnp.concatenate([pre[g]["v"], -us[g]], axis=0).astype(BF16), pre[g]["kv_end"])
        st_ref[g] = sts[g] * jnp.exp(pre[g]["tot"]) + jnp.where(same_head, upd, 0.0)


def rwkv_scan(r, k, v, kk, kka, lw2, n_ctx):
    bsz, t, ch_w = r.shape
    groups = ch_w // RWKV_GW
    n_all = t // CHUNK
    cum, masks = _rwkv_consts()
    ch = functools.partial(_chunk_index, n_ctx=n_ctx, n_all=n_all)
    tok = pl.BlockSpec((1, CHUNK, ch_w), lambda d, b, c: (b, ch(d, c), 0))
    dir_tok = pl.BlockSpec((1, 1, CHUNK, ch_w), lambda d, b, c: (d, b, ch(d, c), 0))
    return pl.pallas_call(
        functools.partial(_rwkv_body, groups=groups),
        out_shape=jax.ShapeDtypeStruct((2, bsz, t, ch_w), F32),
        grid=(2, bsz, n_all),
        in_specs=[
            pl.BlockSpec((1,) + cum.shape[1:], lambda d, b, c: (d, 0, 0)),
            pl.BlockSpec((1,) + masks.shape[1:], lambda d, b, c: (d, 0, 0, 0)),
            tok, tok, tok, tok, tok, dir_tok,
        ],
        out_specs=dir_tok,
        scratch_shapes=[pltpu.VMEM((groups, RWKV_GW, RWKV_GW), F32)],
        compiler_params=_cparams(("parallel", "parallel", "arbitrary")),
        name="rwkv_scan",
    )(cum, masks, r, k, v, kk, kka, lw2)


def _split2(x):
    hi = x.astype(BF16)
    return hi, (x - hi.astype(F32)).astype(BF16)


ROUTER_LANES = 128


def _router_body(h_ref, w_ref, bias_ref, idx_ref, gate_ref, *, n_experts, top_k):
    a_hi, a_mid = _split2(h_ref[...])
    b_hi, b_mid = _split2(w_ref[...])
    dot = functools.partial(jnp.dot, preferred_element_type=F32)
    scores = jax.nn.sigmoid(dot(a_hi, b_hi) + (dot(a_hi, b_mid) + dot(a_mid, b_hi)))
    lane = lax.broadcasted_iota(jnp.int32, scores.shape, 1)
    cand = jnp.where(lane < n_experts, scores + bias_ref[...], -jnp.inf)
    idx_out = jnp.zeros(scores.shape, jnp.int32)
    val_out = jnp.zeros(scores.shape, F32)
    total = jnp.zeros((scores.shape[0], 1), F32)
    for j in range(top_k):
        best = jnp.max(cand, axis=1, keepdims=True)
        arg = jnp.min(jnp.where(cand == best, lane, ROUTER_LANES), axis=1, keepdims=True)
        pick = lane == arg
        val = jnp.sum(jnp.where(pick, scores, 0.0), axis=1, keepdims=True)
        cand = jnp.where(pick, -jnp.inf, cand)
        idx_out = jnp.where(lane == j, arg, idx_out)
        val_out = jnp.where(lane == j, val, val_out)
        total = total + val
    idx_ref[...] = idx_out
    gate_ref[...] = ROUTED_SCALE * val_out / total


def router(h, router_w, router_bias, layer):
    t, d = h.shape
    e = router_w.shape[-1]
    tm = _tile(t, 600)
    w = jnp.pad(router_w[layer].astype(F32), ((0, 0), (0, ROUTER_LANES - e)))
    bias = jnp.pad(router_bias[layer].astype(F32), (0, ROUTER_LANES - e))[None]
    idx, gates = pl.pallas_call(
        functools.partial(_router_body, n_experts=e, top_k=TOP_K),
        out_shape=(jax.ShapeDtypeStruct((t, ROUTER_LANES), jnp.int32), jax.ShapeDtypeStruct((t, ROUTER_LANES), F32)),
        grid=(t // tm,),
        in_specs=[pl.BlockSpec((tm, d), lambda i: (i, 0)), pl.BlockSpec((d, ROUTER_LANES), lambda i: (0, 0)),
                  pl.BlockSpec((1, ROUTER_LANES), lambda i: (0, 0))],
        out_specs=(pl.BlockSpec((tm, ROUTER_LANES), lambda i: (i, 0)), pl.BlockSpec((tm, ROUTER_LANES), lambda i: (i, 0))),
        compiler_params=_cparams(("parallel",)),
        name="router",
    )(h, w, bias)
    return idx[:, :TOP_K], gates[:, :TOP_K]


MOE_BM = 256


def _tile(m, cap, mult=16):
    return max(t for t in range(mult, cap + 1, mult) if m % t == 0)


def _pack_pairs(x):
    n = x.shape[-1] // 2
    bits = lax.bitcast_convert_type(x, jnp.uint32)
    bits = (bits + jnp.uint32(0x7FFF) + ((bits >> 16) & jnp.uint32(1))) & jnp.uint32(0xFFFF0000)
    return bits[..., :n] | (bits[..., n:] >> 16)


def _unpack_pairs(w):
    hi = lax.bitcast_convert_type(w & jnp.uint32(0xFFFF0000), F32)
    lo = lax.bitcast_convert_type(w << 16, F32)
    return jnp.concatenate([hi, lo], axis=-1)


def _swiglu(x, wg, wu, wd, gate=None):
    h1 = jnp.dot(x, wg, preferred_element_type=F32)
    h2 = jnp.dot(x, wu, preferred_element_type=F32)
    a = h1 * jax.nn.sigmoid(h1) * h2
    if gate is not None:
        a = a * gate
    return jnp.dot(a.astype(BF16), wd, preferred_element_type=F32)


def _cast_rows(src, dst, rows):
    def body(j, carry):
        r = pl.multiple_of(j * rows, rows)
        dst[pl.ds(r, rows), :] = src[pl.ds(r, rows), :].astype(BF16)
        return carry
    lax.fori_loop(0, src.shape[0] // rows, body, 0)


def _experts_body(be_ref, nx_ref, tok_ref, h_hbm, gate_ref, wg_hbm, wu_hbm, wd_hbm, o_ref,
                  xbuf0, xbuf1, st_g, st_u, st_d, wg_b, wu_b, wd_b, xsem, wsem, *, layer, bm):
    i = pl.program_id(0)
    n = pl.num_programs(0)
    slot = lax.rem(i, 2)

    def row_copy(base, r, buf, s):
        return pltpu.make_async_copy(h_hbm.at[pl.ds(tok_ref[base + r], 1)], buf.at[pl.ds(r, 1)], xsem.at[s])

    def rows_wait(buf, s):
        pltpu.make_async_copy(h_hbm.at[pl.ds(0, bm)], buf, xsem.at[s]).wait()

    def weight_copies(e):
        return (pltpu.make_async_copy(wg_hbm.at[layer, e], st_g, wsem.at[0]),
                pltpu.make_async_copy(wu_hbm.at[layer, e], st_u, wsem.at[1]),
                pltpu.make_async_copy(wd_hbm.at[layer, e], st_d, wsem.at[2]))

    @pl.when(i == 0)
    def _():
        def body(r, carry):
            row_copy(0, r, xbuf0, 0).start()
            return carry
        lax.fori_loop(0, bm, body, 0, unroll=8)
        for cp in weight_copies(be_ref[0]):
            cp.start()

    @pl.when((i == 0) | (be_ref[i] != be_ref[jnp.maximum(i - 1, 0)]))
    def _():
        for cp in weight_copies(be_ref[i]):
            cp.wait()
        _cast_rows(st_g, wg_b, 256)
        _cast_rows(st_u, wu_b, 256)
        _cast_rows(st_d, wd_b, 32)

        @pl.when(nx_ref[i] >= 0)
        def _():
            for cp in weight_copies(nx_ref[i]):
                cp.start()

    def step(cur, cur_s, nxt, nxt_s):
        rows_wait(cur, cur_s)
        base = jnp.minimum(i + 1, n - 1) * bm
        for r in range(bm):
            row_copy(base, r, nxt, nxt_s).start()
        x = _unpack_pairs(cur[...]).astype(BF16)
        o_ref[...] = _pack_pairs(_swiglu(x, wg_b[...], wu_b[...], wd_b[...], gate_ref[...]))

        @pl.when(i == n - 1)
        def _():
            rows_wait(nxt, nxt_s)

    @pl.when(slot == 0)
    def _():
        step(xbuf0, 0, xbuf1, 1)

    @pl.when(slot == 1)
    def _():
        step(xbuf1, 1, xbuf0, 0)


def experts(h_pad, gate_rows, exp_gate, exp_up, exp_down, layer, block_expert, next_expert, buf_tok):
    bm = MOE_BM
    n = buf_tok.shape[0]
    d = exp_gate.shape[-2]
    f = exp_gate.shape[-1]
    dp = h_pad.shape[1]
    any_spec = pl.BlockSpec(memory_space=pl.ANY)
    grid_spec = pltpu.PrefetchScalarGridSpec(
        num_scalar_prefetch=3,
        grid=(n // bm,),
        in_specs=[any_spec, pl.BlockSpec((bm, 1), lambda i, *_: (i, 0)), any_spec, any_spec, any_spec],
        out_specs=pl.BlockSpec((bm, dp), lambda i, *_: (i, 0)),
        scratch_shapes=[
            pltpu.VMEM((bm, dp), jnp.uint32), pltpu.VMEM((bm, dp), jnp.uint32),
            pltpu.VMEM((d, f), F32), pltpu.VMEM((d, f), F32), pltpu.VMEM((f, d), F32),
            pltpu.VMEM((d, f), BF16), pltpu.VMEM((d, f), BF16), pltpu.VMEM((f, d), BF16),
            pltpu.SemaphoreType.DMA((2,)), pltpu.SemaphoreType.DMA((3,)),
        ],
    )
    return pl.pallas_call(
        functools.partial(_experts_body, layer=layer, bm=bm),
        out_shape=jax.ShapeDtypeStruct((n, dp), jnp.uint32),
        grid_spec=grid_spec,
        compiler_params=_cparams(("arbitrary",)),
        name="experts",
    )(block_expert, next_expert, buf_tok, h_pad, gate_rows, exp_gate, exp_up, exp_down)


def _shared_body(x_ref, wg_ref, wu_ref, wd_ref, o_ref):
    o_ref[...] = _swiglu(x_ref[...].astype(BF16), wg_ref[...].astype(BF16), wu_ref[...].astype(BF16),
                         wd_ref[...].astype(BF16))


def shared_expert(h, sh_gate, sh_up, sh_down, layer):
    t, d = h.shape
    f = sh_gate.shape[-1]
    tm = _tile(t, 300)
    once = pl.Buffered(1)
    return pl.pallas_call(
        _shared_body,
        out_shape=jax.ShapeDtypeStruct((t, d), F32),
        grid=(t // tm,),
        in_specs=[pl.BlockSpec((tm, d), lambda i: (i, 0)),
                  pl.BlockSpec((None, d, f), lambda i: (layer, 0, 0), pipeline_mode=once),
                  pl.BlockSpec((None, d, f), lambda i: (layer, 0, 0), pipeline_mode=once),
                  pl.BlockSpec((None, f, d), lambda i: (layer, 0, 0), pipeline_mode=once)],
        out_specs=pl.BlockSpec((tm, d), lambda i: (i, 0)),
        compiler_params=_cparams(("arbitrary",)),
        name="shared",
    )(h, sh_gate, sh_up, sh_down)


COMBINE_TB = 64


def _combine_body(pos_ref, y_hbm, sh_ref, o_ref, buf, sem, *, tb, k):
    i = pl.program_id(0)
    n = pl.num_programs(0)
    slot = lax.rem(i, 2)

    def gather(blk, s):
        base = blk * tb * k

        def body(t, carry):
            for kk in range(k):
                p = pos_ref[base + t * k + kk]
                pltpu.make_async_copy(y_hbm.at[pl.ds(p, 1)], buf.at[s, pl.ds(kk * tb + t, 1)], sem.at[s]).start()
            return carry
        lax.fori_loop(0, tb, body, 0, unroll=4)

    @pl.when(i == 0)
    def _():
        gather(0, 0)

    @pl.when(i + 1 < n)
    def _():
        gather(i + 1, 1 - slot)

    pltpu.make_async_copy(y_hbm.at[pl.ds(0, tb * k)], buf.at[slot], sem.at[slot]).wait()
    acc = sh_ref[...]
    for kk in range(k):
        acc = acc + _unpack_pairs(buf[slot, kk * tb:(kk + 1) * tb, :])
    o_ref[...] = acc


def combine(y, pos, shared):
    t, k = pos.shape
    d = shared.shape[1]
    dp = y.shape[1]
    tb = COMBINE_TB
    grid_spec = pltpu.PrefetchScalarGridSpec(
        num_scalar_prefetch=1,
        grid=(t // tb,),
        in_specs=[pl.BlockSpec(memory_space=pl.ANY), pl.BlockSpec((tb, d), lambda i, *_: (i, 0))],
        out_specs=pl.BlockSpec((tb, d), lambda i, *_: (i, 0)),
        scratch_shapes=[pltpu.VMEM((2, tb * k, dp), jnp.uint32), pltpu.SemaphoreType.DMA((2,))],
    )
    return pl.pallas_call(
        functools.partial(_combine_body, tb=tb, k=k),
        out_shape=jax.ShapeDtypeStruct((t, d), F32),
        grid_spec=grid_spec,
        compiler_params=_cparams(("arbitrary",)),
        name="combine",
    )(pos.reshape(-1), y, shared)


def moe_ffn(h, layer, router_w, router_bias, exp_gate, exp_up, exp_down, sh_gate, sh_up, sh_down):
    t, d = h.shape
    e, k, bm = N_EXPERTS, TOP_K, MOE_BM
    i32 = jnp.int32
    idx, gates = router(h, router_w, router_bias, layer)
    flat_e = idx.reshape(-1)
    iota = jnp.arange(t * k, dtype=i32)
    e_sorted, order = lax.sort_key_val(flat_e, iota)
    _, rank = lax.sort_key_val(order, iota)
    bounds = jnp.searchsorted(e_sorted, jnp.arange(e + 1, dtype=i32), side='left',
                              method='compare_all').astype(i32)
    first, counts = bounds[:-1], bounds[1:] - bounds[:-1]
    padded = (counts + bm - 1) // bm * bm
    ends = jnp.cumsum(padded)
    starts = ends - padded
    pos = (iota + (starts - first)[e_sorted])[rank].reshape(t, k)
    n_blocks = -(-(t * k) // bm) + e
    blk_start = jnp.arange(n_blocks, dtype=i32) * bm
    block_valid = (blk_start < ends[-1]).astype(i32)
    last_e = jnp.max(jnp.where(counts > 0, jnp.arange(e, dtype=i32), 0))
    block_expert = jnp.where(block_valid > 0,
                             jnp.minimum(jnp.searchsorted(ends, blk_start, side='right',
                                                          method='compare_all').astype(i32), e - 1), last_e)
    present = jnp.where(counts > 0, jnp.arange(e, dtype=i32), e)
    nxt = lax.cummin(jnp.concatenate([present[1:], jnp.full((1,), e, i32)]), reverse=True)
    next_expert = jnp.where(nxt < e, nxt, -1)[block_expert]
    blk_off = blk_start - starts[block_expert]
    blk_live = jnp.where(block_valid > 0, jnp.clip(counts[block_expert] - blk_off, 0, bm), 0)
    lane = jnp.arange(bm, dtype=i32)[None, :]
    live = (lane < blk_live[:, None]).reshape(-1)
    src = order[jnp.clip((first[block_expert] + blk_off)[:, None] + lane, 0, t * k - 1).reshape(-1)]
    buf_tok = jnp.where(live, src // k, t).astype(i32)
    gate_rows = jnp.where(live, gates.reshape(-1)[src], 0.0)[:, None]

    h_pad = jnp.concatenate([_pack_pairs(h), jnp.zeros((1, d // 2), jnp.uint32)], 0)
    y = experts(h_pad, gate_rows, exp_gate, exp_up, exp_down, layer, block_expert, next_expert, buf_tok)
    return combine(y, pos, shared_expert(h, sh_gate, sh_up, sh_down, layer))


def _permute_cols(w):
    gla, ret, rw = w[..., :GLA_N], w[..., GLA_N:GLA_N + RET_N], w[..., GLA_N + RET_N:]
    rkv = 3 * RWKV_WIDTH
    lora = rkv + RWKV_DECAY_RANK + RWKV_ICLR_RANK

    def zeros(n):
        return jnp.zeros(w.shape[:-1] + (n,), w.dtype)

    parts = [gla[..., :GLA_N - GLA_RANK],
             rw[..., lora:], zeros(COL_RET_Q - COL_RW_G1 - RWKV_GATE_RANK),
             ret,
             rw[..., :lora],
             gla[..., GLA_N - GLA_RANK:], zeros(N_IN_PAD - COL_GLA_LR - GLA_RANK)]
    return jnp.concatenate(parts, -1)


def _rmsnorm(x, g):
    return x * lax.rsqrt(jnp.mean(x * x, -1, keepdims=True) + RMS_EPS) * g


def _rope(x, pos):
    d = x.shape[-1]
    inv = ROPE_BASE ** (-jnp.linspace(0.0, 1.0, d // 2, dtype=F32))
    ang = pos.astype(F32)[:, None] * inv[None, :]
    cos, sin = jnp.cos(ang)[:, None, :], jnp.sin(ang)[:, None, :]
    x1, x2 = x[..., :d // 2], x[..., d // 2:]
    return jnp.concatenate([x1 * cos - x2 * sin, x1 * sin + x2 * cos], -1)


def _small_mm(a, b, name):
    bsz, t, k = a.shape
    m = bsz * t
    return matmul(a.reshape(m, k), b, tm=_tile(m, 1100), tn=b.shape[1], name=name).reshape(bsz, t, -1)


def gla_mixer(z, a2, a_bias, n_ctx):
    lr_pad = z[..., COL_GLA_LR:COL_GLA_LR + 128]
    la2 = jnp.stack([_small_mm(lr_pad, jnp.pad(a2[x], ((0, 128 - GLA_RANK), (0, 0))), "gla_gate")
                     for x in range(2)])
    la2 = jax.nn.log_sigmoid(la2 + a_bias[:, None, None, :]) / GLA_TAU
    return gla_scan(z, la2, n_ctx)


def ret_mixer(z, decay_logit, n_ctx):
    bsz, t, _ = z.shape
    q, k = z[..., COL_RET_Q:COL_RET_Q + RET_QK], z[..., COL_RET_K:COL_RET_K + RET_QK]
    pos = jnp.arange(t)
    q = _rope(q.reshape(bsz, t, RET_HEADS, RET_DK), pos).reshape(bsz, t, RET_QK)
    k = (_rope(k.reshape(bsz, t, RET_HEADS, RET_DK), pos) * RET_DK ** -0.5).reshape(bsz, t, RET_QK)
    lam = jax.nn.log_sigmoid(decay_logit.astype(F32))
    return ret_scan(q, k, z, lam, n_ctx)


POST_TM = 128


def _mixer_post_body(og_ref, or_ref, ow_ref, gg_ref, gr_ref, gw_ref, bonus_ref, nw_ref, lw_ref, lb_ref,
                     seg_ref, segt_ref, y_ref):
    def silu(u):
        return u * jax.nn.sigmoid(u)

    def head_rms(x):
        return x * lax.rsqrt(jnp.mean(x * x, axis=1, keepdims=True) + RMS_EPS)

    og = og_ref[0, 0] + og_ref[1, 0]
    gg = gg_ref[0]
    for h in range(GLA_HEADS):
        sl = slice(h * GLA_DV, (h + 1) * GLA_DV)
        y_ref[0, :, sl] = (head_rms(og[:, sl]) * nw_ref[...] * silu(gg[:, sl])).astype(y_ref.dtype)
    orr = or_ref[0, 0] + or_ref[1, 0]
    gr = gr_ref[0]
    for h in range(RET_HEADS):
        sl = slice(h * RET_DV, (h + 1) * RET_DV)
        y_ref[0, :, GLA_W + h * RET_DV:GLA_W + (h + 1) * RET_DV] = (
            head_rms(orr[:, sl]) * silu(gr[:, sl])).astype(y_ref.dtype)
    dot = functools.partial(jnp.dot, preferred_element_type=F32)

    def head_mean(x):
        hi, mid = _split2(x)
        s = (dot(hi, seg_ref[...]) + dot(mid, seg_ref[...])) * (1.0 / RWKV_HEAD)
        hi, mid = _split2(s)
        return dot(hi, segt_ref[...]) + dot(mid, segt_ref[...])

    ow = ow_ref[0, 0] + ow_ref[1, 0]
    xc = ow - head_mean(ow)
    gn = xc * lax.rsqrt(head_mean(xc * xc) + GN_EPS) * lw_ref[...] + lb_ref[...]
    y_ref[0, :, GLA_W + RET_W:] = ((gn + bonus_ref[0]) * gw_ref[0]).astype(y_ref.dtype)


def mixer_post(z, og2, or2, ow2, g_rw, bonus, norm_w, lnx_w, lnx_b, out_from):
    bsz, t, _ = z.shape
    tm = POST_TM
    off = out_from // tm
    assert out_from % tm == 0 and t % tm == 0
    cw = RWKV_WIDTH
    seg = np.zeros((cw, 128), np.float32)
    seg[np.arange(cw), np.arange(cw) // RWKV_HEAD] = 1.0
    params = [norm_w[None], lnx_w[None], lnx_b[None], jnp.asarray(seg, BF16), jnp.asarray(seg.T, BF16)]

    def two(width):
        return pl.BlockSpec((2, 1, tm, width), lambda b, j: (0, b, j + off, 0))

    def tok(width, blk=0):
        return pl.BlockSpec((1, tm, width), lambda b, j: (b, j + off, blk))

    def whole(x):
        return pl.BlockSpec(x.shape, lambda b, j, nd=x.ndim: (0,) * nd)

    d = GLA_W + RET_W + cw
    return pl.pallas_call(
        _mixer_post_body,
        out_shape=jax.ShapeDtypeStruct((bsz, t - out_from, d), BF16),
        grid=(bsz, (t - out_from) // tm),
        in_specs=[two(GLA_W), two(RET_W), two(cw), tok(GLA_W, COL_GLA_G // GLA_W), tok(RET_W, COL_RET_G // RET_W),
                  tok(cw), tok(cw)] + [whole(x) for x in params],
        out_specs=pl.BlockSpec((1, tm, d), lambda b, j: (b, j, 0)),
        compiler_params=_cparams(("parallel", "arbitrary")),
        name="mixer_post",
    )(og2, or2, ow2, z, z, g_rw, bonus, *params)


RW_FIELDS = ((COL_RW_R, RWKV_WIDTH, 0), (COL_RW_K, RWKV_WIDTH, RWKV_WIDTH), (COL_RW_V, RWKV_WIDTH, 2 * RWKV_WIDTH),
             (COL_RW_W1, RWKV_DECAY_RANK + RWKV_ICLR_RANK, 3 * RWKV_WIDTH),
             (COL_RW_G1, COL_RET_Q - COL_RW_G1, 3 * RWKV_WIDTH + RWKV_DECAY_RANK + RWKV_ICLR_RANK))


def _rwkv_prep_body(*refs, n_ctx, n_all):
    (tabs, zs, (w0_ref, w2_ref, a0_ref, a2_ref, g2_ref, kk_ref_, ka_ref, rk_ref, seg_ref, segt_ref),
     (r_out, k_out, v_out, kk_out, kka_out, lw_out, g_out, bonus_out)) = (
        refs[0:5], refs[5:20], refs[20:30], refs[30:38])
    j = pl.program_id(1)
    c = CHUNK
    is_ctx = j < n_ctx
    one = lambda cond: jnp.where(cond, 1.0, 0.0).astype(F32)
    f_prev = one(is_ctx & (j > 0))
    f_next = one(is_ctx & (j < n_ctx - 1))
    f_up = one((j > n_ctx) & (~is_ctx))
    f_down = one((j < n_all - 1) & (~is_ctx))
    f_ctx = one(is_ctx)
    row = lax.broadcasted_iota(jnp.int32, (c, 1), 0)
    fields = []
    for f in range(5):
        prv, cur, nxt = zs[3 * f][0], zs[3 * f + 1][0], zs[3 * f + 2][0]
        tab = tabs[f]
        mu, m_l, m_r, m_u, m_d, m_p, m_n = (tab[i:i + 1] for i in range(7))
        before = jnp.where(row == 0, prv[c - 1:c] * f_prev, pltpu.roll(cur, 1, axis=0))
        after = jnp.where(row == c - 1, nxt[0:1] * f_next, pltpu.roll(cur, c - 1, axis=0))
        shifted = ((m_l + f_ctx * (m_p - m_l)) * before + (m_r + f_ctx * (m_n - m_r)) * after
                   + (m_u * f_up) * prv + (m_d * f_down) * nxt)
        fields.append(cur + mu * (shifted - cur))
    r, k, v, wa, xg = fields
    dot = functools.partial(jnp.dot, preferred_element_type=F32)
    txw = jnp.tanh(wa[:, :RWKV_DECAY_RANK]).astype(BF16)
    for d in range(2):
        lw_out[d, 0] = -jax.nn.sigmoid(w0_ref[d:d + 1] + dot(txw, w2_ref[d])) * float(np.exp(-0.5))
    a = jax.nn.sigmoid(a0_ref[...] + dot(wa[:, RWKV_DECAY_RANK:].astype(BF16), a2_ref[...]))
    g_out[0] = dot(jax.nn.sigmoid(xg).astype(BF16), g2_ref[...])

    def head_sum(x):
        hi, mid = _split2(x)
        s = dot(hi, seg_ref[...]) + dot(mid, seg_ref[...])
        hi, mid = _split2(s)
        return dot(hi, segt_ref[...]) + dot(mid, segt_ref[...])

    kraw = k * kk_ref_[...]
    kk = kraw * lax.rsqrt(head_sum(kraw * kraw) + 1e-12)
    k2 = k * (1.0 + (a - 1.0) * ka_ref[...])
    r_out[0], k_out[0], v_out[0], kk_out[0], kka_out[0] = r, k2, v, kk, kk * a
    bonus_out[0] = head_sum(r * k2 * rk_ref[...]) * v


def rwkv_prep(z, mu, w0, w2, a0, a2, g2, k_k, k_a, r_k, n_ctx):
    bsz, t, _ = z.shape
    cw = RWKV_WIDTH
    n_all = t // CHUNK
    assert CHUNK == GRID_W
    quarter, half = RWKV_N // 4, RWKV_N // 2
    tabs, z_specs = [], []
    for col, width, ch0 in RW_FIELDS:
        n_real = min(width, RWKV_N - ch0)
        ch = ch0 + np.arange(width)
        real = np.arange(width) < n_real
        tab = np.zeros((8, width), np.float32)
        tab[1], tab[2] = real & (ch < quarter), real & (ch >= quarter) & (ch < 2 * quarter)
        tab[3], tab[4] = real & (ch >= 2 * quarter) & (ch < 3 * quarter), real & (ch >= 3 * quarter)
        tab[5], tab[6] = real & (ch < half), real & (ch >= half)
        tab = jnp.asarray(tab).at[0, :n_real].set(mu[ch0:ch0 + n_real])
        tabs.append(tab)
        blk = col // width
        assert col % width == 0
        z_specs += [pl.BlockSpec((1, CHUNK, width), lambda b, j, blk=blk: (b, jnp.maximum(j - 1, 0), blk)),
                    pl.BlockSpec((1, CHUNK, width), lambda b, j, blk=blk: (b, j, blk)),
                    pl.BlockSpec((1, CHUNK, width), lambda b, j, blk=blk: (b, jnp.minimum(j + 1, n_all - 1), blk))]
    gpad = RW_FIELDS[4][1] - RWKV_GATE_RANK
    seg = np.zeros((cw, 128), np.float32)
    seg[np.arange(cw), np.arange(cw) // RWKV_HEAD] = 1.0
    params = [w0, w2.astype(BF16), a0[None], a2.astype(BF16), jnp.pad(g2, ((0, gpad), (0, 0))).astype(BF16),
              k_k[None], k_a[None], r_k.reshape(1, cw), jnp.asarray(seg, BF16), jnp.asarray(seg.T, BF16)]

    def whole(x):
        return pl.BlockSpec(x.shape, lambda b, j, nd=x.ndim: (0,) * nd)

    tok = pl.BlockSpec((1, CHUNK, cw), lambda b, j: (b, j, 0))
    tok_sds = jax.ShapeDtypeStruct((bsz, t, cw), F32)
    return pl.pallas_call(
        functools.partial(_rwkv_prep_body, n_ctx=n_ctx, n_all=n_all),
        out_shape=(tok_sds,) * 5 + (jax.ShapeDtypeStruct((2, bsz, t, cw), F32), tok_sds, tok_sds),
        grid=(bsz, n_all),
        in_specs=[whole(x) for x in tabs] + z_specs + [whole(x) for x in params],
        out_specs=(tok,) * 5 + (pl.BlockSpec((2, 1, CHUNK, cw), lambda b, j: (0, b, j, 0)), tok, tok),
        compiler_params=_cparams(("parallel", "arbitrary")),
        name="rwkv_prep",
    )(*tabs, *([z] * 15), *params)


def rwkv_mixer(z, mu, w0, w2, a0, a2, g2, k_k, k_a, r_k, n_ctx):
    r, k, v, kk, kka, lw2, g, bonus = rwkv_prep(z, mu, w0, w2, a0, a2, g2, k_k, k_a, r_k, n_ctx)
    return rwkv_scan(r, k, v, kk, kka, lw2, n_ctx), g, bonus


def kernel(x, c, ctx, c_ctx, ada_w, ada_b, pre_mix, post_mix, pre_ffn, post_ffn, w_in, w_out, gla_a2, gla_a_bias, gla_norm_w, ret_decay_logit, rwkv_mu, rwkv_w0, rwkv_w2, rwkv_a0, rwkv_a2, rwkv_g2, rwkv_k_k, rwkv_k_a, rwkv_r_k, rwkv_lnx_w, rwkv_lnx_b, router_w, router_bias, exp_gate, exp_up, exp_down, shared_gate, shared_up, shared_down):
    bsz, s, d = x.shape
    lc = ctx.shape[1]
    t = lc + s
    depth = ada_w.shape[0]
    n_ctx = lc // CHUNK
    for l in range(depth):
        ctx_out = l < depth - 1
        out_from = 0 if ctx_out else lc
        cvec = jnp.concatenate([jax.nn.silu(c), jax.nn.silu(c_ctx)[None], jnp.zeros((8 - bsz - 1, d), F32)], 0)
        mod = matmul(cvec, ada_w, tm=8, tn=1024, name="adaln", layer=l) + ada_b[l]
        mx = [m[:, None, :] for m in jnp.split(mod[:bsz], 6, axis=-1)]
        mc = jnp.split(mod[bsz], 6, axis=-1)

        hx = _rmsnorm(x, pre_mix[l]) * (1.0 + mx[1]) + mx[0]
        hc = _rmsnorm(ctx, pre_mix[l]) * (1.0 + mc[1]) + mc[0]
        h_all = jnp.concatenate([hc, hx], 1).reshape(bsz * t, d).astype(BF16)
        wi = _permute_cols(w_in[l]).astype(BF16)
        z = matmul(h_all, wi, tm=_tile(bsz * t, 1100), tn=1024, name="w_in").reshape(bsz, t, N_IN_PAD)
        og2 = gla_mixer(z, gla_a2[l], gla_a_bias[l], n_ctx)
        or2 = ret_mixer(z, ret_decay_logit[l], n_ctx)
        ow2, g_rw, bonus = rwkv_mixer(z, rwkv_mu[l], rwkv_w0[l], rwkv_w2[l], rwkv_a0[l], rwkv_a2[l],
                                      rwkv_g2[l], rwkv_k_k[l], rwkv_k_a[l], rwkv_r_k[l], n_ctx)
        y = mixer_post(z, og2, or2, ow2, g_rw, bonus, gla_norm_w[l], rwkv_lnx_w[l], rwkv_lnx_b[l], out_from)
        t_out = t - out_from
        y = matmul(y.reshape(bsz * t_out, d), w_out, tm=_tile(bsz * t_out, 1100), tn=512,
                   name="w_out", layer=l).reshape(bsz, t_out, d)
        if ctx_out:
            ctx = ctx + mc[2] * _rmsnorm(y[:, :lc], post_mix[l])
            x = x + mx[2] * _rmsnorm(y[:, lc:], post_mix[l])
        else:
            x = x + mx[2] * _rmsnorm(y, post_mix[l])

        hx = _rmsnorm(x, pre_ffn[l]) * (1.0 + mx[4]) + mx[3]
        if ctx_out:
            hc = _rmsnorm(ctx, pre_ffn[l]) * (1.0 + mc[4]) + mc[3]
            h = jnp.concatenate([hc, hx], 1)
        else:
            h = hx
        f = moe_ffn(h.reshape(-1, d), l, router_w, router_bias, exp_gate, exp_up, exp_down,
                    shared_gate, shared_up, shared_down).reshape(h.shape)
        if ctx_out:
            ctx = ctx + mc[5] * _rmsnorm(f[:, :lc], post_ffn[l])
            x = x + mx[5] * _rmsnorm(f[:, lc:], post_ffn[l])
        else:
            x = x + mx[5] * _rmsnorm(f, post_ffn[l])
    return x
```

```python
import functools

import numpy as np
import jax
import jax.numpy as jnp
from jax import lax
from jax.experimental import pallas as pl
from jax.experimental.pallas import tpu as pltpu

F32 = jnp.float32
BF16 = jnp.bfloat16

CHUNK = 64
LEVELS = (32, 16, 8, 4, 2, 1)

GLA_HEADS, GLA_DK, GLA_DV, GLA_RANK, GLA_TAU = 6, 128, 256, 16, 16.0
RET_HEADS, RET_DK, RET_DV = 5, 128, 256
ROPE_BASE = 10000.0
RWKV_HEADS, RWKV_HEAD = 20, 64
RWKV_WIDTH = RWKV_HEADS * RWKV_HEAD
RWKV_DECAY_RANK, RWKV_ICLR_RANK, RWKV_GATE_RANK = 128, 128, 480
GN_EPS = 64e-5
RMS_EPS = 1e-6
N_EXPERTS, TOP_K, D_EXPERT, ROUTED_SCALE = 64, 8, 384, 2.5
GRID_W = 64

GLA_QK, GLA_W = GLA_HEADS * GLA_DK, GLA_HEADS * GLA_DV
RET_QK, RET_W = RET_HEADS * RET_DK, RET_HEADS * RET_DV
GLA_N = 2 * GLA_QK + 2 * GLA_W + GLA_RANK
RET_N = 2 * RET_QK + 2 * RET_W
RWKV_N = 3 * RWKV_WIDTH + RWKV_DECAY_RANK + RWKV_ICLR_RANK + RWKV_GATE_RANK

COL_GLA_Q, COL_GLA_K, COL_GLA_V, COL_GLA_G = 0, 768, 1536, 3072
COL_RW_G1 = 4608
COL_RET_Q, COL_RET_K, COL_RET_V, COL_RET_G = 5120, 5760, 6400, 7680
COL_RW_R, COL_RW_K, COL_RW_V = 8960, 10240, 11520
COL_RW_W1, COL_RW_A1, COL_GLA_LR = 12800, 12928, 13056
N_IN_PAD = 13312

VMEM_LIMIT = 56 * 1024 * 1024


def _cparams(sem):
    return pltpu.CompilerParams(dimension_semantics=sem, vmem_limit_bytes=VMEM_LIMIT)


def _mm_body(a_ref, b_ref, o_ref):
    o_ref[...] = jnp.dot(a_ref[...].astype(BF16), b_ref[...].astype(BF16),
                         preferred_element_type=F32).astype(o_ref.dtype)


def matmul(a, b, *, tm, tn, out_dtype=F32, name="mm", layer=None):
    m, k = a.shape
    n = b.shape[-1]
    assert m % tm == 0 and n % tn == 0, (a.shape, b.shape, tm, tn)
    if layer is None:
        b_spec = pl.BlockSpec((k, tn), lambda i, j: (0, j))
    else:
        b_spec = pl.BlockSpec((None, k, tn), lambda i, j: (layer, 0, j))
    return pl.pallas_call(
        _mm_body,
        out_shape=jax.ShapeDtypeStruct((m, n), out_dtype),
        grid=(m // tm, n // tn),
        in_specs=[pl.BlockSpec((tm, k), lambda i, j: (i, 0)), b_spec],
        out_specs=pl.BlockSpec((tm, tn), lambda i, j: (i, j)),
        compiler_params=_cparams(("parallel", "arbitrary")),
        name=name,
    )(a, b)


def _scan_consts():
    c = CHUNK
    cum = np.zeros((2, 2 * c + 8, c), np.float32)
    masks = np.zeros((2, len(LEVELS), c, c), np.float32)
    tri = np.tril(np.ones((c, c), np.float32))
    cum[0, :c] = tri
    cum[0, c:2 * c] = tri
    cum[1, :c] = tri.T
    cum[1, c:2 * c] = np.triu(np.ones((c, c), np.float32), 1)
    cum[:, 2 * c:] = 1.0
    for li, s in enumerate(LEVELS):
        for i in range(c):
            for j in range(c):
                if (i & s) and not (j & s) and i // (2 * s) == j // (2 * s):
                    masks[0, li, i, j] = 1.0
        masks[1, li] = masks[0, li].T
    return jnp.asarray(cum, BF16), jnp.asarray(masks)


def _split3(x):
    hi = x.astype(BF16)
    r1 = x - hi.astype(F32)
    mid = r1.astype(BF16)
    lo = (r1 - mid.astype(F32)).astype(BF16)
    return hi, mid, lo


def _cumsums(cum, la):
    d = la.shape[1]
    r = jnp.dot(cum, jnp.concatenate(_split3(la), axis=1), preferred_element_type=F32)
    return r[:, :d] + r[:, d:2 * d] + r[:, 2 * d:]


def _boundary(cc, s):
    c, d = cc.shape
    if s >= 8:
        parts = [jnp.broadcast_to(cc[r0 + s - 1:r0 + s], (2 * s, d)) for r0 in range(0, c, 2 * s)]
        return parts[0] if len(parts) == 1 else jnp.concatenate(parts, axis=0)
    cc3 = cc.reshape(c // 8, 8, d)
    if s == 4:
        return jnp.broadcast_to(cc3[:, 3:4], cc3.shape).reshape(c, d)
    sub = lax.broadcasted_iota(jnp.int32, cc3.shape, 1)
    lo = jnp.broadcast_to(cc3[:, 1:2], cc3.shape)
    hi = jnp.broadcast_to(cc3[:, 5:6], cc3.shape)
    return jnp.where(sub < 4, lo, hi).reshape(c, d)


def _nt(a, b):
    return lax.dot_general(a, b, (((1,), (1,)), ((), ())), preferred_element_type=F32)


def _tn(a, b):
    return lax.dot_general(a, b, (((0,), (0,)), ((), ())), preferred_element_type=F32)


def _gla_body(cum_ref, mask_ref, q_ref, k_ref, v_ref, la_ref, o_ref, st_ref, *, heads, dk, dv, scale):
    @pl.when(pl.program_id(2) == 0)
    def _():
        st_ref[...] = jnp.zeros_like(st_ref)

    cum = cum_ref[0]
    c = CHUNK
    hs = range(heads)
    qs_, ks_, vs_, las, bqs, ccs, tots = [], [], [], [], [], [], []
    for h in hs:
        la = la_ref[0, 0, :, h * dk:(h + 1) * dk]
        rs = _cumsums(cum, la)
        qs_.append(q_ref[0, :, h * dk:(h + 1) * dk] * scale)
        ks_.append(k_ref[0, :, h * dk:(h + 1) * dk])
        vs_.append(v_ref[0, :, h * dv:(h + 1) * dv])
        las.append(la)
        bqs.append(rs[:c])
        ccs.append(rs[c:2 * c])
        tots.append(rs[2 * c:2 * c + 1])
    sc = [None] * heads
    for li, s in enumerate(LEVELS):
        for h in hs:
            q, k = qs_[h], ks_[h]
            if s == 1:
                qd, kd = q * jnp.exp(las[h]), k
            else:
                g = bqs[h] - _boundary(ccs[h], s)
                qd = q * jnp.exp(jnp.minimum(g, 0.0))
                kd = k * jnp.exp(jnp.minimum(-g, 0.0))
            p = _nt(qd.astype(BF16), kd.astype(BF16)) * mask_ref[0, li]
            sc[h] = p if sc[h] is None else sc[h] + p
    for h in hs:
        q, k, v, bq, tot = qs_[h], ks_[h], vs_[h], bqs[h], tots[h]
        vb = v.astype(BF16)
        dg = jnp.sum(q * k, axis=1, keepdims=True)
        st = st_ref[h]
        o = (jnp.dot(sc[h].astype(BF16), vb, preferred_element_type=F32) + dg * v
             + _nt((q * jnp.exp(bq)).astype(BF16), st.astype(BF16)))
        o_ref[0, 0, :, h * dv:(h + 1) * dv] = o
        kbar = k * jnp.exp(tot - bq)
        st_ref[h] = st * jnp.exp(tot) + _tn(vb, kbar.astype(BF16))


def _ret_body(lam_ref, q_ref, k_ref, v_ref, o_ref, st_ref, *, heads, dk, dv):
    d = pl.program_id(0)

    @pl.when(pl.program_id(2) == 0)
    def _():
        st_ref[...] = jnp.zeros_like(st_ref)

    c = CHUNK
    ri = lax.broadcasted_iota(jnp.int32, (c, c), 0)
    ci = lax.broadcasted_iota(jnp.int32, (c, c), 1)
    dist = jnp.where(d == 0, ri - ci, ci - ri)
    live = dist >= 0
    distf = jnp.maximum(dist, 0).astype(F32)
    row = lax.broadcasted_iota(jnp.int32, (c, 1), 0)
    pos = jnp.where(d == 0, row + 1, c - row).astype(F32)
    for h in range(heads):
        lam = lam_ref[d * heads + h]
        q = q_ref[0, :, h * dk:(h + 1) * dk]
        k = k_ref[0, :, h * dk:(h + 1) * dk]
        v = v_ref[0, :, h * dv:(h + 1) * dv]
        vb = v.astype(BF16)
        decay = jnp.where(live, jnp.exp(lam * distf), 0.0)
        sc = _nt(q.astype(BF16), k.astype(BF16)) * decay
        st = st_ref[h]
        o = (jnp.dot(sc.astype(BF16), vb, preferred_element_type=F32)
             + _nt((q * jnp.exp(lam * pos)).astype(BF16), st.astype(BF16)))
        o_ref[0, 0, :, h * dv:(h + 1) * dv] = o
        kbar = k * jnp.exp(lam * (c - pos))
        st_ref[h] = st * jnp.exp(lam * c) + _tn(vb, kbar.astype(BF16))


def _chunk_index(d, c, n_ctx, n_all):
    back = jnp.where(c < n_ctx, n_ctx - 1 - c, n_ctx + n_all - 1 - c)
    return jnp.where(d == 0, c, back)


def _col_spec(width, col, ch):
    assert col % width == 0
    return pl.BlockSpec((1, CHUNK, width), lambda d, b, c, *_: (b, ch(d, c), col // width))


def gla_scan(z, la2, n_ctx):
    bsz, t, _ = z.shape
    heads, dk, dv = GLA_HEADS, GLA_DK, GLA_DV
    n_all = t // CHUNK
    cum, masks = _scan_consts()
    ch = functools.partial(_chunk_index, n_ctx=n_ctx, n_all=n_all)
    body = functools.partial(_gla_body, heads=heads, dk=dk, dv=dv, scale=dk ** -0.5)
    return pl.pallas_call(
        body,
        out_shape=jax.ShapeDtypeStruct((2, bsz, t, heads * dv), F32),
        grid=(2, bsz, n_all),
        in_specs=[
            pl.BlockSpec((1,) + cum.shape[1:], lambda d, b, c: (d, 0, 0)),
            pl.BlockSpec((1,) + masks.shape[1:], lambda d, b, c: (d, 0, 0, 0)),
            _col_spec(heads * dk, COL_GLA_Q, ch),
            _col_spec(heads * dk, COL_GLA_K, ch),
            _col_spec(heads * dv, COL_GLA_V, ch),
            pl.BlockSpec((1, 1, CHUNK, heads * dk), lambda d, b, c: (d, b, ch(d, c), 0)),
        ],
        out_specs=pl.BlockSpec((1, 1, CHUNK, heads * dv), lambda d, b, c: (d, b, ch(d, c), 0)),
        scratch_shapes=[pltpu.VMEM((heads, dv, dk), F32)],
        compiler_params=_cparams(("parallel", "parallel", "arbitrary")),
        name="gla_scan",
    )(cum, masks, z, z, z, la2)


def ret_scan(q, k, z, lam, n_ctx):
    bsz, t, _ = q.shape
    heads, dk, dv = RET_HEADS, RET_DK, RET_DV
    n_all = t // CHUNK
    ch = functools.partial(_chunk_index, n_ctx=n_ctx, n_all=n_all)
    body = functools.partial(_ret_body, heads=heads, dk=dk, dv=dv)
    grid_spec = pltpu.PrefetchScalarGridSpec(
        num_scalar_prefetch=1,
        grid=(2, bsz, n_all),
        in_specs=[
            pl.BlockSpec((1, CHUNK, heads * dk), lambda d, b, c, lam: (b, ch(d, c), 0)),
            pl.BlockSpec((1, CHUNK, heads * dk), lambda d, b, c, lam: (b, ch(d, c), 0)),
            _col_spec(heads * dv, COL_RET_V, ch),
        ],
        out_specs=pl.BlockSpec((1, 1, CHUNK, heads * dv), lambda d, b, c, lam: (d, b, ch(d, c), 0)),
        scratch_shapes=[pltpu.VMEM((heads, dv, dk), F32)],
    )
    return pl.pallas_call(
        body,
        out_shape=jax.ShapeDtypeStruct((2, bsz, t, heads * dv), F32),
        grid_spec=grid_spec,
        compiler_params=_cparams(("parallel", "parallel", "arbitrary")),
        name="ret_scan",
    )(lam.reshape(-1).astype(F32), q, k, z)


RWKV_GROUP = 4
RWKV_GW = RWKV_GROUP * RWKV_HEAD


def _rwkv_consts():
    c = CHUNK
    gw = RWKV_GW
    cum = np.zeros((2, 2 * c + 16, c), np.float32)
    tri = np.tril(np.ones((c, c), np.float32))
    mid = c // 2
    cum[0, :c] = tri - tri[mid - 1][None, :]
    cum[0, c:2 * c] = 1.0 - tri
    cum[0, 2 * c + 8:] = tri[mid - 1][None, :]
    cum[1, :c] = tri.T - tri.T[mid][None, :]
    cum[1, c:2 * c] = 1.0 - tri.T
    cum[1, 2 * c + 8:] = tri.T[mid][None, :]
    cum[:, 2 * c:2 * c + 8] = 1.0
    t = np.arange(c)[:, None]
    s = np.tile(np.arange(c), RWKV_GROUP)[None, :]
    masks = np.zeros((2, 3, c, gw), np.float32)
    masks[0, 0], masks[0, 1] = s < t, s <= t
    masks[1, 0], masks[1, 1] = s > t, s >= t
    masks[:, 2] = s == t
    return jnp.asarray(cum, BF16), jnp.asarray(masks)


def _rwkv_body(cum_ref, mask_ref, r_ref, k_ref, v_ref, kk_ref, kka_ref, lw_ref, o_ref, st_ref, *, groups):
    @pl.when(pl.program_id(2) == 0)
    def _():
        st_ref[...] = jnp.zeros_like(st_ref)

    c, gw = CHUNK, RWKV_GW
    cum = cum_ref[0]
    strict, incl, eye = mask_ref[0, 0], mask_ref[0, 1], mask_ref[0, 2]
    rb = lax.broadcasted_iota(jnp.int32, (gw, gw), 0) // RWKV_HEAD
    cb = lax.broadcasted_iota(jnp.int32, (gw, gw), 1) // RWKV_HEAD
    same_head = rb == cb

    def bd(x):
        xb = x.astype(BF16)
        return jnp.where(same_head, jnp.concatenate([xb] * RWKV_GROUP, axis=0), jnp.zeros((), BF16))

    def mm(a, b):
        return jnp.dot(a.astype(BF16), b, preferred_element_type=F32)

    gs = range(groups)
    sls = [slice(g * gw, (g + 1) * gw) for g in gs]
    pre = []
    for sl in sls:
        r, k, v = r_ref[0, :, sl], k_ref[0, :, sl], v_ref[0, :, sl]
        kk, kka, lw = kk_ref[0, :, sl], kka_ref[0, :, sl], lw_ref[0, 0, :, sl]
        rs = _cumsums(cum, lw)
        gm, ge, tot, gmid = rs[:c], rs[c:2 * c], rs[2 * c:2 * c + 1], rs[2 * c + 8:2 * c + 9]
        e_in, e_out, e_end = jnp.exp(gm), jnp.exp(-gm), jnp.exp(ge)
        kt_rel = kk * (e_in * jnp.exp(-lw))
        lhs = jnp.concatenate([kt_rel, r * e_in], axis=0)
        pre.append(dict(lhs=lhs.astype(BF16), lhs_abs=(lhs * jnp.exp(gmid)).astype(BF16),
                        kka_o=kka * e_out, k_o=k * e_out, v=v, tot=tot,
                        kv_end=jnp.concatenate([k * e_end, kka * e_end], axis=0).astype(BF16)))
    s_b = [_nt(q["lhs"], bd(q["kka_o"])) for q in pre]
    s_k = [_nt(q["lhs"], bd(q["k_o"])) for q in pre]
    ns = [-(s[:c] * strict) for s in s_b]
    xs = [eye + n for n in ns]
    ps = [mm(n, bd(n)) for n in ns]
    for _ in range(4):
        pps = [mm(jnp.concatenate([p, x], axis=0), bd(p)) for p, x in zip(ps, xs)]
        ps = [pp[:c] for pp in pps]
        xs = [x + pp[c:] for x, pp in zip(xs, pps)]
    xs = [x + mm(x, bd(p)) for p, x in zip(ps, xs)]
    sts = [st_ref[g] for g in gs]
    hs = [_nt(q["lhs_abs"], st.astype(BF16)) for q, st in zip(pre, sts)]
    vs = [mm(jnp.concatenate([sk[:c] * strict, sk[c:] * incl], axis=0), bd(q["v"])) for sk, q in zip(s_k, pre)]
    us = [mm(x, bd(h[:c] + w[:c])) for x, h, w in zip(xs, hs, vs)]
    for g in gs:
        o_ref[0, 0, :, sls[g]] = hs[g][c:] + vs[g][c:] - mm(s_b[g][c:] * incl, bd(us[g]))
    for g in gs:
        upd = _tn(jnp.concatenate([pre[g]["v"], -us[g]], axis=0).astype(BF16), pre[g]["kv_end"])
        st_ref[g] = sts[g] * jnp.exp(pre[g]["tot"]) + jnp.where(same_head, upd, 0.0)


def rwkv_scan(r, k, v, kk, kka, lw2, n_ctx):
    bsz, t, ch_w = r.shape
    groups = ch_w // RWKV_GW
    n_all = t // CHUNK
    cum, masks = _rwkv_consts()
    ch = functools.partial(_chunk_index, n_ctx=n_ctx, n_all=n_all)
    tok = pl.BlockSpec((1, CHUNK, ch_w), lambda d, b, c: (b, ch(d, c), 0))
    dir_tok = pl.BlockSpec((1, 1, CHUNK, ch_w), lambda d, b, c: (d, b, ch(d, c), 0))
    return pl.pallas_call(
        functools.partial(_rwkv_body, groups=groups),
        out_shape=jax.ShapeDtypeStruct((2, bsz, t, ch_w), F32),
        grid=(2, bsz, n_all),
        in_specs=[
            pl.BlockSpec((1,) + cum.shape[1:], lambda d, b, c: (d, 0, 0)),
            pl.BlockSpec((1,) + masks.shape[1:], lambda d, b, c: (d, 0, 0, 0)),
            tok, tok, tok, tok, tok, dir_tok,
        ],
        out_specs=dir_tok,
        scratch_shapes=[pltpu.VMEM((groups, RWKV_GW, RWKV_GW), F32)],
        compiler_params=_cparams(("parallel", "parallel", "arbitrary")),
        name="rwkv_scan",
    )(cum, masks, r, k, v, kk, kka, lw2)


def _split2(x):
    hi = x.astype(BF16)
    return hi, (x - hi.astype(F32)).astype(BF16)


ROUTER_LANES = 128


def _router_body(h_ref, w_ref, bias_ref, idx_ref, gate_ref, *, n_experts, top_k):
    a_hi, a_mid = _split2(h_ref[...])
    b_hi, b_mid = _split2(w_ref[...])
    dot = functools.partial(jnp.dot, preferred_element_type=F32)
    scores = jax.nn.sigmoid(dot(a_hi, b_hi) + (dot(a_hi, b_mid) + dot(a_mid, b_hi)))
    lane = lax.broadcasted_iota(jnp.int32, scores.shape, 1)
    cand = jnp.where(lane < n_experts, scores + bias_ref[...], -jnp.inf)
    idx_out = jnp.zeros(scores.shape, jnp.int32)
    val_out = jnp.zeros(scores.shape, F32)
    total = jnp.zeros((scores.shape[0], 1), F32)
    for j in range(top_k):
        best = jnp.max(cand, axis=1, keepdims=True)
        arg = jnp.min(jnp.where(cand == best, lane, ROUTER_LANES), axis=1, keepdims=True)
        pick = lane == arg
        val = jnp.sum(jnp.where(pick, scores, 0.0), axis=1, keepdims=True)
        cand = jnp.where(pick, -jnp.inf, cand)
        idx_out = jnp.where(lane == j, arg, idx_out)
        val_out = jnp.where(lane == j, val, val_out)
        total = total + val
    idx_ref[...] = idx_out
    gate_ref[...] = ROUTED_SCALE * val_out / total


def router(h, router_w, router_bias, layer):
    t, d = h.shape
    e = router_w.shape[-1]
    tm = _tile(t, 600)
    w = jnp.pad(router_w[layer].astype(F32), ((0, 0), (0, ROUTER_LANES - e)))
    bias = jnp.pad(router_bias[layer].astype(F32), (0, ROUTER_LANES - e))[None]
    idx, gates = pl.pallas_call(
        functools.partial(_router_body, n_experts=e, top_k=TOP_K),
        out_shape=(jax.ShapeDtypeStruct((t, ROUTER_LANES), jnp.int32), jax.ShapeDtypeStruct((t, ROUTER_LANES), F32)),
        grid=(t // tm,),
        in_specs=[pl.BlockSpec((tm, d), lambda i: (i, 0)), pl.BlockSpec((d, ROUTER_LANES), lambda i: (0, 0)),
                  pl.BlockSpec((1, ROUTER_LANES), lambda i: (0, 0))],
        out_specs=(pl.BlockSpec((tm, ROUTER_LANES), lambda i: (i, 0)), pl.BlockSpec((tm, ROUTER_LANES), lambda i: (i, 0))),
        compiler_params=_cparams(("parallel",)),
        name="router",
    )(h, w, bias)
    return idx[:, :TOP_K], gates[:, :TOP_K]


MOE_BM = 256


def _tile(m, cap, mult=16):
    return max(t for t in range(mult, cap + 1, mult) if m % t == 0)


def _pack_pairs(x):
    n = x.shape[-1] // 2
    bits = lax.bitcast_convert_type(x, jnp.uint32)
    bits = (bits + jnp.uint32(0x7FFF) + ((bits >> 16) & jnp.uint32(1))) & jnp.uint32(0xFFFF0000)
    return bits[..., :n] | (bits[..., n:] >> 16)


def _unpack_pairs(w):
    hi = lax.bitcast_convert_type(w & jnp.uint32(0xFFFF0000), F32)
    lo = lax.bitcast_convert_type(w << 16, F32)
    return jnp.concatenate([hi, lo], axis=-1)


def _swiglu(x, wg, wu, wd, gate=None):
    h1 = jnp.dot(x, wg, preferred_element_type=F32)
    h2 = jnp.dot(x, wu, preferred_element_type=F32)
    a = h1 * jax.nn.sigmoid(h1) * h2
    if gate is not None:
        a = a * gate
    return jnp.dot(a.astype(BF16), wd, preferred_element_type=F32)


def _cast_rows(src, dst, rows):
    def body(j, carry):
        r = pl.multiple_of(j * rows, rows)
        dst[pl.ds(r, rows), :] = src[pl.ds(r, rows), :].astype(BF16)
        return carry
    lax.fori_loop(0, src.shape[0] // rows, body, 0)


def _experts_body(be_ref, nx_ref, nv_ref, tok_ref, h_hbm, gate_ref, wg_hbm, wu_hbm, wd_hbm, o_ref,
                  xbuf0, xbuf1, st_g, st_u, st_d, wg_b, wu_b, wd_b, xsem, wsem, *, layer, bm):
    i = pl.program_id(0)
    n_live = nv_ref[0]
    slot = lax.rem(i, 2)

    def row_copy(base, r, buf, s):
        return pltpu.make_async_copy(h_hbm.at[pl.ds(tok_ref[base + r], 1)], buf.at[pl.ds(r, 1)], xsem.at[s])

    def rows_wait(buf, s):
        pltpu.make_async_copy(h_hbm.at[pl.ds(0, bm)], buf, xsem.at[s]).wait()

    def weight_copies(e):
        return (pltpu.make_async_copy(wg_hbm.at[layer, e], st_g, wsem.at[0]),
                pltpu.make_async_copy(wu_hbm.at[layer, e], st_u, wsem.at[1]),
                pltpu.make_async_copy(wd_hbm.at[layer, e], st_d, wsem.at[2]))

    @pl.when(i == 0)
    def _():
        def body(r, carry):
            row_copy(0, r, xbuf0, 0).start()
            return carry
        lax.fori_loop(0, bm, body, 0, unroll=8)
        for cp in weight_copies(be_ref[0]):
            cp.start()

    @pl.when((i == 0) | (be_ref[i] != be_ref[jnp.maximum(i - 1, 0)]))
    def _():
        for cp in weight_copies(be_ref[i]):
            cp.wait()
        _cast_rows(st_g, wg_b, 256)
        _cast_rows(st_u, wu_b, 256)
        _cast_rows(st_d, wd_b, 32)

        @pl.when(nx_ref[i] >= 0)
        def _():
            for cp in weight_copies(nx_ref[i]):
                cp.start()

    dot = functools.partial(jnp.dot, preferred_element_type=F32)
    half = xbuf0.shape[1]
    kc = 256
    n_k = half // kc
    per = bm // n_k

    def step(cur, cur_s, nxt, nxt_s):
        rows_wait(cur, cur_s)
        base = jnp.minimum(i + 1, n_live - 1) * bm
        h1 = jnp.zeros((bm, wg_b.shape[1]), F32)
        h2 = jnp.zeros((bm, wg_b.shape[1]), F32)
        for s in range(n_k):
            for r in range(s * per, (s + 1) * per):
                row_copy(base, r, nxt, nxt_s).start()
            w = cur[:, s * kc:(s + 1) * kc]
            x_a = lax.bitcast_convert_type(w & jnp.uint32(0xFFFF0000), F32).astype(BF16)
            x_b = lax.bitcast_convert_type(w << 16, F32).astype(BF16)
            rows_a, rows_b = slice(s * kc, (s + 1) * kc), slice(half + s * kc, half + (s + 1) * kc)
            h1 = h1 + (dot(x_a, wg_b[rows_a, :]) + dot(x_b, wg_b[rows_b, :]))
            h2 = h2 + (dot(x_a, wu_b[rows_a, :]) + dot(x_b, wu_b[rows_b, :]))
        a = (h1 * jax.nn.sigmoid(h1) * h2 * gate_ref[...]).astype(BF16)
        for s in range(n_k):
            y = jnp.concatenate([dot(a, wd_b[:, s * kc:(s + 1) * kc]),
                                 dot(a, wd_b[:, half + s * kc:half + (s + 1) * kc])], axis=1)
            o_ref[:, s * kc:(s + 1) * kc] = _pack_pairs(y)

        @pl.when(i == n_live - 1)
        def _():
            rows_wait(nxt, nxt_s)

    @pl.when((i < n_live) & (slot == 0))
    def _():
        step(xbuf0, 0, xbuf1, 1)

    @pl.when((i < n_live) & (slot == 1))
    def _():
        step(xbuf1, 1, xbuf0, 0)

    @pl.when(i >= n_live)
    def _():
        o_ref[...] = jnp.zeros_like(o_ref)


def experts(h_pad, gate_rows, exp_gate, exp_up, exp_down, layer, block_expert, next_expert, n_live, buf_tok):
    bm = MOE_BM
    n = buf_tok.shape[0]
    d = exp_gate.shape[-2]
    f = exp_gate.shape[-1]
    dp = h_pad.shape[1]
    any_spec = pl.BlockSpec(memory_space=pl.ANY)
    grid_spec = pltpu.PrefetchScalarGridSpec(
        num_scalar_prefetch=4,
        grid=(n // bm,),
        in_specs=[any_spec, pl.BlockSpec((bm, 1), lambda i, *_: (i, 0)), any_spec, any_spec, any_spec],
        out_specs=pl.BlockSpec((bm, dp), lambda i, *_: (i, 0)),
        scratch_shapes=[
            pltpu.VMEM((bm, dp), jnp.uint32), pltpu.VMEM((bm, dp), jnp.uint32),
            pltpu.VMEM((d, f), F32), pltpu.VMEM((d, f), F32), pltpu.VMEM((f, d), F32),
            pltpu.VMEM((d, f), BF16), pltpu.VMEM((d, f), BF16), pltpu.VMEM((f, d), BF16),
            pltpu.SemaphoreType.DMA((2,)), pltpu.SemaphoreType.DMA((3,)),
        ],
    )
    return pl.pallas_call(
        functools.partial(_experts_body, layer=layer, bm=bm),
        out_shape=jax.ShapeDtypeStruct((n, dp), jnp.uint32),
        grid_spec=grid_spec,
        compiler_params=_cparams(("arbitrary",)),
        name="experts",
    )(block_expert, next_expert, n_live, buf_tok, h_pad, gate_rows, exp_gate, exp_up, exp_down)


def _shared_body(x_ref, wg_ref, wu_ref, wd_ref, o_ref):
    o_ref[...] = _swiglu(x_ref[...].astype(BF16), wg_ref[...].astype(BF16), wu_ref[...].astype(BF16),
                         wd_ref[...].astype(BF16))


def shared_expert(h, sh_gate, sh_up, sh_down, layer):
    t, d = h.shape
    f = sh_gate.shape[-1]
    tm = _tile(t, 300)
    once = pl.Buffered(1)
    return pl.pallas_call(
        _shared_body,
        out_shape=jax.ShapeDtypeStruct((t, d), F32),
        grid=(t // tm,),
        in_specs=[pl.BlockSpec((tm, d), lambda i: (i, 0)),
                  pl.BlockSpec((None, d, f), lambda i: (layer, 0, 0), pipeline_mode=once),
                  pl.BlockSpec((None, d, f), lambda i: (layer, 0, 0), pipeline_mode=once),
                  pl.BlockSpec((None, f, d), lambda i: (layer, 0, 0), pipeline_mode=once)],
        out_specs=pl.BlockSpec((tm, d), lambda i: (i, 0)),
        compiler_params=_cparams(("arbitrary",)),
        name="shared",
    )(h, sh_gate, sh_up, sh_down)


COMBINE_TB = 64


def _combine_body(pos_ref, y_hbm, sh_ref, o_ref, buf, sem, *, tb, k):
    i = pl.program_id(0)
    n = pl.num_programs(0)
    slot = lax.rem(i, 2)

    def gather(blk, s):
        base = blk * tb * k

        def body(t, carry):
            for kk in range(k):
                p = pos_ref[base + t * k + kk]
                pltpu.make_async_copy(y_hbm.at[pl.ds(p, 1)], buf.at[s, pl.ds(kk * tb + t, 1)], sem.at[s]).start()
            return carry
        lax.fori_loop(0, tb, body, 0, unroll=4)

    @pl.when(i == 0)
    def _():
        gather(0, 0)

    @pl.when(i + 1 < n)
    def _():
        gather(i + 1, 1 - slot)

    pltpu.make_async_copy(y_hbm.at[pl.ds(0, tb * k)], buf.at[slot], sem.at[slot]).wait()
    acc = sh_ref[...]
    for kk in range(k):
        acc = acc + _unpack_pairs(buf[slot, kk * tb:(kk + 1) * tb, :])
    o_ref[...] = acc


def combine(y, pos, shared):
    t, k = pos.shape
    d = shared.shape[1]
    dp = y.shape[1]
    tb = COMBINE_TB
    grid_spec = pltpu.PrefetchScalarGridSpec(
        num_scalar_prefetch=1,
        grid=(t // tb,),
        in_specs=[pl.BlockSpec(memory_space=pl.ANY), pl.BlockSpec((tb, d), lambda i, *_: (i, 0))],
        out_specs=pl.BlockSpec((tb, d), lambda i, *_: (i, 0)),
        scratch_shapes=[pltpu.VMEM((2, tb * k, dp), jnp.uint32), pltpu.SemaphoreType.DMA((2,))],
    )
    return pl.pallas_call(
        functools.partial(_combine_body, tb=tb, k=k),
        out_shape=jax.ShapeDtypeStruct((t, d), F32),
        grid_spec=grid_spec,
        compiler_params=_cparams(("arbitrary",)),
        name="combine",
    )(pos.reshape(-1), y, shared)


def moe_ffn(h, hp, layer, router_w, router_bias, exp_gate, exp_up, exp_down, sh_gate, sh_up, sh_down):
    t, d = h.shape
    e, k, bm = N_EXPERTS, TOP_K, MOE_BM
    i32 = jnp.int32
    idx, gates = router(h, router_w, router_bias, layer)
    flat_e = idx.reshape(-1)
    iota = jnp.arange(t * k, dtype=i32)
    e_sorted, order = lax.sort_key_val(flat_e, iota)
    _, rank = lax.sort_key_val(order, iota)
    bounds = jnp.searchsorted(e_sorted, jnp.arange(e + 1, dtype=i32), side='left',
                              method='compare_all').astype(i32)
    first, counts = bounds[:-1], bounds[1:] - bounds[:-1]
    padded = (counts + bm - 1) // bm * bm
    ends = jnp.cumsum(padded)
    starts = ends - padded
    pos = (iota + (starts - first)[e_sorted])[rank].reshape(t, k)
    n_blocks = -(-(t * k) // bm) + e
    blk_start = jnp.arange(n_blocks, dtype=i32) * bm
    block_valid = (blk_start < ends[-1]).astype(i32)
    last_e = jnp.max(jnp.where(counts > 0, jnp.arange(e, dtype=i32), 0))
    block_expert = jnp.where(block_valid > 0,
                             jnp.minimum(jnp.searchsorted(ends, blk_start, side='right',
                                                          method='compare_all').astype(i32), e - 1), last_e)
    present = jnp.where(counts > 0, jnp.arange(e, dtype=i32), e)
    nxt = lax.cummin(jnp.concatenate([present[1:], jnp.full((1,), e, i32)]), reverse=True)
    next_expert = jnp.where(nxt < e, nxt, -1)[block_expert]
    blk_off = blk_start - starts[block_expert]
    blk_live = jnp.where(block_valid > 0, jnp.clip(counts[block_expert] - blk_off, 0, bm), 0)
    lane = jnp.arange(bm, dtype=i32)[None, :]
    live = (lane < blk_live[:, None]).reshape(-1)
    src = order[jnp.clip((first[block_expert] + blk_off)[:, None] + lane, 0, t * k - 1).reshape(-1)]
    buf_tok = jnp.where(live, src // k, t).astype(i32)
    gate_rows = jnp.where(live, gates.reshape(-1)[src], 0.0)[:, None]

    h_pad = jnp.concatenate([hp, jnp.zeros((1, d // 2), jnp.uint32)], 0)
    n_live = (ends[-1] // bm).astype(i32).reshape(1)
    y = experts(h_pad, gate_rows, exp_gate, exp_up, exp_down, layer, block_expert, next_expert, n_live, buf_tok)
    return combine(y, pos, shared_expert(h, sh_gate, sh_up, sh_down, layer))


def _permute_cols(w):
    gla, ret, rw = w[..., :GLA_N], w[..., GLA_N:GLA_N + RET_N], w[..., GLA_N + RET_N:]
    rkv = 3 * RWKV_WIDTH
    lora = rkv + RWKV_DECAY_RANK + RWKV_ICLR_RANK

    def zeros(n):
        return jnp.zeros(w.shape[:-1] + (n,), w.dtype)

    parts = [gla[..., :GLA_N - GLA_RANK],
             rw[..., lora:], zeros(COL_RET_Q - COL_RW_G1 - RWKV_GATE_RANK),
             ret,
             rw[..., :lora],
             gla[..., GLA_N - GLA_RANK:], zeros(N_IN_PAD - COL_GLA_LR - GLA_RANK)]
    return jnp.concatenate(parts, -1)


RESNORM_TM = 128


def _resnorm_body(*refs, has_res, has_pre, outs):
    def rms(u):
        return u * lax.rsqrt(jnp.mean(u * u, axis=1, keepdims=True) + RMS_EPS)

    it = iter(refs)
    x = next(it)[0]
    if has_res:
        f, gate, post_w = next(it)[0], next(it)[0], next(it)[...]
        x = x + gate * (rms(f) * post_w)
    if has_pre:
        shift, scale, pre_w = next(it)[0], next(it)[0], next(it)[...]
        h = rms(x) * pre_w * (1.0 + scale) + shift
    for name in outs:
        out = next(it)
        if name == "x":
            out[0] = x
        elif name == "h32":
            out[0] = h
        elif name == "hb":
            out[0] = h.astype(BF16)
        else:
            out[0] = _pack_pairs(h)


def resnorm(xs, ctx_tiles, outs, res=None, pre=None, x_off=0):
    bsz, tx, d = xs.shape
    tm = RESNORM_TM
    t_out = tx - x_off * tm

    def mod_spec(col):
        return pl.BlockSpec((1, 1, d), lambda b, j: (jnp.where(j + x_off < ctx_tiles, bsz, b), 0, col))

    w_spec = pl.BlockSpec((1, d), lambda b, j: (0, 0))
    tok = pl.BlockSpec((1, tm, d), lambda b, j: (b, j, 0))
    args, in_specs = [xs], [pl.BlockSpec((1, tm, d), lambda b, j: (b, j + x_off, 0))]
    if res is not None:
        f, mod, col, w = res
        assert f.shape == (bsz, t_out, d)
        args += [f, mod, w[None]]
        in_specs += [tok, mod_spec(col), w_spec]
    if pre is not None:
        mod, shift_col, scale_col, w = pre
        args += [mod, mod, w[None]]
        in_specs += [mod_spec(shift_col), mod_spec(scale_col), w_spec]
    kinds = {"x": (d, F32), "h32": (d, F32), "hb": (d, BF16), "hp": (d // 2, jnp.uint32)}
    out_shape = tuple(jax.ShapeDtypeStruct((bsz, t_out, kinds[o][0]), kinds[o][1]) for o in outs)
    out_specs = tuple(pl.BlockSpec((1, tm, kinds[o][0]), lambda b, j: (b, j, 0)) for o in outs)
    return pl.pallas_call(
        functools.partial(_resnorm_body, has_res=res is not None, has_pre=pre is not None, outs=outs),
        out_shape=out_shape,
        grid=(bsz, t_out // tm),
        in_specs=in_specs,
        out_specs=out_specs,
        compiler_params=_cparams(("parallel", "arbitrary")),
        name="resnorm",
    )(*args)


def _rope(x, pos):
    d = x.shape[-1]
    inv = ROPE_BASE ** (-jnp.linspace(0.0, 1.0, d // 2, dtype=F32))
    ang = pos.astype(F32)[:, None] * inv[None, :]
    cos, sin = jnp.cos(ang)[:, None, :], jnp.sin(ang)[:, None, :]
    x1, x2 = x[..., :d // 2], x[..., d // 2:]
    return jnp.concatenate([x1 * cos - x2 * sin, x1 * sin + x2 * cos], -1)


def _small_mm(a, b, name):
    bsz, t, k = a.shape
    m = bsz * t
    return matmul(a.reshape(m, k), b, tm=_tile(m, 1100), tn=b.shape[1], name=name).reshape(bsz, t, -1)


def gla_mixer(z, a2, a_bias, n_ctx):
    lr_pad = z[..., COL_GLA_LR:COL_GLA_LR + 128]
    la2 = jnp.stack([_small_mm(lr_pad, jnp.pad(a2[x], ((0, 128 - GLA_RANK), (0, 0))), "gla_gate")
                     for x in range(2)])
    la2 = jax.nn.log_sigmoid(la2 + a_bias[:, None, None, :]) / GLA_TAU
    return gla_scan(z, la2, n_ctx)


def ret_mixer(z, decay_logit, n_ctx):
    bsz, t, _ = z.shape
    q, k = z[..., COL_RET_Q:COL_RET_Q + RET_QK], z[..., COL_RET_K:COL_RET_K + RET_QK]
    pos = jnp.arange(t)
    q = _rope(q.reshape(bsz, t, RET_HEADS, RET_DK), pos).reshape(bsz, t, RET_QK)
    k = (_rope(k.reshape(bsz, t, RET_HEADS, RET_DK), pos) * RET_DK ** -0.5).reshape(bsz, t, RET_QK)
    lam = jax.nn.log_sigmoid(decay_logit.astype(F32))
    return ret_scan(q, k, z, lam, n_ctx)


POST_TM = 128


def _mixer_post_body(og_ref, or_ref, ow_ref, gg_ref, gr_ref, gw_ref, bonus_ref, nw_ref, lw_ref, lb_ref,
                     seg_ref, segt_ref, y_ref):
    def silu(u):
        return u * jax.nn.sigmoid(u)

    def head_rms(x):
        return x * lax.rsqrt(jnp.mean(x * x, axis=1, keepdims=True) + RMS_EPS)

    og = og_ref[0, 0] + og_ref[1, 0]
    gg = gg_ref[0]
    for h in range(GLA_HEADS):
        sl = slice(h * GLA_DV, (h + 1) * GLA_DV)
        y_ref[0, :, sl] = (head_rms(og[:, sl]) * nw_ref[...] * silu(gg[:, sl])).astype(y_ref.dtype)
    orr = or_ref[0, 0] + or_ref[1, 0]
    gr = gr_ref[0]
    for h in range(RET_HEADS):
        sl = slice(h * RET_DV, (h + 1) * RET_DV)
        y_ref[0, :, GLA_W + h * RET_DV:GLA_W + (h + 1) * RET_DV] = (
            head_rms(orr[:, sl]) * silu(gr[:, sl])).astype(y_ref.dtype)
    dot = functools.partial(jnp.dot, preferred_element_type=F32)

    def head_mean(x):
        hi, mid = _split2(x)
        s = (dot(hi, seg_ref[...]) + dot(mid, seg_ref[...])) * (1.0 / RWKV_HEAD)
        hi, mid = _split2(s)
        return dot(hi, segt_ref[...]) + dot(mid, segt_ref[...])

    ow = ow_ref[0, 0] + ow_ref[1, 0]
    xc = ow - head_mean(ow)
    gn = xc * lax.rsqrt(head_mean(xc * xc) + GN_EPS) * lw_ref[...] + lb_ref[...]
    y_ref[0, :, GLA_W + RET_W:] = ((gn + bonus_ref[0]) * gw_ref[0]).astype(y_ref.dtype)


def mixer_post(z, og2, or2, ow2, g_rw, bonus, norm_w, lnx_w, lnx_b, out_from):
    bsz, t, _ = z.shape
    tm = POST_TM
    off = out_from // tm
    assert out_from % tm == 0 and t % tm == 0
    cw = RWKV_WIDTH
    seg = np.zeros((cw, 128), np.float32)
    seg[np.arange(cw), np.arange(cw) // RWKV_HEAD] = 1.0
    params = [norm_w[None], lnx_w[None], lnx_b[None], jnp.asarray(seg, BF16), jnp.asarray(seg.T, BF16)]

    def two(width):
        return pl.BlockSpec((2, 1, tm, width), lambda b, j: (0, b, j + off, 0))

    def tok(width, blk=0):
        return pl.BlockSpec((1, tm, width), lambda b, j: (b, j + off, blk))

    def whole(x):
        return pl.BlockSpec(x.shape, lambda b, j, nd=x.ndim: (0,) * nd)

    d = GLA_W + RET_W + cw
    return pl.pallas_call(
        _mixer_post_body,
        out_shape=jax.ShapeDtypeStruct((bsz, t - out_from, d), BF16),
        grid=(bsz, (t - out_from) // tm),
        in_specs=[two(GLA_W), two(RET_W), two(cw), tok(GLA_W, COL_GLA_G // GLA_W), tok(RET_W, COL_RET_G // RET_W),
                  tok(cw), tok(cw)] + [whole(x) for x in params],
        out_specs=pl.BlockSpec((1, tm, d), lambda b, j: (b, j, 0)),
        compiler_params=_cparams(("parallel", "arbitrary")),
        name="mixer_post",
    )(og2, or2, ow2, z, z, g_rw, bonus, *params)


RW_FIELDS = ((COL_RW_R, RWKV_WIDTH, 0), (COL_RW_K, RWKV_WIDTH, RWKV_WIDTH), (COL_RW_V, RWKV_WIDTH, 2 * RWKV_WIDTH),
             (COL_RW_W1, RWKV_DECAY_RANK + RWKV_ICLR_RANK, 3 * RWKV_WIDTH),
             (COL_RW_G1, COL_RET_Q - COL_RW_G1, 3 * RWKV_WIDTH + RWKV_DECAY_RANK + RWKV_ICLR_RANK))


def _rwkv_prep_body(*refs, n_ctx, n_all):
    (tabs, zs, (w0_ref, w2_ref, a0_ref, a2_ref, g2_ref, kk_ref_, ka_ref, rk_ref, seg_ref, segt_ref),
     (r_out, k_out, v_out, kk_out, kka_out, lw_out, g_out, bonus_out)) = (
        refs[0:5], refs[5:20], refs[20:30], refs[30:38])
    j = pl.program_id(1)
    c = CHUNK
    is_ctx = j < n_ctx
    one = lambda cond: jnp.where(cond, 1.0, 0.0).astype(F32)
    f_prev = one(is_ctx & (j > 0))
    f_next = one(is_ctx & (j < n_ctx - 1))
    f_up = one((j > n_ctx) & (~is_ctx))
    f_down = one((j < n_all - 1) & (~is_ctx))
    f_ctx = one(is_ctx)
    row = lax.broadcasted_iota(jnp.int32, (c, 1), 0)
    fields = []
    for f in range(5):
        prv, cur, nxt = zs[3 * f][0], zs[3 * f + 1][0], zs[3 * f + 2][0]
        tab = tabs[f]
        mu, m_l, m_r, m_u, m_d, m_p, m_n = (tab[i:i + 1] for i in range(7))
        before = jnp.where(row == 0, prv[c - 1:c] * f_prev, pltpu.roll(cur, 1, axis=0))
        after = jnp.where(row == c - 1, nxt[0:1] * f_next, pltpu.roll(cur, c - 1, axis=0))
        shifted = ((m_l + f_ctx * (m_p - m_l)) * before + (m_r + f_ctx * (m_n - m_r)) * after
                   + (m_u * f_up) * prv + (m_d * f_down) * nxt)
        fields.append(cur + mu * (shifted - cur))
    r, k, v, wa, xg = fields
    dot = functools.partial(jnp.dot, preferred_element_type=F32)
    txw = jnp.tanh(wa[:, :RWKV_DECAY_RANK]).astype(BF16)
    for d in range(2):
        lw_out[d, 0] = -jax.nn.sigmoid(w0_ref[d:d + 1] + dot(txw, w2_ref[d])) * float(np.exp(-0.5))
    a = jax.nn.sigmoid(a0_ref[...] + dot(wa[:, RWKV_DECAY_RANK:].astype(BF16), a2_ref[...]))
    g_out[0] = dot(jax.nn.sigmoid(xg).astype(BF16), g2_ref[...])

    def head_sum(x):
        hi, mid = _split2(x)
        s = dot(hi, seg_ref[...]) + dot(mid, seg_ref[...])
        hi, mid = _split2(s)
        return dot(hi, segt_ref[...]) + dot(mid, segt_ref[...])

    kraw = k * kk_ref_[...]
    kk = kraw * lax.rsqrt(head_sum(kraw * kraw) + 1e-12)
    k2 = k * (1.0 + (a - 1.0) * ka_ref[...])
    r_out[0], k_out[0], v_out[0], kk_out[0], kka_out[0] = r, k2, v, kk, kk * a
    bonus_out[0] = head_sum(r * k2 * rk_ref[...]) * v


def rwkv_prep(z, mu, w0, w2, a0, a2, g2, k_k, k_a, r_k, n_ctx):
    bsz, t, _ = z.shape
    cw = RWKV_WIDTH
    n_all = t // CHUNK
    assert CHUNK == GRID_W
    quarter, half = RWKV_N // 4, RWKV_N // 2
    tabs, z_specs = [], []
    for col, width, ch0 in RW_FIELDS:
        n_real = min(width, RWKV_N - ch0)
        ch = ch0 + np.arange(width)
        real = np.arange(width) < n_real
        tab = np.zeros((8, width), np.float32)
        tab[1], tab[2] = real & (ch < quarter), real & (ch >= quarter) & (ch < 2 * quarter)
        tab[3], tab[4] = real & (ch >= 2 * quarter) & (ch < 3 * quarter), real & (ch >= 3 * quarter)
        tab[5], tab[6] = real & (ch < half), real & (ch >= half)
        tab = jnp.asarray(tab).at[0, :n_real].set(mu[ch0:ch0 + n_real])
        tabs.append(tab)
        blk = col // width
        assert col % width == 0
        z_specs += [pl.BlockSpec((1, CHUNK, width), lambda b, j, blk=blk: (b, jnp.maximum(j - 1, 0), blk)),
                    pl.BlockSpec((1, CHUNK, width), lambda b, j, blk=blk: (b, j, blk)),
                    pl.BlockSpec((1, CHUNK, width), lambda b, j, blk=blk: (b, jnp.minimum(j + 1, n_all - 1), blk))]
    gpad = RW_FIELDS[4][1] - RWKV_GATE_RANK
    seg = np.zeros((cw, 128), np.float32)
    seg[np.arange(cw), np.arange(cw) // RWKV_HEAD] = 1.0
    params = [w0, w2.astype(BF16), a0[None], a2.astype(BF16), jnp.pad(g2, ((0, gpad), (0, 0))).astype(BF16),
              k_k[None], k_a[None], r_k.reshape(1, cw), jnp.asarray(seg, BF16), jnp.asarray(seg.T, BF16)]

    def whole(x):
        return pl.BlockSpec(x.shape, lambda b, j, nd=x.ndim: (0,) * nd)

    tok = pl.BlockSpec((1, CHUNK, cw), lambda b, j: (b, j, 0))
    tok_sds = jax.ShapeDtypeStruct((bsz, t, cw), F32)
    return pl.pallas_call(
        functools.partial(_rwkv_prep_body, n_ctx=n_ctx, n_all=n_all),
        out_shape=(tok_sds,) * 5 + (jax.ShapeDtypeStruct((2, bsz, t, cw), F32), tok_sds, tok_sds),
        grid=(bsz, n_all),
        in_specs=[whole(x) for x in tabs] + z_specs + [whole(x) for x in params],
        out_specs=(tok,) * 5 + (pl.BlockSpec((2, 1, CHUNK, cw), lambda b, j: (0, b, j, 0)), tok, tok),
        compiler_params=_cparams(("parallel", "arbitrary")),
        name="rwkv_prep",
    )(*tabs, *([z] * 15), *params)


def rwkv_mixer(z, mu, w0, w2, a0, a2, g2, k_k, k_a, r_k, n_ctx):
    r, k, v, kk, kka, lw2, g, bonus = rwkv_prep(z, mu, w0, w2, a0, a2, g2, k_k, k_a, r_k, n_ctx)
    return rwkv_scan(r, k, v, kk, kka, lw2, n_ctx), g, bonus


def kernel(x, c, ctx, c_ctx, ada_w, ada_b, pre_mix, post_mix, pre_ffn, post_ffn, w_in, w_out, gla_a2, gla_a_bias, gla_norm_w, ret_decay_logit, rwkv_mu, rwkv_w0, rwkv_w2, rwkv_a0, rwkv_a2, rwkv_g2, rwkv_k_k, rwkv_k_a, rwkv_r_k, rwkv_lnx_w, rwkv_lnx_b, router_w, router_bias, exp_gate, exp_up, exp_down, shared_gate, shared_up, shared_down):
    bsz, s, d = x.shape
    lc = ctx.shape[1]
    t = lc + s
    depth = ada_w.shape[0]
    n_ctx = lc // CHUNK
    ctx_tiles = lc // RESNORM_TM
    cvec = jnp.concatenate([jax.nn.silu(c), jax.nn.silu(c_ctx)[None], jnp.zeros((8 - bsz - 1, d), F32)], 0)
    mods = [(matmul(cvec, ada_w, tm=8, tn=1024, name="adaln", layer=l) + ada_b[l])[:, None, :] for l in range(depth)]
    xs = jnp.concatenate([ctx, x], 1)
    (hb,) = resnorm(xs, ctx_tiles, ("hb",), pre=(mods[0], 0, 1, pre_mix[0]))
    for l in range(depth):
        ctx_out = l < depth - 1
        out_from = 0 if ctx_out else lc
        wi = _permute_cols(w_in[l]).astype(BF16)
        z = matmul(hb.reshape(bsz * t, d), wi, tm=_tile(bsz * t, 1100), tn=1024,
                   name="w_in").reshape(bsz, t, N_IN_PAD)
        og2 = gla_mixer(z, gla_a2[l], gla_a_bias[l], n_ctx)
        or2 = ret_mixer(z, ret_decay_logit[l], n_ctx)
        ow2, g_rw, bonus = rwkv_mixer(z, rwkv_mu[l], rwkv_w0[l], rwkv_w2[l], rwkv_a0[l], rwkv_a2[l],
                                      rwkv_g2[l], rwkv_k_k[l], rwkv_k_a[l], rwkv_r_k[l], n_ctx)
        y = mixer_post(z, og2, or2, ow2, g_rw, bonus, gla_norm_w[l], rwkv_lnx_w[l], rwkv_lnx_b[l], out_from)
        t_out = t - out_from
        y = matmul(y.reshape(bsz * t_out, d), w_out, tm=_tile(bsz * t_out, 1100), tn=512,
                   name="w_out", layer=l).reshape(bsz, t_out, d)
        xs, h32, hp = resnorm(xs, ctx_tiles, ("x", "h32", "hp"), res=(y, mods[l], 2, post_mix[l]),
                              pre=(mods[l], 3, 4, pre_ffn[l]), x_off=out_from // RESNORM_TM)
        if not ctx_out:
            ctx_tiles = 0
        f = moe_ffn(h32.reshape(-1, d), hp.reshape(-1, d // 2), l, router_w, router_bias, exp_gate, exp_up,
                    exp_down, shared_gate, shared_up, shared_down).reshape(xs.shape)
        if ctx_out:
            xs, hb = resnorm(xs, ctx_tiles, ("x", "hb"), res=(f, mods[l], 5, post_ffn[l]),
                             pre=(mods[l + 1], 0, 1, pre_mix[l + 1]))
        else:
            (xs,) = resnorm(xs, ctx_tiles, ("x",), res=(f, mods[l], 5, post_ffn[l]))
    return xs
```

```python
import functools

import numpy as np
import jax
import jax.numpy as jnp
from jax import lax
from jax.experimental import pallas as pl
from jax.experimental.pallas import tpu as pltpu

F32 = jnp.float32
BF16 = jnp.bfloat16

CHUNK = 64
LEVELS = (32, 16, 8, 4, 2, 1)

GLA_HEADS, GLA_DK, GLA_DV, GLA_RANK, GLA_TAU = 6, 128, 256, 16, 16.0
RET_HEADS, RET_DK, RET_DV = 5, 128, 256
ROPE_BASE = 10000.0
RWKV_HEADS, RWKV_HEAD = 20, 64
RWKV_WIDTH = RWKV_HEADS * RWKV_HEAD
RWKV_DECAY_RANK, RWKV_ICLR_RANK, RWKV_GATE_RANK = 128, 128, 480
GN_EPS = 64e-5
RMS_EPS = 1e-6
N_EXPERTS, TOP_K, D_EXPERT, ROUTED_SCALE = 64, 8, 384, 2.5
GRID_W = 64

GLA_QK, GLA_W = GLA_HEADS * GLA_DK, GLA_HEADS * GLA_DV
RET_QK, RET_W = RET_HEADS * RET_DK, RET_HEADS * RET_DV
GLA_N = 2 * GLA_QK + 2 * GLA_W + GLA_RANK
RET_N = 2 * RET_QK + 2 * RET_W
RWKV_N = 3 * RWKV_WIDTH + RWKV_DECAY_RANK + RWKV_ICLR_RANK + RWKV_GATE_RANK

COL_GLA_Q, COL_GLA_K, COL_GLA_V, COL_GLA_G = 0, 768, 1536, 3072
COL_RW_G1 = 4608
COL_RET_Q, COL_RET_K, COL_RET_V, COL_RET_G = 5120, 5760, 6400, 7680
COL_RW_R, COL_RW_K, COL_RW_V = 8960, 10240, 11520
COL_RW_W1, COL_RW_A1, COL_GLA_LR = 12800, 12928, 13056
N_IN_PAD = 13312

VMEM_LIMIT = 56 * 1024 * 1024


def _cparams(sem):
    return pltpu.CompilerParams(dimension_semantics=sem, vmem_limit_bytes=VMEM_LIMIT)


def _mm_body(a_ref, b_ref, o_ref):
    o_ref[...] = jnp.dot(a_ref[...].astype(BF16), b_ref[...].astype(BF16),
                         preferred_element_type=F32).astype(o_ref.dtype)


def matmul(a, b, *, tm, tn, out_dtype=F32, name="mm", layer=None):
    m, k = a.shape
    n = b.shape[-1]
    assert m % tm == 0 and n % tn == 0, (a.shape, b.shape, tm, tn)
    if layer is None:
        b_spec = pl.BlockSpec((k, tn), lambda i, j: (0, j))
    else:
        b_spec = pl.BlockSpec((None, k, tn), lambda i, j: (layer, 0, j))
    return pl.pallas_call(
        _mm_body,
        out_shape=jax.ShapeDtypeStruct((m, n), out_dtype),
        grid=(m // tm, n // tn),
        in_specs=[pl.BlockSpec((tm, k), lambda i, j: (i, 0)), b_spec],
        out_specs=pl.BlockSpec((tm, tn), lambda i, j: (i, j)),
        compiler_params=_cparams(("parallel", "arbitrary")),
        name=name,
    )(a, b)


def _scan_consts():
    c = CHUNK
    cum = np.zeros((2, 2 * c + 8, c), np.float32)
    masks = np.zeros((2, len(LEVELS), c, c), np.float32)
    tri = np.tril(np.ones((c, c), np.float32))
    cum[0, :c] = tri
    cum[0, c:2 * c] = tri
    cum[1, :c] = tri.T
    cum[1, c:2 * c] = np.triu(np.ones((c, c), np.float32), 1)
    cum[:, 2 * c:] = 1.0
    for li, s in enumerate(LEVELS):
        for i in range(c):
            for j in range(c):
                if (i & s) and not (j & s) and i // (2 * s) == j // (2 * s):
                    masks[0, li, i, j] = 1.0
        masks[1, li] = masks[0, li].T
    return jnp.asarray(cum, BF16), jnp.asarray(masks)


def _split3(x):
    hi = x.astype(BF16)
    r1 = x - hi.astype(F32)
    mid = r1.astype(BF16)
    lo = (r1 - mid.astype(F32)).astype(BF16)
    return hi, mid, lo


def _cumsums(cum, la):
    d = la.shape[1]
    r = jnp.dot(cum, jnp.concatenate(_split3(la), axis=1), preferred_element_type=F32)
    return r[:, :d] + r[:, d:2 * d] + r[:, 2 * d:]


def _boundary(cc, s):
    c, d = cc.shape
    if s >= 8:
        parts = [jnp.broadcast_to(cc[r0 + s - 1:r0 + s], (2 * s, d)) for r0 in range(0, c, 2 * s)]
        return parts[0] if len(parts) == 1 else jnp.concatenate(parts, axis=0)
    cc3 = cc.reshape(c // 8, 8, d)
    if s == 4:
        return jnp.broadcast_to(cc3[:, 3:4], cc3.shape).reshape(c, d)
    sub = lax.broadcasted_iota(jnp.int32, cc3.shape, 1)
    lo = jnp.broadcast_to(cc3[:, 1:2], cc3.shape)
    hi = jnp.broadcast_to(cc3[:, 5:6], cc3.shape)
    return jnp.where(sub < 4, lo, hi).reshape(c, d)


def _nt(a, b):
    return lax.dot_general(a, b, (((1,), (1,)), ((), ())), preferred_element_type=F32)


def _tn(a, b):
    return lax.dot_general(a, b, (((0,), (0,)), ((), ())), preferred_element_type=F32)


def _gla_body(cum_ref, mask_ref, q_ref, k_ref, v_ref, lr_ref, a2_ref, ab_ref, o_ref, st_ref, *,
              heads, dk, dv, scale):
    @pl.when(pl.program_id(2) == 0)
    def _():
        st_ref[...] = jnp.zeros_like(st_ref)

    cum = cum_ref[0]
    c = CHUNK
    gx = jnp.dot(lr_ref[0].astype(BF16), a2_ref[0], preferred_element_type=F32) + ab_ref[0]
    la_all = (jnp.minimum(gx, 0.0) - jnp.log(1.0 + jnp.exp(-jnp.abs(gx)))) * (1.0 / GLA_TAU)
    hs = range(heads)
    qs_, ks_, vs_, las, bqs, ccs, tots = [], [], [], [], [], [], []
    for h in hs:
        la = la_all[:, h * dk:(h + 1) * dk]
        rs = _cumsums(cum, la)
        qs_.append(q_ref[0, :, h * dk:(h + 1) * dk] * scale)
        ks_.append(k_ref[0, :, h * dk:(h + 1) * dk])
        vs_.append(v_ref[0, :, h * dv:(h + 1) * dv])
        las.append(la)
        bqs.append(rs[:c])
        ccs.append(rs[c:2 * c])
        tots.append(rs[2 * c:2 * c + 1])
    sc = [None] * heads
    for li, s in enumerate(LEVELS):
        for h in hs:
            q, k = qs_[h], ks_[h]
            if s == 1:
                qd, kd = q * jnp.exp(las[h]), k
            else:
                g = bqs[h] - _boundary(ccs[h], s)
                qd = q * jnp.exp(jnp.minimum(g, 0.0))
                kd = k * jnp.exp(jnp.minimum(-g, 0.0))
            p = _nt(qd.astype(BF16), kd.astype(BF16)) * mask_ref[0, li]
            sc[h] = p if sc[h] is None else sc[h] + p
    for h in hs:
        q, k, v, bq, tot = qs_[h], ks_[h], vs_[h], bqs[h], tots[h]
        vb = v.astype(BF16)
        dg = jnp.sum(q * k, axis=1, keepdims=True)
        st = st_ref[h]
        o = (jnp.dot(sc[h].astype(BF16), vb, preferred_element_type=F32) + dg * v
             + _nt((q * jnp.exp(bq)).astype(BF16), st.astype(BF16)))
        o_ref[0, 0, :, h * dv:(h + 1) * dv] = o
        kbar = k * jnp.exp(tot - bq)
        st_ref[h] = st * jnp.exp(tot) + _tn(vb, kbar.astype(BF16))


def _ret_body(lam_ref, q_ref, k_ref, v_ref, o_ref, st_ref, *, heads, dk, dv):
    d = pl.program_id(0)

    @pl.when(pl.program_id(2) == 0)
    def _():
        st_ref[...] = jnp.zeros_like(st_ref)

    c = CHUNK
    ri = lax.broadcasted_iota(jnp.int32, (c, c), 0)
    ci = lax.broadcasted_iota(jnp.int32, (c, c), 1)
    dist = jnp.where(d == 0, ri - ci, ci - ri)
    live = dist >= 0
    distf = jnp.maximum(dist, 0).astype(F32)
    row = lax.broadcasted_iota(jnp.int32, (c, 1), 0)
    pos = jnp.where(d == 0, row + 1, c - row).astype(F32)
    for h in range(heads):
        lam = lam_ref[d * heads + h]
        q = q_ref[0, :, h * dk:(h + 1) * dk]
        k = k_ref[0, :, h * dk:(h + 1) * dk]
        v = v_ref[0, :, h * dv:(h + 1) * dv]
        vb = v.astype(BF16)
        decay = jnp.where(live, jnp.exp(lam * distf), 0.0)
        sc = _nt(q.astype(BF16), k.astype(BF16)) * decay
        st = st_ref[h]
        o = (jnp.dot(sc.astype(BF16), vb, preferred_element_type=F32)
             + _nt((q * jnp.exp(lam * pos)).astype(BF16), st.astype(BF16)))
        o_ref[0, 0, :, h * dv:(h + 1) * dv] = o
        kbar = k * jnp.exp(lam * (c - pos))
        st_ref[h] = st * jnp.exp(lam * c) + _tn(vb, kbar.astype(BF16))


def _chunk_index(d, c, n_ctx, n_all):
    back = jnp.where(c < n_ctx, n_ctx - 1 - c, n_ctx + n_all - 1 - c)
    return jnp.where(d == 0, c, back)


def _col_spec(width, col, ch):
    assert col % width == 0
    return pl.BlockSpec((1, CHUNK, width), lambda d, b, c, *_: (b, ch(d, c), col // width))


def gla_scan(z, a2, a_bias, n_ctx):
    bsz, t, _ = z.shape
    heads, dk, dv = GLA_HEADS, GLA_DK, GLA_DV
    n_all = t // CHUNK
    cum, masks = _scan_consts()
    lr_w = 128
    a2p = jnp.pad(a2, ((0, 0), (0, lr_w - GLA_RANK), (0, 0))).astype(BF16)
    ch = functools.partial(_chunk_index, n_ctx=n_ctx, n_all=n_all)
    body = functools.partial(_gla_body, heads=heads, dk=dk, dv=dv, scale=dk ** -0.5)
    return pl.pallas_call(
        body,
        out_shape=jax.ShapeDtypeStruct((2, bsz, t, heads * dv), F32),
        grid=(2, bsz, n_all),
        in_specs=[
            pl.BlockSpec((1,) + cum.shape[1:], lambda d, b, c: (d, 0, 0)),
            pl.BlockSpec((1,) + masks.shape[1:], lambda d, b, c: (d, 0, 0, 0)),
            _col_spec(heads * dk, COL_GLA_Q, ch),
            _col_spec(heads * dk, COL_GLA_K, ch),
            _col_spec(heads * dv, COL_GLA_V, ch),
            _col_spec(lr_w, COL_GLA_LR, ch),
            pl.BlockSpec((1, lr_w, heads * dk), lambda d, b, c: (d, 0, 0)),
            pl.BlockSpec((1, 1, heads * dk), lambda d, b, c: (d, 0, 0)),
        ],
        out_specs=pl.BlockSpec((1, 1, CHUNK, heads * dv), lambda d, b, c: (d, b, ch(d, c), 0)),
        scratch_shapes=[pltpu.VMEM((heads, dv, dk), F32)],
        compiler_params=_cparams(("parallel", "parallel", "arbitrary")),
        name="gla_scan",
    )(cum, masks, z, z, z, z, a2p, a_bias[:, None, :])


def ret_scan(q, k, z, lam, n_ctx):
    bsz, t, _ = q.shape
    heads, dk, dv = RET_HEADS, RET_DK, RET_DV
    n_all = t // CHUNK
    ch = functools.partial(_chunk_index, n_ctx=n_ctx, n_all=n_all)
    body = functools.partial(_ret_body, heads=heads, dk=dk, dv=dv)
    grid_spec = pltpu.PrefetchScalarGridSpec(
        num_scalar_prefetch=1,
        grid=(2, bsz, n_all),
        in_specs=[
            pl.BlockSpec((1, CHUNK, heads * dk), lambda d, b, c, lam: (b, ch(d, c), 0)),
            pl.BlockSpec((1, CHUNK, heads * dk), lambda d, b, c, lam: (b, ch(d, c), 0)),
            _col_spec(heads * dv, COL_RET_V, ch),
        ],
        out_specs=pl.BlockSpec((1, 1, CHUNK, heads * dv), lambda d, b, c, lam: (d, b, ch(d, c), 0)),
        scratch_shapes=[pltpu.VMEM((heads, dv, dk), F32)],
    )
    return pl.pallas_call(
        body,
        out_shape=jax.ShapeDtypeStruct((2, bsz, t, heads * dv), F32),
        grid_spec=grid_spec,
        compiler_params=_cparams(("parallel", "parallel", "arbitrary")),
        name="ret_scan",
    )(lam.reshape(-1).astype(F32), q, k, z)


RWKV_GROUP = 4
RWKV_GW = RWKV_GROUP * RWKV_HEAD


def _rwkv_consts():
    c = CHUNK
    gw = RWKV_GW
    cum = np.zeros((2, 2 * c + 16, c), np.float32)
    tri = np.tril(np.ones((c, c), np.float32))
    mid = c // 2
    cum[0, :c] = tri - tri[mid - 1][None, :]
    cum[0, c:2 * c] = 1.0 - tri
    cum[0, 2 * c + 8:] = tri[mid - 1][None, :]
    cum[1, :c] = tri.T - tri.T[mid][None, :]
    cum[1, c:2 * c] = 1.0 - tri.T
    cum[1, 2 * c + 8:] = tri.T[mid][None, :]
    cum[:, 2 * c:2 * c + 8] = 1.0
    t = np.arange(c)[:, None]
    s = np.tile(np.arange(c), RWKV_GROUP)[None, :]
    masks = np.zeros((2, 3, c, gw), np.float32)
    masks[0, 0], masks[0, 1] = s < t, s <= t
    masks[1, 0], masks[1, 1] = s > t, s >= t
    masks[:, 2] = s == t
    return jnp.asarray(cum, BF16), jnp.asarray(masks)


def _rwkv_body(cum_ref, mask_ref, r_ref, k_ref, v_ref, kk_ref, kka_ref, lw_ref, o_ref, st_ref, *, groups):
    @pl.when(pl.program_id(2) == 0)
    def _():
        st_ref[...] = jnp.zeros_like(st_ref)

    c, gw = CHUNK, RWKV_GW
    cum = cum_ref[0]
    strict, incl, eye = mask_ref[0, 0], mask_ref[0, 1], mask_ref[0, 2]
    rb = lax.broadcasted_iota(jnp.int32, (gw, gw), 0) // RWKV_HEAD
    cb = lax.broadcasted_iota(jnp.int32, (gw, gw), 1) // RWKV_HEAD
    same_head = rb == cb

    def bd(x):
        xb = x.astype(BF16)
        return jnp.where(same_head, jnp.concatenate([xb] * RWKV_GROUP, axis=0), jnp.zeros((), BF16))

    def mm(a, b):
        return jnp.dot(a.astype(BF16), b, preferred_element_type=F32)

    gs = range(groups)
    sls = [slice(g * gw, (g + 1) * gw) for g in gs]
    pre = []
    for sl in sls:
        r, k, v = r_ref[0, :, sl], k_ref[0, :, sl], v_ref[0, :, sl]
        kk, kka, lw = kk_ref[0, :, sl], kka_ref[0, :, sl], lw_ref[0, 0, :, sl]
        rs = _cumsums(cum, lw)
        gm, ge, tot, gmid = rs[:c], rs[c:2 * c], rs[2 * c:2 * c + 1], rs[2 * c + 8:2 * c + 9]
        e_in, e_out, e_end = jnp.exp(gm), jnp.exp(-gm), jnp.exp(ge)
        kt_rel = kk * (e_in * jnp.exp(-lw))
        lhs = jnp.concatenate([kt_rel, r * e_in], axis=0)
        pre.append(dict(lhs=lhs.astype(BF16), lhs_abs=(lhs * jnp.exp(gmid)).astype(BF16),
                        kka_o=kka * e_out, k_o=k * e_out, v=v, tot=tot,
                        kv_end=jnp.concatenate([k * e_end, kka * e_end], axis=0).astype(BF16)))
    s_b = [_nt(q["lhs"], bd(q["kka_o"])) for q in pre]
    s_k = [_nt(q["lhs"], bd(q["k_o"])) for q in pre]
    ns = [-(s[:c] * strict) for s in s_b]
    xs = [eye + n for n in ns]
    ps = [mm(n, bd(n)) for n in ns]
    for _ in range(4):
        pps = [mm(jnp.concatenate([p, x], axis=0), bd(p)) for p, x in zip(ps, xs)]
        ps = [pp[:c] for pp in pps]
        xs = [x + pp[c:] for x, pp in zip(xs, pps)]
    xs = [x + mm(x, bd(p)) for p, x in zip(ps, xs)]
    sts = [st_ref[g] for g in gs]
    hs = [_nt(q["lhs_abs"], st.astype(BF16)) for q, st in zip(pre, sts)]
    vs = [mm(jnp.concatenate([sk[:c] * strict, sk[c:] * incl], axis=0), bd(q["v"])) for sk, q in zip(s_k, pre)]
    us = [mm(x, bd(h[:c] + w[:c])) for x, h, w in zip(xs, hs, vs)]
    for g in gs:
        o_ref[0, 0, :, sls[g]] = hs[g][c:] + vs[g][c:] - mm(s_b[g][c:] * incl, bd(us[g]))
    for g in gs:
        upd = _tn(jnp.concatenate([pre[g]["v"], -us[g]], axis=0).astype(BF16), pre[g]["kv_end"])
        st_ref[g] = sts[g] * jnp.exp(pre[g]["tot"]) + jnp.where(same_head, upd, 0.0)


def rwkv_scan(r, k, v, kk, kka, lw2, n_ctx):
    bsz, t, ch_w = r.shape
    groups = ch_w // RWKV_GW
    n_all = t // CHUNK
    cum, masks = _rwkv_consts()
    ch = functools.partial(_chunk_index, n_ctx=n_ctx, n_all=n_all)
    tok = pl.BlockSpec((1, CHUNK, ch_w), lambda d, b, c: (b, ch(d, c), 0))
    dir_tok = pl.BlockSpec((1, 1, CHUNK, ch_w), lambda d, b, c: (d, b, ch(d, c), 0))
    return pl.pallas_call(
        functools.partial(_rwkv_body, groups=groups),
        out_shape=jax.ShapeDtypeStruct((2, bsz, t, ch_w), F32),
        grid=(2, bsz, n_all),
        in_specs=[
            pl.BlockSpec((1,) + cum.shape[1:], lambda d, b, c: (d, 0, 0)),
            pl.BlockSpec((1,) + masks.shape[1:], lambda d, b, c: (d, 0, 0, 0)),
            tok, tok, tok, tok, tok, dir_tok,
        ],
        out_specs=dir_tok,
        scratch_shapes=[pltpu.VMEM((groups, RWKV_GW, RWKV_GW), F32)],
        compiler_params=_cparams(("parallel", "parallel", "arbitrary")),
        name="rwkv_scan",
    )(cum, masks, r, k, v, kk, kka, lw2)


def _split2(x):
    hi = x.astype(BF16)
    return hi, (x - hi.astype(F32)).astype(BF16)


ROUTER_LANES = 128


def _router_body(h_ref, w_ref, bias_ref, idx_ref, gate_ref, *, n_experts, top_k):
    a_hi, a_mid = _split2(h_ref[...])
    b_hi, b_mid = _split2(w_ref[...])
    dot = functools.partial(jnp.dot, preferred_element_type=F32)
    scores = jax.nn.sigmoid(dot(a_hi, b_hi) + (dot(a_hi, b_mid) + dot(a_mid, b_hi)))
    lane = lax.broadcasted_iota(jnp.int32, scores.shape, 1)
    cand = jnp.where(lane < n_experts, scores + bias_ref[...], -jnp.inf)
    idx_out = jnp.zeros(scores.shape, jnp.int32)
    val_out = jnp.zeros(scores.shape, F32)
    total = jnp.zeros((scores.shape[0], 1), F32)
    for j in range(top_k):
        best = jnp.max(cand, axis=1, keepdims=True)
        arg = jnp.min(jnp.where(cand == best, lane, ROUTER_LANES), axis=1, keepdims=True)
        pick = lane == arg
        val = jnp.sum(jnp.where(pick, scores, 0.0), axis=1, keepdims=True)
        cand = jnp.where(pick, -jnp.inf, cand)
        idx_out = jnp.where(lane == j, arg, idx_out)
        val_out = jnp.where(lane == j, val, val_out)
        total = total + val
    idx_ref[...] = idx_out
    gate_ref[...] = ROUTED_SCALE * val_out / total


def router(h, router_w, router_bias, layer):
    t, d = h.shape
    e = router_w.shape[-1]
    tm = _tile(t, 600)
    w = jnp.pad(router_w[layer].astype(F32), ((0, 0), (0, ROUTER_LANES - e)))
    bias = jnp.pad(router_bias[layer].astype(F32), (0, ROUTER_LANES - e))[None]
    idx, gates = pl.pallas_call(
        functools.partial(_router_body, n_experts=e, top_k=TOP_K),
        out_shape=(jax.ShapeDtypeStruct((t, ROUTER_LANES), jnp.int32), jax.ShapeDtypeStruct((t, ROUTER_LANES), F32)),
        grid=(t // tm,),
        in_specs=[pl.BlockSpec((tm, d), lambda i: (i, 0)), pl.BlockSpec((d, ROUTER_LANES), lambda i: (0, 0)),
                  pl.BlockSpec((1, ROUTER_LANES), lambda i: (0, 0))],
        out_specs=(pl.BlockSpec((tm, ROUTER_LANES), lambda i: (i, 0)), pl.BlockSpec((tm, ROUTER_LANES), lambda i: (i, 0))),
        compiler_params=_cparams(("parallel",)),
        name="router",
    )(h, w, bias)
    return idx[:, :TOP_K], gates[:, :TOP_K]


MOE_BM = 256


def _tile(m, cap, mult=16):
    return max(t for t in range(mult, cap + 1, mult) if m % t == 0)


def _pack_pairs(x):
    n = x.shape[-1] // 2
    bits = lax.bitcast_convert_type(x, jnp.uint32)
    bits = (bits + jnp.uint32(0x7FFF) + ((bits >> 16) & jnp.uint32(1))) & jnp.uint32(0xFFFF0000)
    return bits[..., :n] | (bits[..., n:] >> 16)


def _unpack_pairs(w):
    hi = lax.bitcast_convert_type(w & jnp.uint32(0xFFFF0000), F32)
    lo = lax.bitcast_convert_type(w << 16, F32)
    return jnp.concatenate([hi, lo], axis=-1)


def _swiglu(x, wg, wu, wd, gate=None):
    h1 = jnp.dot(x, wg, preferred_element_type=F32)
    h2 = jnp.dot(x, wu, preferred_element_type=F32)
    a = h1 * jax.nn.sigmoid(h1) * h2
    if gate is not None:
        a = a * gate
    return jnp.dot(a.astype(BF16), wd, preferred_element_type=F32)


def _cast_rows(src, dst, rows):
    def body(j, carry):
        r = pl.multiple_of(j * rows, rows)
        dst[pl.ds(r, rows), :] = src[pl.ds(r, rows), :].astype(BF16)
        return carry
    lax.fori_loop(0, src.shape[0] // rows, body, 0)


def _experts_body(be_ref, nx_ref, nv_ref, tok_ref, h_hbm, gate_ref, wg_hbm, wu_hbm, wd_hbm, o_ref,
                  xbuf0, xbuf1, st_g, st_u, st_d, wg_b, wu_b, wd_b, xsem, wsem, *, layer, bm):
    i = pl.program_id(0)
    n_live = nv_ref[0]
    slot = lax.rem(i, 2)

    def row_copy(base, r, buf, s):
        return pltpu.make_async_copy(h_hbm.at[pl.ds(tok_ref[base + r], 1)], buf.at[pl.ds(r, 1)], xsem.at[s])

    def rows_wait(buf, s):
        pltpu.make_async_copy(h_hbm.at[pl.ds(0, bm)], buf, xsem.at[s]).wait()

    def weight_copies(e):
        return (pltpu.make_async_copy(wg_hbm.at[layer, e], st_g, wsem.at[0]),
                pltpu.make_async_copy(wu_hbm.at[layer, e], st_u, wsem.at[1]),
                pltpu.make_async_copy(wd_hbm.at[layer, e], st_d, wsem.at[2]))

    @pl.when(i == 0)
    def _():
        def body(r, carry):
            row_copy(0, r, xbuf0, 0).start()
            return carry
        lax.fori_loop(0, bm, body, 0, unroll=8)
        for cp in weight_copies(be_ref[0]):
            cp.start(priority=1)

    @pl.when((i == 0) | (be_ref[i] != be_ref[jnp.maximum(i - 1, 0)]))
    def _():
        for cp in weight_copies(be_ref[i]):
            cp.wait()
        _cast_rows(st_g, wg_b, 256)
        _cast_rows(st_u, wu_b, 256)
        _cast_rows(st_d, wd_b, 32)

        @pl.when(nx_ref[i] >= 0)
        def _():
            for cp in weight_copies(nx_ref[i]):
                cp.start(priority=1)

    dot = functools.partial(jnp.dot, preferred_element_type=F32)
    half = xbuf0.shape[1]
    kc = 256
    n_k = half // kc
    n_issue = n_k // 2
    per = bm // n_issue

    def step(cur, cur_s, nxt, nxt_s):
        rows_wait(cur, cur_s)
        base = jnp.minimum(i + 1, n_live - 1) * bm
        h1 = jnp.zeros((bm, wg_b.shape[1]), F32)
        h2 = jnp.zeros((bm, wg_b.shape[1]), F32)
        for s in range(n_k):
            if s < n_issue:
                for r in range(s * per, (s + 1) * per):
                    row_copy(base, r, nxt, nxt_s).start()
            w = cur[:, s * kc:(s + 1) * kc]
            x_a = lax.bitcast_convert_type(w & jnp.uint32(0xFFFF0000), F32).astype(BF16)
            x_b = lax.bitcast_convert_type(w << 16, F32).astype(BF16)
            rows_a, rows_b = slice(s * kc, (s + 1) * kc), slice(half + s * kc, half + (s + 1) * kc)
            h1 = h1 + (dot(x_a, wg_b[rows_a, :]) + dot(x_b, wg_b[rows_b, :]))
            h2 = h2 + (dot(x_a, wu_b[rows_a, :]) + dot(x_b, wu_b[rows_b, :]))
        a = (h1 * jax.nn.sigmoid(h1) * h2 * gate_ref[...]).astype(BF16)
        for s in range(n_k):
            y = jnp.concatenate([dot(a, wd_b[:, s * kc:(s + 1) * kc]),
                                 dot(a, wd_b[:, half + s * kc:half + (s + 1) * kc])], axis=1)
            o_ref[:, s * kc:(s + 1) * kc] = _pack_pairs(y)

        @pl.when(i == n_live - 1)
        def _():
            rows_wait(nxt, nxt_s)

    @pl.when((i < n_live) & (slot == 0))
    def _():
        step(xbuf0, 0, xbuf1, 1)

    @pl.when((i < n_live) & (slot == 1))
    def _():
        step(xbuf1, 1, xbuf0, 0)

    @pl.when(i >= n_live)
    def _():
        o_ref[...] = jnp.zeros_like(o_ref)


def experts(h_pad, gate_rows, exp_gate, exp_up, exp_down, layer, block_expert, next_expert, n_live, buf_tok):
    bm = MOE_BM
    n = buf_tok.shape[0]
    d = exp_gate.shape[-2]
    f = exp_gate.shape[-1]
    dp = h_pad.shape[1]
    any_spec = pl.BlockSpec(memory_space=pl.ANY)
    grid_spec = pltpu.PrefetchScalarGridSpec(
        num_scalar_prefetch=4,
        grid=(n // bm,),
        in_specs=[any_spec, pl.BlockSpec((bm, 1), lambda i, *_: (i, 0)), any_spec, any_spec, any_spec],
        out_specs=pl.BlockSpec((bm, dp), lambda i, *_: (i, 0)),
        scratch_shapes=[
            pltpu.VMEM((bm, dp), jnp.uint32), pltpu.VMEM((bm, dp), jnp.uint32),
            pltpu.VMEM((d, f), F32), pltpu.VMEM((d, f), F32), pltpu.VMEM((f, d), F32),
            pltpu.VMEM((d, f), BF16), pltpu.VMEM((d, f), BF16), pltpu.VMEM((f, d), BF16),
            pltpu.SemaphoreType.DMA((2,)), pltpu.SemaphoreType.DMA((3,)),
        ],
    )
    return pl.pallas_call(
        functools.partial(_experts_body, layer=layer, bm=bm),
        out_shape=jax.ShapeDtypeStruct((n, dp), jnp.uint32),
        grid_spec=grid_spec,
        compiler_params=_cparams(("arbitrary",)),
        name="experts",
    )(block_expert, next_expert, n_live, buf_tok, h_pad, gate_rows, exp_gate, exp_up, exp_down)


def _shared_body(x_ref, wg_ref, wu_ref, wd_ref, o_ref):
    o_ref[...] = _swiglu(x_ref[...].astype(BF16), wg_ref[...].astype(BF16), wu_ref[...].astype(BF16),
                         wd_ref[...].astype(BF16))


def shared_expert(h, sh_gate, sh_up, sh_down, layer):
    t, d = h.shape
    f = sh_gate.shape[-1]
    tm = _tile(t, 300)
    once = pl.Buffered(1)
    return pl.pallas_call(
        _shared_body,
        out_shape=jax.ShapeDtypeStruct((t, d), F32),
        grid=(t // tm,),
        in_specs=[pl.BlockSpec((tm, d), lambda i: (i, 0)),
                  pl.BlockSpec((None, d, f), lambda i: (layer, 0, 0), pipeline_mode=once),
                  pl.BlockSpec((None, d, f), lambda i: (layer, 0, 0), pipeline_mode=once),
                  pl.BlockSpec((None, f, d), lambda i: (layer, 0, 0), pipeline_mode=once)],
        out_specs=pl.BlockSpec((tm, d), lambda i: (i, 0)),
        compiler_params=_cparams(("arbitrary",)),
        name="shared",
    )(h, sh_gate, sh_up, sh_down)


COMBINE_TB = 64


def _combine_body(pos_ref, y_hbm, sh_ref, o_ref, buf, sem, *, tb, k):
    i = pl.program_id(0)
    n = pl.num_programs(0)
    slot = lax.rem(i, 2)

    def gather(blk, s):
        base = blk * tb * k

        def body(t, carry):
            for kk in range(k):
                p = pos_ref[base + t * k + kk]
                pltpu.make_async_copy(y_hbm.at[pl.ds(p, 1)], buf.at[s, pl.ds(kk * tb + t, 1)],
                                      sem.at[s]).start(priority=kk % 2)
            return carry
        lax.fori_loop(0, tb, body, 0, unroll=4)

    @pl.when(i == 0)
    def _():
        gather(0, 0)

    @pl.when(i + 1 < n)
    def _():
        gather(i + 1, 1 - slot)

    pltpu.make_async_copy(y_hbm.at[pl.ds(0, tb * k)], buf.at[slot], sem.at[slot]).wait()
    acc = sh_ref[...]
    for kk in range(k):
        acc = acc + _unpack_pairs(buf[slot, kk * tb:(kk + 1) * tb, :])
    o_ref[...] = acc


def combine(y, pos, shared):
    t, k = pos.shape
    d = shared.shape[1]
    dp = y.shape[1]
    tb = COMBINE_TB
    grid_spec = pltpu.PrefetchScalarGridSpec(
        num_scalar_prefetch=1,
        grid=(t // tb,),
        in_specs=[pl.BlockSpec(memory_space=pl.ANY), pl.BlockSpec((tb, d), lambda i, *_: (i, 0))],
        out_specs=pl.BlockSpec((tb, d), lambda i, *_: (i, 0)),
        scratch_shapes=[pltpu.VMEM((2, tb * k, dp), jnp.uint32), pltpu.SemaphoreType.DMA((2,))],
    )
    return pl.pallas_call(
        functools.partial(_combine_body, tb=tb, k=k),
        out_shape=jax.ShapeDtypeStruct((t, d), F32),
        grid_spec=grid_spec,
        compiler_params=_cparams(("arbitrary",)),
        name="combine",
    )(pos.reshape(-1), y, shared)


def moe_ffn(h, hp, layer, router_w, router_bias, exp_gate, exp_up, exp_down, sh_gate, sh_up, sh_down):
    t, d = h.shape
    e, k, bm = N_EXPERTS, TOP_K, MOE_BM
    i32 = jnp.int32
    idx, gates = router(h, router_w, router_bias, layer)
    flat_e = idx.reshape(-1)
    iota = jnp.arange(t * k, dtype=i32)
    e_sorted, order = lax.sort_key_val(flat_e, iota)
    _, rank = lax.sort_key_val(order, iota)
    bounds = jnp.searchsorted(e_sorted, jnp.arange(e + 1, dtype=i32), side='left',
                              method='compare_all').astype(i32)
    first, counts = bounds[:-1], bounds[1:] - bounds[:-1]
    padded = (counts + bm - 1) // bm * bm
    ends = jnp.cumsum(padded)
    starts = ends - padded
    pos = (iota + (starts - first)[e_sorted])[rank].reshape(t, k)
    n_blocks = -(-(t * k) // bm) + e
    blk_start = jnp.arange(n_blocks, dtype=i32) * bm
    block_valid = (blk_start < ends[-1]).astype(i32)
    last_e = jnp.max(jnp.where(counts > 0, jnp.arange(e, dtype=i32), 0))
    block_expert = jnp.where(block_valid > 0,
                             jnp.minimum(jnp.searchsorted(ends, blk_start, side='right',
                                                          method='compare_all').astype(i32), e - 1), last_e)
    present = jnp.where(counts > 0, jnp.arange(e, dtype=i32), e)
    nxt = lax.cummin(jnp.concatenate([present[1:], jnp.full((1,), e, i32)]), reverse=True)
    next_expert = jnp.where(nxt < e, nxt, -1)[block_expert]
    blk_off = blk_start - starts[block_expert]
    blk_live = jnp.where(block_valid > 0, jnp.clip(counts[block_expert] - blk_off, 0, bm), 0)
    lane = jnp.arange(bm, dtype=i32)[None, :]
    live = (lane < blk_live[:, None]).reshape(-1)
    src = order[jnp.clip((first[block_expert] + blk_off)[:, None] + lane, 0, t * k - 1).reshape(-1)]
    buf_tok = jnp.where(live, src // k, t).astype(i32)
    gate_rows = jnp.where(live, gates.reshape(-1)[src], 0.0)[:, None]

    h_pad = jnp.concatenate([hp, jnp.zeros((1, d // 2), jnp.uint32)], 0)
    n_live = (ends[-1] // bm).astype(i32).reshape(1)
    y = experts(h_pad, gate_rows, exp_gate, exp_up, exp_down, layer, block_expert, next_expert, n_live, buf_tok)
    return combine(y, pos, shared_expert(h, sh_gate, sh_up, sh_down, layer))


def _permute_cols(w):
    gla, ret, rw = w[..., :GLA_N], w[..., GLA_N:GLA_N + RET_N], w[..., GLA_N + RET_N:]
    rkv = 3 * RWKV_WIDTH
    lora = rkv + RWKV_DECAY_RANK + RWKV_ICLR_RANK

    def zeros(n):
        return jnp.zeros(w.shape[:-1] + (n,), w.dtype)

    parts = [gla[..., :GLA_N - GLA_RANK],
             rw[..., lora:], zeros(COL_RET_Q - COL_RW_G1 - RWKV_GATE_RANK),
             ret,
             rw[..., :lora],
             gla[..., GLA_N - GLA_RANK:], zeros(N_IN_PAD - COL_GLA_LR - GLA_RANK)]
    return jnp.concatenate(parts, -1)


RESNORM_TM = 128


def _resnorm_body(*refs, has_res, has_pre, outs):
    def rms(u):
        return u * lax.rsqrt(jnp.mean(u * u, axis=1, keepdims=True) + RMS_EPS)

    it = iter(refs)
    x = next(it)[0]
    if has_res:
        f, gate, post_w = next(it)[0], next(it)[0], next(it)[...]
        x = x + gate * (rms(f) * post_w)
    if has_pre:
        shift, scale, pre_w = next(it)[0], next(it)[0], next(it)[...]
        h = rms(x) * pre_w * (1.0 + scale) + shift
    for name in outs:
        out = next(it)
        if name == "x":
            out[0] = x
        elif name == "h32":
            out[0] = h
        elif name == "hb":
            out[0] = h.astype(BF16)
        else:
            out[0] = _pack_pairs(h)


def resnorm(xs, ctx_tiles, outs, res=None, pre=None, x_off=0):
    bsz, tx, d = xs.shape
    tm = RESNORM_TM
    t_out = tx - x_off * tm

    def mod_spec(col):
        return pl.BlockSpec((1, 1, d), lambda b, j: (jnp.where(j + x_off < ctx_tiles, bsz, b), 0, col))

    w_spec = pl.BlockSpec((1, d), lambda b, j: (0, 0))
    tok = pl.BlockSpec((1, tm, d), lambda b, j: (b, j, 0))
    args, in_specs = [xs], [pl.BlockSpec((1, tm, d), lambda b, j: (b, j + x_off, 0))]
    if res is not None:
        f, mod, col, w = res
        assert f.shape == (bsz, t_out, d)
        args += [f, mod, w[None]]
        in_specs += [tok, mod_spec(col), w_spec]
    if pre is not None:
        mod, shift_col, scale_col, w = pre
        args += [mod, mod, w[None]]
        in_specs += [mod_spec(shift_col), mod_spec(scale_col), w_spec]
    kinds = {"x": (d, F32), "h32": (d, F32), "hb": (d, BF16), "hp": (d // 2, jnp.uint32)}
    out_shape = tuple(jax.ShapeDtypeStruct((bsz, t_out, kinds[o][0]), kinds[o][1]) for o in outs)
    out_specs = tuple(pl.BlockSpec((1, tm, kinds[o][0]), lambda b, j: (b, j, 0)) for o in outs)
    return pl.pallas_call(
        functools.partial(_resnorm_body, has_res=res is not None, has_pre=pre is not None, outs=outs),
        out_shape=out_shape,
        grid=(bsz, t_out // tm),
        in_specs=in_specs,
        out_specs=out_specs,
        compiler_params=_cparams(("parallel", "arbitrary")),
        name="resnorm",
    )(*args)


def _rope(x, pos):
    d = x.shape[-1]
    inv = ROPE_BASE ** (-jnp.linspace(0.0, 1.0, d // 2, dtype=F32))
    ang = pos.astype(F32)[:, None] * inv[None, :]
    cos, sin = jnp.cos(ang)[:, None, :], jnp.sin(ang)[:, None, :]
    x1, x2 = x[..., :d // 2], x[..., d // 2:]
    return jnp.concatenate([x1 * cos - x2 * sin, x1 * sin + x2 * cos], -1)


def ret_mixer(z, decay_logit, n_ctx):
    bsz, t, _ = z.shape
    q, k = z[..., COL_RET_Q:COL_RET_Q + RET_QK], z[..., COL_RET_K:COL_RET_K + RET_QK]
    pos = jnp.arange(t)
    q = _rope(q.reshape(bsz, t, RET_HEADS, RET_DK), pos).reshape(bsz, t, RET_QK)
    k = (_rope(k.reshape(bsz, t, RET_HEADS, RET_DK), pos) * RET_DK ** -0.5).reshape(bsz, t, RET_QK)
    lam = jax.nn.log_sigmoid(decay_logit.astype(F32))
    return ret_scan(q, k, z, lam, n_ctx)


POST_TM = 128


def _mixer_post_body(og_ref, or_ref, ow_ref, gg_ref, gr_ref, gw_ref, bonus_ref, nw_ref, lw_ref, lb_ref,
                     seg_ref, segt_ref, y_ref):
    def silu(u):
        return u * jax.nn.sigmoid(u)

    def head_rms(x):
        return x * lax.rsqrt(jnp.mean(x * x, axis=1, keepdims=True) + RMS_EPS)

    og = og_ref[0, 0] + og_ref[1, 0]
    gg = gg_ref[0]
    for h in range(GLA_HEADS):
        sl = slice(h * GLA_DV, (h + 1) * GLA_DV)
        y_ref[0, :, sl] = (head_rms(og[:, sl]) * nw_ref[...] * silu(gg[:, sl])).astype(y_ref.dtype)
    orr = or_ref[0, 0] + or_ref[1, 0]
    gr = gr_ref[0]
    for h in range(RET_HEADS):
        sl = slice(h * RET_DV, (h + 1) * RET_DV)
        y_ref[0, :, GLA_W + h * RET_DV:GLA_W + (h + 1) * RET_DV] = (
            head_rms(orr[:, sl]) * silu(gr[:, sl])).astype(y_ref.dtype)
    dot = functools.partial(jnp.dot, preferred_element_type=F32)

    def head_mean(x):
        hi, mid = _split2(x)
        s = (dot(hi, seg_ref[...]) + dot(mid, seg_ref[...])) * (1.0 / RWKV_HEAD)
        hi, mid = _split2(s)
        return dot(hi, segt_ref[...]) + dot(mid, segt_ref[...])

    ow = ow_ref[0, 0] + ow_ref[1, 0]
    xc = ow - head_mean(ow)
    gn = xc * lax.rsqrt(head_mean(xc * xc) + GN_EPS) * lw_ref[...] + lb_ref[...]
    y_ref[0, :, GLA_W + RET_W:] = ((gn + bonus_ref[0]) * gw_ref[0]).astype(y_ref.dtype)


def mixer_post(z, og2, or2, ow2, g_rw, bonus, norm_w, lnx_w, lnx_b, out_from):
    bsz, t, _ = z.shape
    tm = POST_TM
    off = out_from // tm
    assert out_from % tm == 0 and t % tm == 0
    cw = RWKV_WIDTH
    seg = np.zeros((cw, 128), np.float32)
    seg[np.arange(cw), np.arange(cw) // RWKV_HEAD] = 1.0
    params = [norm_w[None], lnx_w[None], lnx_b[None], jnp.asarray(seg, BF16), jnp.asarray(seg.T, BF16)]

    def two(width):
        return pl.BlockSpec((2, 1, tm, width), lambda b, j: (0, b, j + off, 0))

    def tok(width, blk=0):
        return pl.BlockSpec((1, tm, width), lambda b, j: (b, j + off, blk))

    def whole(x):
        return pl.BlockSpec(x.shape, lambda b, j, nd=x.ndim: (0,) * nd)

    d = GLA_W + RET_W + cw
    return pl.pallas_call(
        _mixer_post_body,
        out_shape=jax.ShapeDtypeStruct((bsz, t - out_from, d), BF16),
        grid=(bsz, (t - out_from) // tm),
        in_specs=[two(GLA_W), two(RET_W), two(cw), tok(GLA_W, COL_GLA_G // GLA_W), tok(RET_W, COL_RET_G // RET_W),
                  tok(cw), tok(cw)] + [whole(x) for x in params],
        out_specs=pl.BlockSpec((1, tm, d), lambda b, j: (b, j, 0)),
        compiler_params=_cparams(("parallel", "arbitrary")),
        name="mixer_post",
    )(og2, or2, ow2, z, z, g_rw, bonus, *params)


RW_FIELDS = ((COL_RW_R, RWKV_WIDTH, 0), (COL_RW_K, RWKV_WIDTH, RWKV_WIDTH), (COL_RW_V, RWKV_WIDTH, 2 * RWKV_WIDTH),
             (COL_RW_W1, RWKV_DECAY_RANK + RWKV_ICLR_RANK, 3 * RWKV_WIDTH),
             (COL_RW_G1, COL_RET_Q - COL_RW_G1, 3 * RWKV_WIDTH + RWKV_DECAY_RANK + RWKV_ICLR_RANK))


def _rwkv_prep_body(*refs, n_ctx, n_all):
    (tabs, zs, (w0_ref, w2_ref, a0_ref, a2_ref, g2_ref, kk_ref_, ka_ref, rk_ref, seg_ref, segt_ref),
     (r_out, k_out, v_out, kk_out, kka_out, lw_out, g_out, bonus_out)) = (
        refs[0:5], refs[5:20], refs[20:30], refs[30:38])
    j = pl.program_id(1)
    c = CHUNK
    is_ctx = j < n_ctx
    one = lambda cond: jnp.where(cond, 1.0, 0.0).astype(F32)
    f_prev = one(is_ctx & (j > 0))
    f_next = one(is_ctx & (j < n_ctx - 1))
    f_up = one((j > n_ctx) & (~is_ctx))
    f_down = one((j < n_all - 1) & (~is_ctx))
    f_ctx = one(is_ctx)
    row = lax.broadcasted_iota(jnp.int32, (c, 1), 0)
    fields = []
    for f in range(5):
        prv, cur, nxt = zs[3 * f][0], zs[3 * f + 1][0], zs[3 * f + 2][0]
        tab = tabs[f]
        mu, m_l, m_r, m_u, m_d, m_p, m_n = (tab[i:i + 1] for i in range(7))
        before = jnp.where(row == 0, prv[c - 1:c] * f_prev, pltpu.roll(cur, 1, axis=0))
        after = jnp.where(row == c - 1, nxt[0:1] * f_next, pltpu.roll(cur, c - 1, axis=0))
        shifted = ((m_l + f_ctx * (m_p - m_l)) * before + (m_r + f_ctx * (m_n - m_r)) * after
                   + (m_u * f_up) * prv + (m_d * f_down) * nxt)
        fields.append(cur + mu * (shifted - cur))
    r, k, v, wa, xg = fields
    dot = functools.partial(jnp.dot, preferred_element_type=F32)
    txw = jnp.tanh(wa[:, :RWKV_DECAY_RANK]).astype(BF16)
    for d in range(2):
        lw_out[d, 0] = -jax.nn.sigmoid(w0_ref[d:d + 1] + dot(txw, w2_ref[d])) * float(np.exp(-0.5))
    a = jax.nn.sigmoid(a0_ref[...] + dot(wa[:, RWKV_DECAY_RANK:].astype(BF16), a2_ref[...]))
    g_out[0] = dot(jax.nn.sigmoid(xg).astype(BF16), g2_ref[...])

    def head_sum(x):
        hi, mid = _split2(x)
        s = dot(hi, seg_ref[...]) + dot(mid, seg_ref[...])
        hi, mid = _split2(s)
        return dot(hi, segt_ref[...]) + dot(mid, segt_ref[...])

    kraw = k * kk_ref_[...]
    kk = kraw * lax.rsqrt(head_sum(kraw * kraw) + 1e-12)
    k2 = k * (1.0 + (a - 1.0) * ka_ref[...])
    r_out[0], k_out[0], v_out[0], kk_out[0], kka_out[0] = r, k2, v, kk, kk * a
    bonus_out[0] = head_sum(r * k2 * rk_ref[...]) * v


def rwkv_prep(z, mu, w0, w2, a0, a2, g2, k_k, k_a, r_k, n_ctx):
    bsz, t, _ = z.shape
    cw = RWKV_WIDTH
    n_all = t // CHUNK
    assert CHUNK == GRID_W
    quarter, half = RWKV_N // 4, RWKV_N // 2
    tabs, z_specs = [], []
    for col, width, ch0 in RW_FIELDS:
        n_real = min(width, RWKV_N - ch0)
        ch = ch0 + np.arange(width)
        real = np.arange(width) < n_real
        tab = np.zeros((8, width), np.float32)
        tab[1], tab[2] = real & (ch < quarter), real & (ch >= quarter) & (ch < 2 * quarter)
        tab[3], tab[4] = real & (ch >= 2 * quarter) & (ch < 3 * quarter), real & (ch >= 3 * quarter)
        tab[5], tab[6] = real & (ch < half), real & (ch >= half)
        tab = jnp.asarray(tab).at[0, :n_real].set(mu[ch0:ch0 + n_real])
        tabs.append(tab)
        blk = col // width
        assert col % width == 0
        z_specs += [pl.BlockSpec((1, CHUNK, width), lambda b, j, blk=blk: (b, jnp.maximum(j - 1, 0), blk)),
                    pl.BlockSpec((1, CHUNK, width), lambda b, j, blk=blk: (b, j, blk)),
                    pl.BlockSpec((1, CHUNK, width), lambda b, j, blk=blk: (b, jnp.minimum(j + 1, n_all - 1), blk))]
    gpad = RW_FIELDS[4][1] - RWKV_GATE_RANK
    seg = np.zeros((cw, 128), np.float32)
    seg[np.arange(cw), np.arange(cw) // RWKV_HEAD] = 1.0
    params = [w0, w2.astype(BF16), a0[None], a2.astype(BF16), jnp.pad(g2, ((0, gpad), (0, 0))).astype(BF16),
              k_k[None], k_a[None], r_k.reshape(1, cw), jnp.asarray(seg, BF16), jnp.asarray(seg.T, BF16)]

    def whole(x):
        return pl.BlockSpec(x.shape, lambda b, j, nd=x.ndim: (0,) * nd)

    tok = pl.BlockSpec((1, CHUNK, cw), lambda b, j: (b, j, 0))
    tok_sds = jax.ShapeDtypeStruct((bsz, t, cw), F32)
    return pl.pallas_call(
        functools.partial(_rwkv_prep_body, n_ctx=n_ctx, n_all=n_all),
        out_shape=(tok_sds,) * 5 + (jax.ShapeDtypeStruct((2, bsz, t, cw), F32), tok_sds, tok_sds),
        grid=(bsz, n_all),
        in_specs=[whole(x) for x in tabs] + z_specs + [whole(x) for x in params],
        out_specs=(tok,) * 5 + (pl.BlockSpec((2, 1, CHUNK, cw), lambda b, j: (0, b, j, 0)), tok, tok),
        compiler_params=_cparams(("parallel", "arbitrary")),
        name="rwkv_prep",
    )(*tabs, *([z] * 15), *params)


def rwkv_mixer(z, mu, w0, w2, a0, a2, g2, k_k, k_a, r_k, n_ctx):
    r, k, v, kk, kka, lw2, g, bonus = rwkv_prep(z, mu, w0, w2, a0, a2, g2, k_k, k_a, r_k, n_ctx)
    return rwkv_scan(r, k, v, kk, kka, lw2, n_ctx), g, bonus


def kernel(x, c, ctx, c_ctx, ada_w, ada_b, pre_mix, post_mix, pre_ffn, post_ffn, w_in, w_out, gla_a2, gla_a_bias, gla_norm_w, ret_decay_logit, rwkv_mu, rwkv_w0, rwkv_w2, rwkv_a0, rwkv_a2, rwkv_g2, rwkv_k_k, rwkv_k_a, rwkv_r_k, rwkv_lnx_w, rwkv_lnx_b, router_w, router_bias, exp_gate, exp_up, exp_down, shared_gate, shared_up, shared_down):
    bsz, s, d = x.shape
    lc = ctx.shape[1]
    t = lc + s
    depth = ada_w.shape[0]
    n_ctx = lc // CHUNK
    ctx_tiles = lc // RESNORM_TM
    cvec = jnp.concatenate([jax.nn.silu(c), jax.nn.silu(c_ctx)[None], jnp.zeros((8 - bsz - 1, d), F32)], 0)
    mods = [(matmul(cvec, ada_w, tm=8, tn=1024, name="adaln", layer=l) + ada_b[l])[:, None, :] for l in range(depth)]
    xs = jnp.concatenate([ctx, x], 1)
    (hb,) = resnorm(xs, ctx_tiles, ("hb",), pre=(mods[0], 0, 1, pre_mix[0]))
    for l in range(depth):
        ctx_out = l < depth - 1
        out_from = 0 if ctx_out else lc
        wi = _permute_cols(w_in[l]).astype(BF16)
        z = matmul(hb.reshape(bsz * t, d), wi, tm=_tile(bsz * t, 1100), tn=1024,
                   name="w_in").reshape(bsz, t, N_IN_PAD)
        og2 = gla_scan(z, gla_a2[l], gla_a_bias[l], n_ctx)
        or2 = ret_mixer(z, ret_decay_logit[l], n_ctx)
        ow2, g_rw, bonus = rwkv_mixer(z, rwkv_mu[l], rwkv_w0[l], rwkv_w2[l], rwkv_a0[l], rwkv_a2[l],
                                      rwkv_g2[l], rwkv_k_k[l], rwkv_k_a[l], rwkv_r_k[l], n_ctx)
        y = mixer_post(z, og2, or2, ow2, g_rw, bonus, gla_norm_w[l], rwkv_lnx_w[l], rwkv_lnx_b[l], out_from)
        t_out = t - out_from
        y = matmul(y.reshape(bsz * t_out, d), w_out, tm=_tile(bsz * t_out, 1100), tn=512,
                   name="w_out", layer=l).reshape(bsz, t_out, d)
        xs, h32, hp = resnorm(xs, ctx_tiles, ("x", "h32", "hp"), res=(y, mods[l], 2, post_mix[l]),
                              pre=(mods[l], 3, 4, pre_ffn[l]), x_off=out_from // RESNORM_TM)
        if not ctx_out:
            ctx_tiles = 0
        f = moe_ffn(h32.reshape(-1, d), hp.reshape(-1, d // 2), l, router_w, router_bias, exp_gate, exp_up,
                    exp_down, shared_gate, shared_up, shared_down).reshape(xs.shape)
        if ctx_out:
            xs, hb = resnorm(xs, ctx_tiles, ("x", "hb"), res=(f, mods[l], 5, post_ffn[l]),
                             pre=(mods[l + 1], 0, 1, pre_mix[l + 1]))
        else:
            (xs,) = resnorm(xs, ctx_tiles, ("x",), res=(f, mods[l], 5, post_ffn[l]))
    return xs
```

```python
import functools

import numpy as np
import jax
import jax.numpy as jnp
from jax import lax
from jax.experimental import pallas as pl
from jax.experimental.pallas import tpu as pltpu

F32 = jnp.float32
BF16 = jnp.bfloat16

CHUNK = 64
LEVELS = (32, 16, 8, 4, 2, 1)

GLA_HEADS, GLA_DK, GLA_DV, GLA_RANK, GLA_TAU = 6, 128, 256, 16, 16.0
RET_HEADS, RET_DK, RET_DV = 5, 128, 256
ROPE_BASE = 10000.0
RWKV_HEADS, RWKV_HEAD = 20, 64
RWKV_WIDTH = RWKV_HEADS * RWKV_HEAD
RWKV_DECAY_RANK, RWKV_ICLR_RANK, RWKV_GATE_RANK = 128, 128, 480
GN_EPS = 64e-5
RMS_EPS = 1e-6
N_EXPERTS, TOP_K, D_EXPERT, ROUTED_SCALE = 64, 8, 384, 2.5
GRID_W = 64

GLA_QK, GLA_W = GLA_HEADS * GLA_DK, GLA_HEADS * GLA_DV
RET_QK, RET_W = RET_HEADS * RET_DK, RET_HEADS * RET_DV
GLA_N = 2 * GLA_QK + 2 * GLA_W + GLA_RANK
RET_N = 2 * RET_QK + 2 * RET_W
RWKV_N = 3 * RWKV_WIDTH + RWKV_DECAY_RANK + RWKV_ICLR_RANK + RWKV_GATE_RANK

COL_GLA_Q, COL_GLA_K, COL_GLA_V, COL_GLA_G = 0, 768, 1536, 3072
COL_RW_G1 = 4608
COL_RET_Q, COL_RET_K, COL_RET_V, COL_RET_G = 5120, 5760, 6400, 7680
COL_RW_R, COL_RW_K, COL_RW_V = 8960, 10240, 11520
COL_RW_W1, COL_RW_A1, COL_GLA_LR = 12800, 12928, 13056
N_IN_PAD = 13312

VMEM_LIMIT = 56 * 1024 * 1024


def _cparams(sem):
    return pltpu.CompilerParams(dimension_semantics=sem, vmem_limit_bytes=VMEM_LIMIT)


def _mm_body(a_ref, b_ref, o_ref):
    o_ref[...] = jnp.dot(a_ref[...].astype(BF16), b_ref[...].astype(BF16),
                         preferred_element_type=F32).astype(o_ref.dtype)


def matmul(a, b, *, tm, tn, out_dtype=F32, name="mm", layer=None):
    m, k = a.shape
    n = b.shape[-1]
    assert m % tm == 0 and n % tn == 0, (a.shape, b.shape, tm, tn)
    if layer is None:
        b_spec = pl.BlockSpec((k, tn), lambda i, j: (0, j))
    else:
        b_spec = pl.BlockSpec((None, k, tn), lambda i, j: (layer, 0, j))
    return pl.pallas_call(
        _mm_body,
        out_shape=jax.ShapeDtypeStruct((m, n), out_dtype),
        grid=(m // tm, n // tn),
        in_specs=[pl.BlockSpec((tm, k), lambda i, j: (i, 0)), b_spec],
        out_specs=pl.BlockSpec((tm, tn), lambda i, j: (i, j)),
        compiler_params=_cparams(("parallel", "arbitrary")),
        name=name,
    )(a, b)


def _scan_consts():
    c = CHUNK
    cum = np.zeros((2, 2 * c + 8, c), np.float32)
    masks = np.zeros((2, len(LEVELS), c, c), np.float32)
    tri = np.tril(np.ones((c, c), np.float32))
    cum[0, :c] = tri
    cum[0, c:2 * c] = tri
    cum[1, :c] = tri.T
    cum[1, c:2 * c] = np.triu(np.ones((c, c), np.float32), 1)
    cum[:, 2 * c:] = 1.0
    for li, s in enumerate(LEVELS):
        for i in range(c):
            for j in range(c):
                if (i & s) and not (j & s) and i // (2 * s) == j // (2 * s):
                    masks[0, li, i, j] = 1.0
        masks[1, li] = masks[0, li].T
    return jnp.asarray(cum, BF16), jnp.asarray(masks)


def _split3(x):
    hi = x.astype(BF16)
    r1 = x - hi.astype(F32)
    mid = r1.astype(BF16)
    lo = (r1 - mid.astype(F32)).astype(BF16)
    return hi, mid, lo


def _cumsums(cum, la):
    d = la.shape[1]
    r = jnp.dot(cum, jnp.concatenate(_split3(la), axis=1), preferred_element_type=F32)
    return r[:, :d] + r[:, d:2 * d] + r[:, 2 * d:]


def _boundary(cc, s):
    c, d = cc.shape
    if s >= 8:
        parts = [jnp.broadcast_to(cc[r0 + s - 1:r0 + s], (2 * s, d)) for r0 in range(0, c, 2 * s)]
        return parts[0] if len(parts) == 1 else jnp.concatenate(parts, axis=0)
    cc3 = cc.reshape(c // 8, 8, d)
    if s == 4:
        return jnp.broadcast_to(cc3[:, 3:4], cc3.shape).reshape(c, d)
    sub = lax.broadcasted_iota(jnp.int32, cc3.shape, 1)
    lo = jnp.broadcast_to(cc3[:, 1:2], cc3.shape)
    hi = jnp.broadcast_to(cc3[:, 5:6], cc3.shape)
    return jnp.where(sub < 4, lo, hi).reshape(c, d)


def _nt(a, b):
    return lax.dot_general(a, b, (((1,), (1,)), ((), ())), preferred_element_type=F32)


def _tn(a, b):
    return lax.dot_general(a, b, (((0,), (0,)), ((), ())), preferred_element_type=F32)


def _gla_body(cum_ref, mask_ref, q_ref, k_ref, v_ref, lr_ref, a2_ref, ab_ref, o_ref, st_ref, *,
              heads, dk, dv, scale):
    @pl.when(pl.program_id(2) == 0)
    def _():
        st_ref[...] = jnp.zeros_like(st_ref)

    cum = cum_ref[0]
    c = CHUNK
    gx = jnp.dot(lr_ref[0].astype(BF16), a2_ref[0], preferred_element_type=F32) + ab_ref[0]
    la_all = (jnp.minimum(gx, 0.0) - jnp.log(1.0 + jnp.exp(-jnp.abs(gx)))) * (1.0 / GLA_TAU)
    hs = range(heads)
    qs_, ks_, vs_, las, bqs, ccs, tots = [], [], [], [], [], [], []
    for h in hs:
        la = la_all[:, h * dk:(h + 1) * dk]
        rs = _cumsums(cum, la)
        qs_.append(q_ref[0, :, h * dk:(h + 1) * dk] * scale)
        ks_.append(k_ref[0, :, h * dk:(h + 1) * dk])
        vs_.append(v_ref[0, :, h * dv:(h + 1) * dv])
        las.append(la)
        bqs.append(rs[:c])
        ccs.append(rs[c:2 * c])
        tots.append(rs[2 * c:2 * c + 1])
    sc = [None] * heads
    for li, s in enumerate(LEVELS):
        for h in hs:
            q, k = qs_[h], ks_[h]
            if s == 1:
                qd, kd = q * jnp.exp(las[h]), k
            else:
                g = bqs[h] - _boundary(ccs[h], s)
                qd = q * jnp.exp(jnp.minimum(g, 0.0))
                kd = k * jnp.exp(jnp.minimum(-g, 0.0))
            p = _nt(qd.astype(BF16), kd.astype(BF16)) * mask_ref[0, li]
            sc[h] = p if sc[h] is None else sc[h] + p
    for h in hs:
        q, k, v, bq, tot = qs_[h], ks_[h], vs_[h], bqs[h], tots[h]
        vb = v.astype(BF16)
        dg = jnp.sum(q * k, axis=1, keepdims=True)
        st = st_ref[h]
        o = (jnp.dot(sc[h].astype(BF16), vb, preferred_element_type=F32) + dg * v
             + _nt((q * jnp.exp(bq)).astype(BF16), st.astype(BF16)))
        o_ref[0, 0, :, h * dv:(h + 1) * dv] = o
        kbar = k * jnp.exp(tot - bq)
        st_ref[h] = st * jnp.exp(tot) + _tn(vb, kbar.astype(BF16))


def _ret_body(lam_ref, q_ref, k_ref, v_ref, cos_ref, sin_ref, o_ref, st_ref, *, heads, dk, dv):
    d = pl.program_id(0)
    cos, sin = cos_ref[...], sin_ref[...]

    def rope(x):
        return x * cos + pltpu.roll(x, dk // 2, axis=1) * sin

    @pl.when(pl.program_id(2) == 0)
    def _():
        st_ref[...] = jnp.zeros_like(st_ref)

    c = CHUNK
    ri = lax.broadcasted_iota(jnp.int32, (c, c), 0)
    ci = lax.broadcasted_iota(jnp.int32, (c, c), 1)
    dist = jnp.where(d == 0, ri - ci, ci - ri)
    live = dist >= 0
    distf = jnp.maximum(dist, 0).astype(F32)
    row = lax.broadcasted_iota(jnp.int32, (c, 1), 0)
    pos = jnp.where(d == 0, row + 1, c - row).astype(F32)
    for h in range(heads):
        lam = lam_ref[d * heads + h]
        q = rope(q_ref[0, :, h * dk:(h + 1) * dk])
        k = rope(k_ref[0, :, h * dk:(h + 1) * dk]) * dk ** -0.5
        v = v_ref[0, :, h * dv:(h + 1) * dv]
        vb = v.astype(BF16)
        decay = jnp.where(live, jnp.exp(lam * distf), 0.0)
        sc = _nt(q.astype(BF16), k.astype(BF16)) * decay
        st = st_ref[h]
        o = (jnp.dot(sc.astype(BF16), vb, preferred_element_type=F32)
             + _nt((q * jnp.exp(lam * pos)).astype(BF16), st.astype(BF16)))
        o_ref[0, 0, :, h * dv:(h + 1) * dv] = o
        kbar = k * jnp.exp(lam * (c - pos))
        st_ref[h] = st * jnp.exp(lam * c) + _tn(vb, kbar.astype(BF16))


def _chunk_index(d, c, n_ctx, n_all):
    back = jnp.where(c < n_ctx, n_ctx - 1 - c, n_ctx + n_all - 1 - c)
    return jnp.where(d == 0, c, back)


def _col_spec(width, col, ch):
    assert col % width == 0
    return pl.BlockSpec((1, CHUNK, width), lambda d, b, c, *_: (b, ch(d, c), col // width))


def gla_scan(z, a2, a_bias, n_ctx):
    bsz, t, _ = z.shape
    heads, dk, dv = GLA_HEADS, GLA_DK, GLA_DV
    n_all = t // CHUNK
    cum, masks = _scan_consts()
    lr_w = 128
    a2p = jnp.pad(a2, ((0, 0), (0, lr_w - GLA_RANK), (0, 0))).astype(BF16)
    ch = functools.partial(_chunk_index, n_ctx=n_ctx, n_all=n_all)
    body = functools.partial(_gla_body, heads=heads, dk=dk, dv=dv, scale=dk ** -0.5)
    return pl.pallas_call(
        body,
        out_shape=jax.ShapeDtypeStruct((2, bsz, t, heads * dv), F32),
        grid=(2, bsz, n_all),
        in_specs=[
            pl.BlockSpec((1,) + cum.shape[1:], lambda d, b, c: (d, 0, 0)),
            pl.BlockSpec((1,) + masks.shape[1:], lambda d, b, c: (d, 0, 0, 0)),
            _col_spec(heads * dk, COL_GLA_Q, ch),
            _col_spec(heads * dk, COL_GLA_K, ch),
            _col_spec(heads * dv, COL_GLA_V, ch),
            _col_spec(lr_w, COL_GLA_LR, ch),
            pl.BlockSpec((1, lr_w, heads * dk), lambda d, b, c: (d, 0, 0)),
            pl.BlockSpec((1, 1, heads * dk), lambda d, b, c: (d, 0, 0)),
        ],
        out_specs=pl.BlockSpec((1, 1, CHUNK, heads * dv), lambda d, b, c: (d, b, ch(d, c), 0)),
        scratch_shapes=[pltpu.VMEM((heads, dv, dk), F32)],
        compiler_params=_cparams(("parallel", "parallel", "arbitrary")),
        name="gla_scan",
    )(cum, masks, z, z, z, z, a2p, a_bias[:, None, :])


def ret_scan(z, lam, n_ctx):
    bsz, t, _ = z.shape
    heads, dk, dv = RET_HEADS, RET_DK, RET_DV
    n_all = t // CHUNK
    ch = functools.partial(_chunk_index, n_ctx=n_ctx, n_all=n_all)
    body = functools.partial(_ret_body, heads=heads, dk=dk, dv=dv)
    inv = ROPE_BASE ** (-jnp.linspace(0.0, 1.0, dk // 2, dtype=F32))
    ang = jnp.arange(t, dtype=F32)[:, None] * inv[None, :]
    cos_t = jnp.concatenate([jnp.cos(ang), jnp.cos(ang)], -1)
    sin_t = jnp.concatenate([-jnp.sin(ang), jnp.sin(ang)], -1)
    table = pl.BlockSpec((CHUNK, dk), lambda d, b, c, lam: (ch(d, c), 0))
    grid_spec = pltpu.PrefetchScalarGridSpec(
        num_scalar_prefetch=1,
        grid=(2, bsz, n_all),
        in_specs=[
            _col_spec(heads * dk, COL_RET_Q, ch),
            _col_spec(heads * dk, COL_RET_K, ch),
            _col_spec(heads * dv, COL_RET_V, ch),
            table, table,
        ],
        out_specs=pl.BlockSpec((1, 1, CHUNK, heads * dv), lambda d, b, c, lam: (d, b, ch(d, c), 0)),
        scratch_shapes=[pltpu.VMEM((heads, dv, dk), F32)],
    )
    return pl.pallas_call(
        body,
        out_shape=jax.ShapeDtypeStruct((2, bsz, t, heads * dv), F32),
        grid_spec=grid_spec,
        compiler_params=_cparams(("parallel", "parallel", "arbitrary")),
        name="ret_scan",
    )(lam.reshape(-1).astype(F32), z, z, z, cos_t, sin_t)


RWKV_GROUP = 4
RWKV_GW = RWKV_GROUP * RWKV_HEAD


def _rwkv_consts():
    c = CHUNK
    gw = RWKV_GW
    cum = np.zeros((2, 2 * c + 16, c), np.float32)
    tri = np.tril(np.ones((c, c), np.float32))
    mid = c // 2
    cum[0, :c] = tri - tri[mid - 1][None, :]
    cum[0, c:2 * c] = 1.0 - tri
    cum[0, 2 * c + 8:] = tri[mid - 1][None, :]
    cum[1, :c] = tri.T - tri.T[mid][None, :]
    cum[1, c:2 * c] = 1.0 - tri.T
    cum[1, 2 * c + 8:] = tri.T[mid][None, :]
    cum[:, 2 * c:2 * c + 8] = 1.0
    t = np.arange(c)[:, None]
    s = np.tile(np.arange(c), RWKV_GROUP)[None, :]
    masks = np.zeros((2, 3, c, gw), np.float32)
    masks[0, 0], masks[0, 1] = s < t, s <= t
    masks[1, 0], masks[1, 1] = s > t, s >= t
    masks[:, 2] = s == t
    return jnp.asarray(cum, BF16), jnp.asarray(masks)


def _rwkv_body(cum_ref, mask_ref, r_ref, k_ref, v_ref, kk_ref, kka_ref, lw_ref, o_ref, st_ref, *, groups):
    @pl.when(pl.program_id(2) == 0)
    def _():
        st_ref[...] = jnp.zeros_like(st_ref)

    c, gw = CHUNK, RWKV_GW
    cum = cum_ref[0]
    strict, incl, eye = mask_ref[0, 0], mask_ref[0, 1], mask_ref[0, 2]
    rb = lax.broadcasted_iota(jnp.int32, (gw, gw), 0) // RWKV_HEAD
    cb = lax.broadcasted_iota(jnp.int32, (gw, gw), 1) // RWKV_HEAD
    same_head = rb == cb

    def bd(x):
        xb = x.astype(BF16)
        return jnp.where(same_head, jnp.concatenate([xb] * RWKV_GROUP, axis=0), jnp.zeros((), BF16))

    def mm(a, b):
        return jnp.dot(a.astype(BF16), b, preferred_element_type=F32)

    gs = range(groups)
    sls = [slice(g * gw, (g + 1) * gw) for g in gs]
    pre = []
    for sl in sls:
        r, k, v = r_ref[0, :, sl], k_ref[0, :, sl], v_ref[0, :, sl]
        kk, kka, lw = kk_ref[0, :, sl], kka_ref[0, :, sl], lw_ref[0, 0, :, sl]
        rs = _cumsums(cum, lw)
        gm, ge, tot, gmid = rs[:c], rs[c:2 * c], rs[2 * c:2 * c + 1], rs[2 * c + 8:2 * c + 9]
        e_in, e_out, e_end = jnp.exp(gm), jnp.exp(-gm), jnp.exp(ge)
        kt_rel = kk * (e_in * jnp.exp(-lw))
        lhs = jnp.concatenate([kt_rel, r * e_in], axis=0)
        pre.append(dict(lhs=lhs.astype(BF16), lhs_abs=(lhs * jnp.exp(gmid)).astype(BF16),
                        kka_o=kka * e_out, k_o=k * e_out, v=v, tot=tot,
                        kv_end=jnp.concatenate([k * e_end, kka * e_end], axis=0).astype(BF16)))
    s_b = [_nt(q["lhs"], bd(q["kka_o"])) for q in pre]
    s_k = [_nt(q["lhs"], bd(q["k_o"])) for q in pre]
    ns = [-(s[:c] * strict) for s in s_b]
    xs = [eye + n for n in ns]
    ps = [mm(n, bd(n)) for n in ns]
    for _ in range(4):
        pps = [mm(jnp.concatenate([p, x], axis=0), bd(p)) for p, x in zip(ps, xs)]
        ps = [pp[:c] for pp in pps]
        xs = [x + pp[c:] for x, pp in zip(xs, pps)]
    xs = [x + mm(x, bd(p)) for p, x in zip(ps, xs)]
    sts = [st_ref[g] for g in gs]
    hs = [_nt(q["lhs_abs"], st.astype(BF16)) for q, st in zip(pre, sts)]
    vs = [mm(jnp.concatenate([sk[:c] * strict, sk[c:] * incl], axis=0), bd(q["v"])) for sk, q in zip(s_k, pre)]
    us = [mm(x, bd(h[:c] + w[:c])) for x, h, w in zip(xs, hs, vs)]
    for g in gs:
        o_ref[0, 0, :, sls[g]] = hs[g][c:] + vs[g][c:] - mm(s_b[g][c:] * incl, bd(us[g]))
    for g in gs:
        upd = _tn(jnp.concatenate([pre[g]["v"], -us[g]], axis=0).astype(BF16), pre[g]["kv_end"])
        st_ref[g] = sts[g] * jnp.exp(pre[g]["tot"]) + jnp.where(same_head, upd, 0.0)


def rwkv_scan(r, k, v, kk, kka, lw2, n_ctx):
    bsz, t, ch_w = r.shape
    groups = ch_w // RWKV_GW
    n_all = t // CHUNK
    cum, masks = _rwkv_consts()
    ch = functools.partial(_chunk_index, n_ctx=n_ctx, n_all=n_all)
    tok = pl.BlockSpec((1, CHUNK, ch_w), lambda d, b, c: (b, ch(d, c), 0))
    dir_tok = pl.BlockSpec((1, 1, CHUNK, ch_w), lambda d, b, c: (d, b, ch(d, c), 0))
    return pl.pallas_call(
        functools.partial(_rwkv_body, groups=groups),
        out_shape=jax.ShapeDtypeStruct((2, bsz, t, ch_w), F32),
        grid=(2, bsz, n_all),
        in_specs=[
            pl.BlockSpec((1,) + cum.shape[1:], lambda d, b, c: (d, 0, 0)),
            pl.BlockSpec((1,) + masks.shape[1:], lambda d, b, c: (d, 0, 0, 0)),
            tok, tok, tok, tok, tok, dir_tok,
        ],
        out_specs=dir_tok,
        scratch_shapes=[pltpu.VMEM((groups, RWKV_GW, RWKV_GW), F32)],
        compiler_params=_cparams(("parallel", "parallel", "arbitrary")),
        name="rwkv_scan",
    )(cum, masks, r, k, v, kk, kka, lw2)


def _split2(x):
    hi = x.astype(BF16)
    return hi, (x - hi.astype(F32)).astype(BF16)


ROUTER_LANES = 128


def _router_body(h_ref, w_ref, bias_ref, idx_ref, gate_ref, *, n_experts, top_k):
    a_hi, a_mid = _split2(h_ref[...])
    b_hi, b_mid = _split2(w_ref[...])
    dot = functools.partial(jnp.dot, preferred_element_type=F32)
    scores = jax.nn.sigmoid(dot(a_hi, b_hi) + (dot(a_hi, b_mid) + dot(a_mid, b_hi)))
    lane = lax.broadcasted_iota(jnp.int32, scores.shape, 1)
    cand = jnp.where(lane < n_experts, scores + bias_ref[...], -jnp.inf)
    idx_out = jnp.zeros(scores.shape, jnp.int32)
    val_out = jnp.zeros(scores.shape, F32)
    total = jnp.zeros((scores.shape[0], 1), F32)
    for j in range(top_k):
        best = jnp.max(cand, axis=1, keepdims=True)
        arg = jnp.min(jnp.where(cand == best, lane, ROUTER_LANES), axis=1, keepdims=True)
        pick = lane == arg
        val = jnp.sum(jnp.where(pick, scores, 0.0), axis=1, keepdims=True)
        cand = jnp.where(pick, -jnp.inf, cand)
        idx_out = jnp.where(lane == j, arg, idx_out)
        val_out = jnp.where(lane == j, val, val_out)
        total = total + val
    idx_ref[...] = idx_out
    gate_ref[...] = ROUTED_SCALE * val_out / total


def router(h, router_w, router_bias, layer):
    t, d = h.shape
    e = router_w.shape[-1]
    tm = _tile(t, 600)
    w = jnp.pad(router_w[layer].astype(F32), ((0, 0), (0, ROUTER_LANES - e)))
    bias = jnp.pad(router_bias[layer].astype(F32), (0, ROUTER_LANES - e))[None]
    idx, gates = pl.pallas_call(
        functools.partial(_router_body, n_experts=e, top_k=TOP_K),
        out_shape=(jax.ShapeDtypeStruct((t, ROUTER_LANES), jnp.int32), jax.ShapeDtypeStruct((t, ROUTER_LANES), F32)),
        grid=(t // tm,),
        in_specs=[pl.BlockSpec((tm, d), lambda i: (i, 0)), pl.BlockSpec((d, ROUTER_LANES), lambda i: (0, 0)),
                  pl.BlockSpec((1, ROUTER_LANES), lambda i: (0, 0))],
        out_specs=(pl.BlockSpec((tm, ROUTER_LANES), lambda i: (i, 0)), pl.BlockSpec((tm, ROUTER_LANES), lambda i: (i, 0))),
        compiler_params=_cparams(("parallel",)),
        name="router",
    )(h, w, bias)
    return idx[:, :TOP_K], gates[:, :TOP_K]


MOE_BM = 256


def _tile(m, cap, mult=16):
    return max(t for t in range(mult, cap + 1, mult) if m % t == 0)


def _pack_pairs(x):
    n = x.shape[-1] // 2
    bits = lax.bitcast_convert_type(x, jnp.uint32)
    bits = (bits + jnp.uint32(0x7FFF) + ((bits >> 16) & jnp.uint32(1))) & jnp.uint32(0xFFFF0000)
    return bits[..., :n] | (bits[..., n:] >> 16)


def _unpack_pairs(w):
    hi = lax.bitcast_convert_type(w & jnp.uint32(0xFFFF0000), F32)
    lo = lax.bitcast_convert_type(w << 16, F32)
    return jnp.concatenate([hi, lo], axis=-1)


def _swiglu(x, wg, wu, wd, gate=None):
    h1 = jnp.dot(x, wg, preferred_element_type=F32)
    h2 = jnp.dot(x, wu, preferred_element_type=F32)
    a = h1 * jax.nn.sigmoid(h1) * h2
    if gate is not None:
        a = a * gate
    return jnp.dot(a.astype(BF16), wd, preferred_element_type=F32)


def _cast_rows(src, dst, rows):
    def body(j, carry):
        r = pl.multiple_of(j * rows, rows)
        dst[pl.ds(r, rows), :] = src[pl.ds(r, rows), :].astype(BF16)
        return carry
    lax.fori_loop(0, src.shape[0] // rows, body, 0)


def _experts_body(be_ref, nx_ref, nv_ref, tok_ref, h_hbm, gate_ref, wg_hbm, wu_hbm, wd_hbm, o_ref,
                  xbuf0, xbuf1, st_g, st_u, st_d, wg_b, wu_b, wd_b, xsem, wsem, *, layer, bm):
    i = pl.program_id(0)
    n_live = nv_ref[0]
    slot = lax.rem(i, 2)

    def row_copy(base, r, buf, s):
        return pltpu.make_async_copy(h_hbm.at[pl.ds(tok_ref[base + r], 1)], buf.at[pl.ds(r, 1)], xsem.at[s])

    def rows_wait(buf, s):
        pltpu.make_async_copy(h_hbm.at[pl.ds(0, bm)], buf, xsem.at[s]).wait()

    def weight_copies(e):
        return (pltpu.make_async_copy(wg_hbm.at[layer, e], st_g, wsem.at[0]),
                pltpu.make_async_copy(wu_hbm.at[layer, e], st_u, wsem.at[1]),
                pltpu.make_async_copy(wd_hbm.at[layer, e], st_d, wsem.at[2]))

    @pl.when(i == 0)
    def _():
        def body(r, carry):
            row_copy(0, r, xbuf0, 0).start()
            return carry
        lax.fori_loop(0, bm, body, 0, unroll=8)
        for cp in weight_copies(be_ref[0]):
            cp.start(priority=1)

    @pl.when((i == 0) | (be_ref[i] != be_ref[jnp.maximum(i - 1, 0)]))
    def _():
        for cp in weight_copies(be_ref[i]):
            cp.wait()
        _cast_rows(st_g, wg_b, 256)
        _cast_rows(st_u, wu_b, 256)
        _cast_rows(st_d, wd_b, 32)

        @pl.when(nx_ref[i] >= 0)
        def _():
            for cp in weight_copies(nx_ref[i]):
                cp.start(priority=1)

    dot = functools.partial(jnp.dot, preferred_element_type=F32)
    half = xbuf0.shape[1]
    kc = 256
    n_k = half // kc
    n_issue = n_k // 2
    per = bm // n_issue

    def step(cur, cur_s, nxt, nxt_s):
        rows_wait(cur, cur_s)
        base = jnp.minimum(i + 1, n_live - 1) * bm
        h1 = jnp.zeros((bm, wg_b.shape[1]), F32)
        h2 = jnp.zeros((bm, wg_b.shape[1]), F32)
        for s in range(n_k):
            if s < n_issue:
                for r in range(s * per, (s + 1) * per):
                    row_copy(base, r, nxt, nxt_s).start()
            w = cur[:, s * kc:(s + 1) * kc]
            x_a = lax.bitcast_convert_type(w & jnp.uint32(0xFFFF0000), F32).astype(BF16)
            x_b = lax.bitcast_convert_type(w << 16, F32).astype(BF16)
            rows_a, rows_b = slice(s * kc, (s + 1) * kc), slice(half + s * kc, half + (s + 1) * kc)
            h1 = h1 + (dot(x_a, wg_b[rows_a, :]) + dot(x_b, wg_b[rows_b, :]))
            h2 = h2 + (dot(x_a, wu_b[rows_a, :]) + dot(x_b, wu_b[rows_b, :]))
        a = (h1 * jax.nn.sigmoid(h1) * h2 * gate_ref[...]).astype(BF16)
        for s in range(n_k):
            y = jnp.concatenate([dot(a, wd_b[:, s * kc:(s + 1) * kc]),
                                 dot(a, wd_b[:, half + s * kc:half + (s + 1) * kc])], axis=1)
            o_ref[:, s * kc:(s + 1) * kc] = _pack_pairs(y)

        @pl.when(i == n_live - 1)
        def _():
            rows_wait(nxt, nxt_s)

    @pl.when((i < n_live) & (slot == 0))
    def _():
        step(xbuf0, 0, xbuf1, 1)

    @pl.when((i < n_live) & (slot == 1))
    def _():
        step(xbuf1, 1, xbuf0, 0)

    @pl.when(i >= n_live)
    def _():
        o_ref[...] = jnp.zeros_like(o_ref)


def experts(h_pad, gate_rows, exp_gate, exp_up, exp_down, layer, block_expert, next_expert, n_live, buf_tok):
    bm = MOE_BM
    n = buf_tok.shape[0]
    d = exp_gate.shape[-2]
    f = exp_gate.shape[-1]
    dp = h_pad.shape[1]
    any_spec = pl.BlockSpec(memory_space=pl.ANY)
    grid_spec = pltpu.PrefetchScalarGridSpec(
        num_scalar_prefetch=4,
        grid=(n // bm,),
        in_specs=[any_spec, pl.BlockSpec((bm, 1), lambda i, *_: (i, 0)), any_spec, any_spec, any_spec],
        out_specs=pl.BlockSpec((bm, dp), lambda i, *_: (i, 0)),
        scratch_shapes=[
            pltpu.VMEM((bm, dp), jnp.uint32), pltpu.VMEM((bm, dp), jnp.uint32),
            pltpu.VMEM((d, f), F32), pltpu.VMEM((d, f), F32), pltpu.VMEM((f, d), F32),
            pltpu.VMEM((d, f), BF16), pltpu.VMEM((d, f), BF16), pltpu.VMEM((f, d), BF16),
            pltpu.SemaphoreType.DMA((2,)), pltpu.SemaphoreType.DMA((3,)),
        ],
    )
    return pl.pallas_call(
        functools.partial(_experts_body, layer=layer, bm=bm),
        out_shape=jax.ShapeDtypeStruct((n, dp), jnp.uint32),
        grid_spec=grid_spec,
        compiler_params=_cparams(("arbitrary",)),
        name="experts",
    )(block_expert, next_expert, n_live, buf_tok, h_pad, gate_rows, exp_gate, exp_up, exp_down)


def _shared_body(x_ref, wg_ref, wu_ref, wd_ref, o_ref):
    o_ref[...] = _swiglu(x_ref[...].astype(BF16), wg_ref[...].astype(BF16), wu_ref[...].astype(BF16),
                         wd_ref[...].astype(BF16))


def shared_expert(h, sh_gate, sh_up, sh_down, layer):
    t, d = h.shape
    f = sh_gate.shape[-1]
    tm = _tile(t, 300)
    once = pl.Buffered(1)
    return pl.pallas_call(
        _shared_body,
        out_shape=jax.ShapeDtypeStruct((t, d), F32),
        grid=(t // tm,),
        in_specs=[pl.BlockSpec((tm, d), lambda i: (i, 0)),
                  pl.BlockSpec((None, d, f), lambda i: (layer, 0, 0), pipeline_mode=once),
                  pl.BlockSpec((None, d, f), lambda i: (layer, 0, 0), pipeline_mode=once),
                  pl.BlockSpec((None, f, d), lambda i: (layer, 0, 0), pipeline_mode=once)],
        out_specs=pl.BlockSpec((tm, d), lambda i: (i, 0)),
        compiler_params=_cparams(("arbitrary",)),
        name="shared",
    )(h, sh_gate, sh_up, sh_down)


COMBINE_TB = 64


def _combine_body(pos_ref, y_hbm, sh_ref, o_ref, buf, sem, *, tb, k):
    i = pl.program_id(0)
    n = pl.num_programs(0)
    slot = lax.rem(i, 2)

    def gather(blk, s):
        base = blk * tb * k

        def body(t, carry):
            for kk in range(k):
                p = pos_ref[base + t * k + kk]
                pltpu.make_async_copy(y_hbm.at[pl.ds(p, 1)], buf.at[s, pl.ds(kk * tb + t, 1)],
                                      sem.at[s]).start(priority=kk % 2)
            return carry
        lax.fori_loop(0, tb, body, 0, unroll=4)

    @pl.when(i == 0)
    def _():
        gather(0, 0)

    @pl.when(i + 1 < n)
    def _():
        gather(i + 1, 1 - slot)

    pltpu.make_async_copy(y_hbm.at[pl.ds(0, tb * k)], buf.at[slot], sem.at[slot]).wait()
    acc = sh_ref[...]
    for kk in range(k):
        acc = acc + _unpack_pairs(buf[slot, kk * tb:(kk + 1) * tb, :])
    o_ref[...] = acc


def combine(y, pos, shared):
    t, k = pos.shape
    d = shared.shape[1]
    dp = y.shape[1]
    tb = COMBINE_TB
    grid_spec = pltpu.PrefetchScalarGridSpec(
        num_scalar_prefetch=1,
        grid=(t // tb,),
        in_specs=[pl.BlockSpec(memory_space=pl.ANY), pl.BlockSpec((tb, d), lambda i, *_: (i, 0))],
        out_specs=pl.BlockSpec((tb, d), lambda i, *_: (i, 0)),
        scratch_shapes=[pltpu.VMEM((2, tb * k, dp), jnp.uint32), pltpu.SemaphoreType.DMA((2,))],
    )
    return pl.pallas_call(
        functools.partial(_combine_body, tb=tb, k=k),
        out_shape=jax.ShapeDtypeStruct((t, d), F32),
        grid_spec=grid_spec,
        compiler_params=_cparams(("arbitrary",)),
        name="combine",
    )(pos.reshape(-1), y, shared)


def moe_ffn(h, hp, layer, router_w, router_bias, exp_gate, exp_up, exp_down, sh_gate, sh_up, sh_down):
    t, d = h.shape
    e, k, bm = N_EXPERTS, TOP_K, MOE_BM
    i32 = jnp.int32
    idx, gates = router(h, router_w, router_bias, layer)
    flat_e = idx.reshape(-1)
    iota = jnp.arange(t * k, dtype=i32)
    e_sorted, order = lax.sort_key_val(flat_e, iota)
    _, rank = lax.sort_key_val(order, iota)
    bounds = jnp.searchsorted(e_sorted, jnp.arange(e + 1, dtype=i32), side='left',
                              method='compare_all').astype(i32)
    first, counts = bounds[:-1], bounds[1:] - bounds[:-1]
    padded = (counts + bm - 1) // bm * bm
    ends = jnp.cumsum(padded)
    starts = ends - padded
    pos = (iota + (starts - first)[e_sorted])[rank].reshape(t, k)
    n_blocks = -(-(t * k) // bm) + e
    blk_start = jnp.arange(n_blocks, dtype=i32) * bm
    block_valid = (blk_start < ends[-1]).astype(i32)
    last_e = jnp.max(jnp.where(counts > 0, jnp.arange(e, dtype=i32), 0))
    block_expert = jnp.where(block_valid > 0,
                             jnp.minimum(jnp.searchsorted(ends, blk_start, side='right',
                                                          method='compare_all').astype(i32), e - 1), last_e)
    present = jnp.where(counts > 0, jnp.arange(e, dtype=i32), e)
    nxt = lax.cummin(jnp.concatenate([present[1:], jnp.full((1,), e, i32)]), reverse=True)
    next_expert = jnp.where(nxt < e, nxt, -1)[block_expert]
    blk_off = blk_start - starts[block_expert]
    blk_live = jnp.where(block_valid > 0, jnp.clip(counts[block_expert] - blk_off, 0, bm), 0)
    lane = jnp.arange(bm, dtype=i32)[None, :]
    live = (lane < blk_live[:, None]).reshape(-1)
    src = order[jnp.clip((first[block_expert] + blk_off)[:, None] + lane, 0, t * k - 1).reshape(-1)]
    buf_tok = jnp.where(live, src // k, 0).astype(i32)
    gate_rows = jnp.where(live, gates.reshape(-1)[src], 0.0)[:, None]

    n_live = (ends[-1] // bm).astype(i32).reshape(1)
    y = experts(hp, gate_rows, exp_gate, exp_up, exp_down, layer, block_expert, next_expert, n_live, buf_tok)
    return combine(y, pos, shared_expert(h, sh_gate, sh_up, sh_down, layer))


def _permute_cols(w):
    gla, ret, rw = w[..., :GLA_N], w[..., GLA_N:GLA_N + RET_N], w[..., GLA_N + RET_N:]
    rkv = 3 * RWKV_WIDTH
    lora = rkv + RWKV_DECAY_RANK + RWKV_ICLR_RANK

    def zeros(n):
        return jnp.zeros(w.shape[:-1] + (n,), w.dtype)

    parts = [gla[..., :GLA_N - GLA_RANK],
             rw[..., lora:], zeros(COL_RET_Q - COL_RW_G1 - RWKV_GATE_RANK),
             ret,
             rw[..., :lora],
             gla[..., GLA_N - GLA_RANK:], zeros(N_IN_PAD - COL_GLA_LR - GLA_RANK)]
    return jnp.concatenate(parts, -1)


RESNORM_TM = 128


def _resnorm_body(*refs, has_res, has_pre, outs):
    def rms(u):
        return u * lax.rsqrt(jnp.mean(u * u, axis=1, keepdims=True) + RMS_EPS)

    it = iter(refs)
    x = next(it)[0]
    if has_res:
        f, gate, post_w = next(it)[0], next(it)[0], next(it)[...]
        x = x + gate * (rms(f) * post_w)
    if has_pre:
        shift, scale, pre_w = next(it)[0], next(it)[0], next(it)[...]
        h = rms(x) * pre_w * (1.0 + scale) + shift
    for name in outs:
        out = next(it)
        if name == "x":
            out[0] = x
        elif name == "h32":
            out[0] = h
        elif name == "hb":
            out[0] = h.astype(BF16)
        else:
            out[0] = _pack_pairs(h)


def resnorm(xs, ctx_tiles, outs, res=None, pre=None, x_off=0):
    bsz, tx, d = xs.shape
    tm = RESNORM_TM
    t_out = tx - x_off * tm

    def mod_spec(col):
        return pl.BlockSpec((1, 1, d), lambda b, j: (jnp.where(j + x_off < ctx_tiles, bsz, b), 0, col))

    w_spec = pl.BlockSpec((1, d), lambda b, j: (0, 0))
    tok = pl.BlockSpec((1, tm, d), lambda b, j: (b, j, 0))
    args, in_specs = [xs], [pl.BlockSpec((1, tm, d), lambda b, j: (b, j + x_off, 0))]
    if res is not None:
        f, mod, col, w = res
        assert f.shape == (bsz, t_out, d)
        args += [f, mod, w[None]]
        in_specs += [tok, mod_spec(col), w_spec]
    if pre is not None:
        mod, shift_col, scale_col, w = pre
        args += [mod, mod, w[None]]
        in_specs += [mod_spec(shift_col), mod_spec(scale_col), w_spec]
    kinds = {"x": (d, F32), "h32": (d, F32), "hb": (d, BF16), "hp": (d // 2, jnp.uint32)}
    out_shape = tuple(jax.ShapeDtypeStruct((bsz, t_out, kinds[o][0]), kinds[o][1]) for o in outs)
    out_specs = tuple(pl.BlockSpec((1, tm, kinds[o][0]), lambda b, j: (b, j, 0)) for o in outs)
    return pl.pallas_call(
        functools.partial(_resnorm_body, has_res=res is not None, has_pre=pre is not None, outs=outs),
        out_shape=out_shape,
        grid=(bsz, t_out // tm),
        in_specs=in_specs,
        out_specs=out_specs,
        compiler_params=_cparams(("parallel", "arbitrary")),
        name="resnorm",
    )(*args)


def ret_mixer(z, decay_logit, n_ctx):
    return ret_scan(z, jax.nn.log_sigmoid(decay_logit.astype(F32)), n_ctx)


POST_TM = 128


def _mixer_post_body(og_ref, or_ref, ow_ref, gg_ref, gr_ref, gw_ref, bonus_ref, nw_ref, lw_ref, lb_ref,
                     seg_ref, segt_ref, y_ref):
    def silu(u):
        return u * jax.nn.sigmoid(u)

    def head_rms(x):
        return x * lax.rsqrt(jnp.mean(x * x, axis=1, keepdims=True) + RMS_EPS)

    og = og_ref[0, 0] + og_ref[1, 0]
    gg = gg_ref[0]
    for h in range(GLA_HEADS):
        sl = slice(h * GLA_DV, (h + 1) * GLA_DV)
        y_ref[0, :, sl] = (head_rms(og[:, sl]) * nw_ref[...] * silu(gg[:, sl])).astype(y_ref.dtype)
    orr = or_ref[0, 0] + or_ref[1, 0]
    gr = gr_ref[0]
    for h in range(RET_HEADS):
        sl = slice(h * RET_DV, (h + 1) * RET_DV)
        y_ref[0, :, GLA_W + h * RET_DV:GLA_W + (h + 1) * RET_DV] = (
            head_rms(orr[:, sl]) * silu(gr[:, sl])).astype(y_ref.dtype)
    dot = functools.partial(jnp.dot, preferred_element_type=F32)

    def head_mean(x):
        hi, mid = _split2(x)
        s = (dot(hi, seg_ref[...]) + dot(mid, seg_ref[...])) * (1.0 / RWKV_HEAD)
        hi, mid = _split2(s)
        return dot(hi, segt_ref[...]) + dot(mid, segt_ref[...])

    ow = ow_ref[0, 0] + ow_ref[1, 0]
    xc = ow - head_mean(ow)
    gn = xc * lax.rsqrt(head_mean(xc * xc) + GN_EPS) * lw_ref[...] + lb_ref[...]
    y_ref[0, :, GLA_W + RET_W:] = ((gn + bonus_ref[0]) * gw_ref[0]).astype(y_ref.dtype)


def mixer_post(z, og2, or2, ow2, g_rw, bonus, norm_w, lnx_w, lnx_b, out_from):
    bsz, t, _ = z.shape
    tm = POST_TM
    off = out_from // tm
    assert out_from % tm == 0 and t % tm == 0
    cw = RWKV_WIDTH
    seg = np.zeros((cw, 128), np.float32)
    seg[np.arange(cw), np.arange(cw) // RWKV_HEAD] = 1.0
    params = [norm_w[None], lnx_w[None], lnx_b[None], jnp.asarray(seg, BF16), jnp.asarray(seg.T, BF16)]

    def two(width):
        return pl.BlockSpec((2, 1, tm, width), lambda b, j: (0, b, j + off, 0))

    def tok(width, blk=0):
        return pl.BlockSpec((1, tm, width), lambda b, j: (b, j + off, blk))

    def whole(x):
        return pl.BlockSpec(x.shape, lambda b, j, nd=x.ndim: (0,) * nd)

    d = GLA_W + RET_W + cw
    return pl.pallas_call(
        _mixer_post_body,
        out_shape=jax.ShapeDtypeStruct((bsz, t - out_from, d), BF16),
        grid=(bsz, (t - out_from) // tm),
        in_specs=[two(GLA_W), two(RET_W), two(cw), tok(GLA_W, COL_GLA_G // GLA_W), tok(RET_W, COL_RET_G // RET_W),
                  tok(cw), tok(cw)] + [whole(x) for x in params],
        out_specs=pl.BlockSpec((1, tm, d), lambda b, j: (b, j, 0)),
        compiler_params=_cparams(("parallel", "arbitrary")),
        name="mixer_post",
    )(og2, or2, ow2, z, z, g_rw, bonus, *params)


RW_FIELDS = ((COL_RW_R, RWKV_WIDTH, 0), (COL_RW_K, RWKV_WIDTH, RWKV_WIDTH), (COL_RW_V, RWKV_WIDTH, 2 * RWKV_WIDTH),
             (COL_RW_W1, RWKV_DECAY_RANK + RWKV_ICLR_RANK, 3 * RWKV_WIDTH),
             (COL_RW_G1, COL_RET_Q - COL_RW_G1, 3 * RWKV_WIDTH + RWKV_DECAY_RANK + RWKV_ICLR_RANK))


def _rwkv_prep_body(*refs, n_ctx, n_all):
    (tabs, zs, (w0_ref, w2_ref, a0_ref, a2_ref, g2_ref, kk_ref_, ka_ref, rk_ref, seg_ref, segt_ref),
     (r_out, k_out, v_out, kk_out, kka_out, lw_out, g_out, bonus_out)) = (
        refs[0:5], refs[5:20], refs[20:30], refs[30:38])
    j = pl.program_id(1)
    c = CHUNK
    is_ctx = j < n_ctx
    one = lambda cond: jnp.where(cond, 1.0, 0.0).astype(F32)
    f_prev = one(is_ctx & (j > 0))
    f_next = one(is_ctx & (j < n_ctx - 1))
    f_up = one((j > n_ctx) & (~is_ctx))
    f_down = one((j < n_all - 1) & (~is_ctx))
    f_ctx = one(is_ctx)
    row = lax.broadcasted_iota(jnp.int32, (c, 1), 0)
    fields = []
    for f in range(5):
        prv, cur, nxt = zs[3 * f][0], zs[3 * f + 1][0], zs[3 * f + 2][0]
        tab = tabs[f]
        mu, m_l, m_r, m_u, m_d, m_p, m_n = (tab[i:i + 1] for i in range(7))
        before = jnp.where(row == 0, prv[c - 1:c] * f_prev, pltpu.roll(cur, 1, axis=0))
        after = jnp.where(row == c - 1, nxt[0:1] * f_next, pltpu.roll(cur, c - 1, axis=0))
        shifted = ((m_l + f_ctx * (m_p - m_l)) * before + (m_r + f_ctx * (m_n - m_r)) * after
                   + (m_u * f_up) * prv + (m_d * f_down) * nxt)
        fields.append(cur + mu * (shifted - cur))
    r, k, v, wa, xg = fields
    dot = functools.partial(jnp.dot, preferred_element_type=F32)
    txw = jnp.tanh(wa[:, :RWKV_DECAY_RANK]).astype(BF16)
    for d in range(2):
        lw_out[d, 0] = -jax.nn.sigmoid(w0_ref[d:d + 1] + dot(txw, w2_ref[d])) * float(np.exp(-0.5))
    a = jax.nn.sigmoid(a0_ref[...] + dot(wa[:, RWKV_DECAY_RANK:].astype(BF16), a2_ref[...]))
    g_out[0] = dot(jax.nn.sigmoid(xg).astype(BF16), g2_ref[...])

    def head_sum(x):
        hi, mid = _split2(x)
        s = dot(hi, seg_ref[...]) + dot(mid, seg_ref[...])
        hi, mid = _split2(s)
        return dot(hi, segt_ref[...]) + dot(mid, segt_ref[...])

    kraw = k * kk_ref_[...]
    kk = kraw * lax.rsqrt(head_sum(kraw * kraw) + 1e-12)
    k2 = k * (1.0 + (a - 1.0) * ka_ref[...])
    r_out[0], k_out[0], v_out[0], kk_out[0], kka_out[0] = r, k2, v, kk, kk * a
    bonus_out[0] = head_sum(r * k2 * rk_ref[...]) * v


def rwkv_prep(z, mu, w0, w2, a0, a2, g2, k_k, k_a, r_k, n_ctx):
    bsz, t, _ = z.shape
    cw = RWKV_WIDTH
    n_all = t // CHUNK
    assert CHUNK == GRID_W
    quarter, half = RWKV_N // 4, RWKV_N // 2
    tabs, z_specs = [], []
    for col, width, ch0 in RW_FIELDS:
        n_real = min(width, RWKV_N - ch0)
        ch = ch0 + np.arange(width)
        real = np.arange(width) < n_real
        tab = np.zeros((8, width), np.float32)
        tab[1], tab[2] = real & (ch < quarter), real & (ch >= quarter) & (ch < 2 * quarter)
        tab[3], tab[4] = real & (ch >= 2 * quarter) & (ch < 3 * quarter), real & (ch >= 3 * quarter)
        tab[5], tab[6] = real & (ch < half), real & (ch >= half)
        tab = jnp.asarray(tab).at[0, :n_real].set(mu[ch0:ch0 + n_real])
        tabs.append(tab)
        blk = col // width
        assert col % width == 0
        z_specs += [pl.BlockSpec((1, CHUNK, width), lambda b, j, blk=blk: (b, jnp.maximum(j - 1, 0), blk)),
                    pl.BlockSpec((1, CHUNK, width), lambda b, j, blk=blk: (b, j, blk)),
                    pl.BlockSpec((1, CHUNK, width), lambda b, j, blk=blk: (b, jnp.minimum(j + 1, n_all - 1), blk))]
    gpad = RW_FIELDS[4][1] - RWKV_GATE_RANK
    seg = np.zeros((cw, 128), np.float32)
    seg[np.arange(cw), np.arange(cw) // RWKV_HEAD] = 1.0
    params = [w0, w2.astype(BF16), a0[None], a2.astype(BF16), jnp.pad(g2, ((0, gpad), (0, 0))).astype(BF16),
              k_k[None], k_a[None], r_k.reshape(1, cw), jnp.asarray(seg, BF16), jnp.asarray(seg.T, BF16)]

    def whole(x):
        return pl.BlockSpec(x.shape, lambda b, j, nd=x.ndim: (0,) * nd)

    tok = pl.BlockSpec((1, CHUNK, cw), lambda b, j: (b, j, 0))
    tok_sds = jax.ShapeDtypeStruct((bsz, t, cw), F32)
    return pl.pallas_call(
        functools.partial(_rwkv_prep_body, n_ctx=n_ctx, n_all=n_all),
        out_shape=(tok_sds,) * 5 + (jax.ShapeDtypeStruct((2, bsz, t, cw), F32), tok_sds, tok_sds),
        grid=(bsz, n_all),
        in_specs=[whole(x) for x in tabs] + z_specs + [whole(x) for x in params],
        out_specs=(tok,) * 5 + (pl.BlockSpec((2, 1, CHUNK, cw), lambda b, j: (0, b, j, 0)), tok, tok),
        compiler_params=_cparams(("parallel", "arbitrary")),
        name="rwkv_prep",
    )(*tabs, *([z] * 15), *params)


def rwkv_mixer(z, mu, w0, w2, a0, a2, g2, k_k, k_a, r_k, n_ctx):
    r, k, v, kk, kka, lw2, g, bonus = rwkv_prep(z, mu, w0, w2, a0, a2, g2, k_k, k_a, r_k, n_ctx)
    return rwkv_scan(r, k, v, kk, kka, lw2, n_ctx), g, bonus


def kernel(x, c, ctx, c_ctx, ada_w, ada_b, pre_mix, post_mix, pre_ffn, post_ffn, w_in, w_out, gla_a2, gla_a_bias, gla_norm_w, ret_decay_logit, rwkv_mu, rwkv_w0, rwkv_w2, rwkv_a0, rwkv_a2, rwkv_g2, rwkv_k_k, rwkv_k_a, rwkv_r_k, rwkv_lnx_w, rwkv_lnx_b, router_w, router_bias, exp_gate, exp_up, exp_down, shared_gate, shared_up, shared_down):
    bsz, s, d = x.shape
    lc = ctx.shape[1]
    t = lc + s
    depth = ada_w.shape[0]
    n_ctx = lc // CHUNK
    ctx_tiles = lc // RESNORM_TM
    cvec = jnp.concatenate([jax.nn.silu(c), jax.nn.silu(c_ctx)[None], jnp.zeros((8 - bsz - 1, d), F32)], 0)
    mods = [(matmul(cvec, ada_w, tm=8, tn=1024, name="adaln", layer=l) + ada_b[l])[:, None, :] for l in range(depth)]
    xs = jnp.concatenate([ctx, x], 1)
    (hb,) = resnorm(xs, ctx_tiles, ("hb",), pre=(mods[0], 0, 1, pre_mix[0]))
    for l in range(depth):
        ctx_out = l < depth - 1
        out_from = 0 if ctx_out else lc
        wi = _permute_cols(w_in[l]).astype(BF16)
        z = matmul(hb.reshape(bsz * t, d), wi, tm=_tile(bsz * t, 1100), tn=1024,
                   name="w_in").reshape(bsz, t, N_IN_PAD)
        og2 = gla_scan(z, gla_a2[l], gla_a_bias[l], n_ctx)
        or2 = ret_mixer(z, ret_decay_logit[l], n_ctx)
        ow2, g_rw, bonus = rwkv_mixer(z, rwkv_mu[l], rwkv_w0[l], rwkv_w2[l], rwkv_a0[l], rwkv_a2[l],
                                      rwkv_g2[l], rwkv_k_k[l], rwkv_k_a[l], rwkv_r_k[l], n_ctx)
        y = mixer_post(z, og2, or2, ow2, g_rw, bonus, gla_norm_w[l], rwkv_lnx_w[l], rwkv_lnx_b[l], out_from)
        t_out = t - out_from
        y = matmul(y.reshape(bsz * t_out, d), w_out, tm=_tile(bsz * t_out, 1100), tn=512,
                   name="w_out", layer=l).reshape(bsz, t_out, d)
        xs, h32, hp = resnorm(xs, ctx_tiles, ("x", "h32", "hp"), res=(y, mods[l], 2, post_mix[l]),
                              pre=(mods[l], 3, 4, pre_ffn[l]), x_off=out_from // RESNORM_TM)
        if not ctx_out:
            ctx_tiles = 0
        f = moe_ffn(h32.reshape(-1, d), hp.reshape(-1, d // 2), l, router_w, router_bias, exp_gate, exp_up,
                    exp_down, shared_gate, shared_up, shared_down).reshape(xs.shape)
        if ctx_out:
            xs, hb = resnorm(xs, ctx_tiles, ("x", "hb"), res=(f, mods[l], 5, post_ffn[l]),
                             pre=(mods[l + 1], 0, 1, pre_mix[l + 1]))
        else:
            (xs,) = resnorm(xs, ctx_tiles, ("x",), res=(f, mods[l], 5, post_ffn[l]))
    return xs
```

```python
import functools

import numpy as np
import jax
import jax.numpy as jnp
from jax import lax
from jax.experimental import pallas as pl
from jax.experimental.pallas import tpu as pltpu

F32 = jnp.float32
BF16 = jnp.bfloat16

CHUNK = 64
LEVELS = (32, 16, 8, 4, 2, 1)

GLA_HEADS, GLA_DK, GLA_DV, GLA_RANK, GLA_TAU = 6, 128, 256, 16, 16.0
RET_HEADS, RET_DK, RET_DV = 5, 128, 256
ROPE_BASE = 10000.0
RWKV_HEADS, RWKV_HEAD = 20, 64
RWKV_WIDTH = RWKV_HEADS * RWKV_HEAD
RWKV_DECAY_RANK, RWKV_ICLR_RANK, RWKV_GATE_RANK = 128, 128, 480
GN_EPS = 64e-5
RMS_EPS = 1e-6
N_EXPERTS, TOP_K, D_EXPERT, ROUTED_SCALE = 64, 8, 384, 2.5
GRID_W = 64

GLA_QK, GLA_W = GLA_HEADS * GLA_DK, GLA_HEADS * GLA_DV
RET_QK, RET_W = RET_HEADS * RET_DK, RET_HEADS * RET_DV
GLA_N = 2 * GLA_QK + 2 * GLA_W + GLA_RANK
RET_N = 2 * RET_QK + 2 * RET_W
RWKV_N = 3 * RWKV_WIDTH + RWKV_DECAY_RANK + RWKV_ICLR_RANK + RWKV_GATE_RANK

COL_GLA_Q, COL_GLA_K, COL_GLA_V, COL_GLA_G = 0, 768, 1536, 3072
COL_RW_G1 = 4608
COL_RET_Q, COL_RET_K, COL_RET_V, COL_RET_G = 5120, 5760, 6400, 7680
COL_RW_R, COL_RW_K, COL_RW_V = 8960, 10240, 11520
COL_RW_W1, COL_RW_A1, COL_GLA_LR = 12800, 12928, 13056
N_IN_PAD = 13312

VMEM_LIMIT = 56 * 1024 * 1024


def _cparams(sem):
    return pltpu.CompilerParams(dimension_semantics=sem, vmem_limit_bytes=VMEM_LIMIT)


def _mm_body(a_ref, b_ref, o_ref):
    o_ref[...] = jnp.dot(a_ref[...].astype(BF16), b_ref[...].astype(BF16),
                         preferred_element_type=F32).astype(o_ref.dtype)


def matmul(a, b, *, tm, tn, out_dtype=F32, name="mm", layer=None):
    m, k = a.shape
    n = b.shape[-1]
    assert m % tm == 0 and n % tn == 0, (a.shape, b.shape, tm, tn)
    if layer is None:
        b_spec = pl.BlockSpec((k, tn), lambda i, j: (0, j))
    else:
        b_spec = pl.BlockSpec((None, k, tn), lambda i, j: (layer, 0, j))
    return pl.pallas_call(
        _mm_body,
        out_shape=jax.ShapeDtypeStruct((m, n), out_dtype),
        grid=(m // tm, n // tn),
        in_specs=[pl.BlockSpec((tm, k), lambda i, j: (i, 0)), b_spec],
        out_specs=pl.BlockSpec((tm, tn), lambda i, j: (i, j)),
        compiler_params=_cparams(("parallel", "arbitrary")),
        name=name,
    )(a, b)


def _scan_consts():
    c = CHUNK
    cum = np.zeros((2, 2 * c + 8, c), np.float32)
    masks = np.zeros((2, len(LEVELS), c, c), np.float32)
    tri = np.tril(np.ones((c, c), np.float32))
    cum[0, :c] = tri
    cum[0, c:2 * c] = tri
    cum[1, :c] = tri.T
    cum[1, c:2 * c] = np.triu(np.ones((c, c), np.float32), 1)
    cum[:, 2 * c:] = 1.0
    for li, s in enumerate(LEVELS):
        for i in range(c):
            for j in range(c):
                if (i & s) and not (j & s) and i // (2 * s) == j // (2 * s):
                    masks[0, li, i, j] = 1.0
        masks[1, li] = masks[0, li].T
    return jnp.asarray(cum, BF16), jnp.asarray(masks)


def _split3(x):
    hi = x.astype(BF16)
    r1 = x - hi.astype(F32)
    mid = r1.astype(BF16)
    lo = (r1 - mid.astype(F32)).astype(BF16)
    return hi, mid, lo


def _cumsums(cum, la):
    d = la.shape[1]
    r = jnp.dot(cum, jnp.concatenate(_split3(la), axis=1), preferred_element_type=F32)
    return r[:, :d] + r[:, d:2 * d] + r[:, 2 * d:]


def _boundary(cc, s):
    c, d = cc.shape
    if s >= 8:
        parts = [jnp.broadcast_to(cc[r0 + s - 1:r0 + s], (2 * s, d)) for r0 in range(0, c, 2 * s)]
        return parts[0] if len(parts) == 1 else jnp.concatenate(parts, axis=0)
    cc3 = cc.reshape(c // 8, 8, d)
    if s == 4:
        return jnp.broadcast_to(cc3[:, 3:4], cc3.shape).reshape(c, d)
    sub = lax.broadcasted_iota(jnp.int32, cc3.shape, 1)
    lo = jnp.broadcast_to(cc3[:, 1:2], cc3.shape)
    hi = jnp.broadcast_to(cc3[:, 5:6], cc3.shape)
    return jnp.where(sub < 4, lo, hi).reshape(c, d)


def _nt(a, b):
    return lax.dot_general(a, b, (((1,), (1,)), ((), ())), preferred_element_type=F32)


def _tn(a, b):
    return lax.dot_general(a, b, (((0,), (0,)), ((), ())), preferred_element_type=F32)


def _gla_body(cum_ref, mask_ref, q_ref, k_ref, v_ref, lr_ref, a2_ref, ab_ref, o_ref, st_ref, *,
              heads, dk, dv, scale):
    @pl.when(pl.program_id(2) == 0)
    def _():
        st_ref[...] = jnp.zeros_like(st_ref)

    cum = cum_ref[0]
    c = CHUNK
    gx = jnp.dot(lr_ref[0].astype(BF16), a2_ref[0], preferred_element_type=F32) + ab_ref[0]
    la_all = (jnp.minimum(gx, 0.0) - jnp.log(1.0 + jnp.exp(-jnp.abs(gx)))) * (1.0 / GLA_TAU)
    hs = range(heads)
    qs_, ks_, vs_, las, bqs, ccs, tots = [], [], [], [], [], [], []
    for h in hs:
        la = la_all[:, h * dk:(h + 1) * dk]
        rs = _cumsums(cum, la)
        qs_.append(q_ref[0, :, h * dk:(h + 1) * dk] * scale)
        ks_.append(k_ref[0, :, h * dk:(h + 1) * dk])
        vs_.append(v_ref[0, :, h * dv:(h + 1) * dv])
        las.append(la)
        bqs.append(rs[:c])
        ccs.append(rs[c:2 * c])
        tots.append(rs[2 * c:2 * c + 1])
    sc = [None] * heads
    for li, s in enumerate(LEVELS):
        for h in hs:
            q, k = qs_[h], ks_[h]
            if s == 1:
                qd, kd = q * jnp.exp(las[h]), k
            else:
                g = bqs[h] - _boundary(ccs[h], s)
                qd = q * jnp.exp(jnp.minimum(g, 0.0))
                kd = k * jnp.exp(jnp.minimum(-g, 0.0))
            p = _nt(qd.astype(BF16), kd.astype(BF16)) * mask_ref[0, li]
            sc[h] = p if sc[h] is None else sc[h] + p
    for h in hs:
        q, k, v, bq, tot = qs_[h], ks_[h], vs_[h], bqs[h], tots[h]
        vb = v.astype(BF16)
        dg = jnp.sum(q * k, axis=1, keepdims=True)
        st = st_ref[h]
        o = (jnp.dot(sc[h].astype(BF16), vb, preferred_element_type=F32) + dg * v
             + _nt((q * jnp.exp(bq)).astype(BF16), st.astype(BF16)))
        o_ref[0, 0, :, h * dv:(h + 1) * dv] = o
        kbar = k * jnp.exp(tot - bq)
        st_ref[h] = st * jnp.exp(tot) + _tn(vb, kbar.astype(BF16))


def _ret_body(lam_ref, q_ref, k_ref, v_ref, cos_ref, sin_ref, o_ref, st_ref, *, heads, dk, dv):
    d = pl.program_id(0)
    cos, sin = cos_ref[...], sin_ref[...]

    def rope(x):
        return x * cos + pltpu.roll(x, dk // 2, axis=1) * sin

    @pl.when(pl.program_id(2) == 0)
    def _():
        st_ref[...] = jnp.zeros_like(st_ref)

    c = CHUNK
    ri = lax.broadcasted_iota(jnp.int32, (c, c), 0)
    ci = lax.broadcasted_iota(jnp.int32, (c, c), 1)
    dist = jnp.where(d == 0, ri - ci, ci - ri)
    live = dist >= 0
    distf = jnp.maximum(dist, 0).astype(F32)
    row = lax.broadcasted_iota(jnp.int32, (c, 1), 0)
    pos = jnp.where(d == 0, row + 1, c - row).astype(F32)
    for h in range(heads):
        lam = lam_ref[d * heads + h]
        q = rope(q_ref[0, :, h * dk:(h + 1) * dk])
        k = rope(k_ref[0, :, h * dk:(h + 1) * dk]) * dk ** -0.5
        v = v_ref[0, :, h * dv:(h + 1) * dv]
        vb = v.astype(BF16)
        decay = jnp.where(live, jnp.exp(lam * distf), 0.0)
        sc = _nt(q.astype(BF16), k.astype(BF16)) * decay
        st = st_ref[h]
        o = (jnp.dot(sc.astype(BF16), vb, preferred_element_type=F32)
             + _nt((q * jnp.exp(lam * pos)).astype(BF16), st.astype(BF16)))
        o_ref[0, 0, :, h * dv:(h + 1) * dv] = o
        kbar = k * jnp.exp(lam * (c - pos))
        st_ref[h] = st * jnp.exp(lam * c) + _tn(vb, kbar.astype(BF16))


def _chunk_index(d, c, n_ctx, n_all):
    back = jnp.where(c < n_ctx, n_ctx - 1 - c, n_ctx + n_all - 1 - c)
    return jnp.where(d == 0, c, back)


def _col_spec(width, col, ch):
    assert col % width == 0
    return pl.BlockSpec((1, CHUNK, width), lambda d, b, c, *_: (b, ch(d, c), col // width))


def gla_scan(z, a2, a_bias, n_ctx):
    bsz, t, _ = z.shape
    heads, dk, dv = GLA_HEADS, GLA_DK, GLA_DV
    n_all = t // CHUNK
    cum, masks = _scan_consts()
    lr_w = 128
    a2p = jnp.pad(a2, ((0, 0), (0, lr_w - GLA_RANK), (0, 0))).astype(BF16)
    ch = functools.partial(_chunk_index, n_ctx=n_ctx, n_all=n_all)
    body = functools.partial(_gla_body, heads=heads, dk=dk, dv=dv, scale=dk ** -0.5)
    return pl.pallas_call(
        body,
        out_shape=jax.ShapeDtypeStruct((2, bsz, t, heads * dv), F32),
        grid=(2, bsz, n_all),
        in_specs=[
            pl.BlockSpec((1,) + cum.shape[1:], lambda d, b, c: (d, 0, 0)),
            pl.BlockSpec((1,) + masks.shape[1:], lambda d, b, c: (d, 0, 0, 0)),
            _col_spec(heads * dk, COL_GLA_Q, ch),
            _col_spec(heads * dk, COL_GLA_K, ch),
            _col_spec(heads * dv, COL_GLA_V, ch),
            _col_spec(lr_w, COL_GLA_LR, ch),
            pl.BlockSpec((1, lr_w, heads * dk), lambda d, b, c: (d, 0, 0)),
            pl.BlockSpec((1, 1, heads * dk), lambda d, b, c: (d, 0, 0)),
        ],
        out_specs=pl.BlockSpec((1, 1, CHUNK, heads * dv), lambda d, b, c: (d, b, ch(d, c), 0)),
        scratch_shapes=[pltpu.VMEM((heads, dv, dk), F32)],
        compiler_params=_cparams(("parallel", "parallel", "arbitrary")),
        name="gla_scan",
    )(cum, masks, z, z, z, z, a2p, a_bias[:, None, :])


def ret_scan(z, lam, n_ctx):
    bsz, t, _ = z.shape
    heads, dk, dv = RET_HEADS, RET_DK, RET_DV
    n_all = t // CHUNK
    ch = functools.partial(_chunk_index, n_ctx=n_ctx, n_all=n_all)
    body = functools.partial(_ret_body, heads=heads, dk=dk, dv=dv)
    inv = ROPE_BASE ** (-jnp.linspace(0.0, 1.0, dk // 2, dtype=F32))
    ang = jnp.arange(t, dtype=F32)[:, None] * inv[None, :]
    cos_t = jnp.concatenate([jnp.cos(ang), jnp.cos(ang)], -1)
    sin_t = jnp.concatenate([-jnp.sin(ang), jnp.sin(ang)], -1)
    table = pl.BlockSpec((CHUNK, dk), lambda d, b, c, lam: (ch(d, c), 0))
    grid_spec = pltpu.PrefetchScalarGridSpec(
        num_scalar_prefetch=1,
        grid=(2, bsz, n_all),
        in_specs=[
            _col_spec(heads * dk, COL_RET_Q, ch),
            _col_spec(heads * dk, COL_RET_K, ch),
            _col_spec(heads * dv, COL_RET_V, ch),
            table, table,
        ],
        out_specs=pl.BlockSpec((1, 1, CHUNK, heads * dv), lambda d, b, c, lam: (d, b, ch(d, c), 0)),
        scratch_shapes=[pltpu.VMEM((heads, dv, dk), F32)],
    )
    return pl.pallas_call(
        body,
        out_shape=jax.ShapeDtypeStruct((2, bsz, t, heads * dv), F32),
        grid_spec=grid_spec,
        compiler_params=_cparams(("parallel", "parallel", "arbitrary")),
        name="ret_scan",
    )(lam.reshape(-1).astype(F32), z, z, z, cos_t, sin_t)


RWKV_GROUP = 4
RWKV_GW = RWKV_GROUP * RWKV_HEAD


def _rwkv_consts():
    c = CHUNK
    gw = RWKV_GW
    cum = np.zeros((2, 2 * c + 16, c), np.float32)
    tri = np.tril(np.ones((c, c), np.float32))
    mid = c // 2
    cum[0, :c] = tri - tri[mid - 1][None, :]
    cum[0, c:2 * c] = 1.0 - tri
    cum[0, 2 * c + 8:] = tri[mid - 1][None, :]
    cum[1, :c] = tri.T - tri.T[mid][None, :]
    cum[1, c:2 * c] = 1.0 - tri.T
    cum[1, 2 * c + 8:] = tri.T[mid][None, :]
    cum[:, 2 * c:2 * c + 8] = 1.0
    t = np.arange(c)[:, None]
    s = np.tile(np.arange(c), RWKV_GROUP)[None, :]
    masks = np.zeros((2, 3, c, gw), np.float32)
    masks[0, 0], masks[0, 1] = s < t, s <= t
    masks[1, 0], masks[1, 1] = s > t, s >= t
    masks[:, 2] = s == t
    return jnp.asarray(cum, BF16), jnp.asarray(masks)


def _rwkv_body(cum_ref, mask_ref, r_ref, k_ref, v_ref, kk_ref, kka_ref, lw_ref, o_ref, st_ref, *, groups):
    @pl.when(pl.program_id(2) == 0)
    def _():
        st_ref[...] = jnp.zeros_like(st_ref)

    c, gw = CHUNK, RWKV_GW
    cum = cum_ref[0]
    strict, incl, eye = mask_ref[0, 0], mask_ref[0, 1], mask_ref[0, 2]
    rb = lax.broadcasted_iota(jnp.int32, (gw, gw), 0) // RWKV_HEAD
    cb = lax.broadcasted_iota(jnp.int32, (gw, gw), 1) // RWKV_HEAD
    same_head = rb == cb

    def bd(x):
        xb = x.astype(BF16)
        return jnp.where(same_head, jnp.concatenate([xb] * RWKV_GROUP, axis=0), jnp.zeros((), BF16))

    def mm(a, b):
        return jnp.dot(a.astype(BF16), b, preferred_element_type=F32)

    gs = range(groups)
    sls = [slice(g * gw, (g + 1) * gw) for g in gs]
    pre = []
    for sl in sls:
        r, k, v = r_ref[0, :, sl], k_ref[0, :, sl], v_ref[0, :, sl]
        kk, kka, lw = kk_ref[0, :, sl], kka_ref[0, :, sl], lw_ref[0, 0, :, sl]
        rs = _cumsums(cum, lw)
        gm, ge, tot, gmid = rs[:c], rs[c:2 * c], rs[2 * c:2 * c + 1], rs[2 * c + 8:2 * c + 9]
        e_in, e_out, e_end = jnp.exp(gm), jnp.exp(-gm), jnp.exp(ge)
        kt_rel = kk * (e_in * jnp.exp(-lw))
        lhs = jnp.concatenate([kt_rel, r * e_in], axis=0)
        pre.append(dict(lhs=lhs.astype(BF16), lhs_abs=(lhs * jnp.exp(gmid)).astype(BF16),
                        kka_o=kka * e_out, k_o=k * e_out, v=v, tot=tot,
                        kv_end=jnp.concatenate([k * e_end, kka * e_end], axis=0).astype(BF16)))
    s_b = [_nt(q["lhs"], bd(q["kka_o"])) for q in pre]
    s_k = [_nt(q["lhs"], bd(q["k_o"])) for q in pre]
    ns = [-(s[:c] * strict) for s in s_b]
    xs = [eye + n for n in ns]
    ps = [mm(n, bd(n)) for n in ns]
    for _ in range(4):
        pps = [mm(jnp.concatenate([p, x], axis=0), bd(p)) for p, x in zip(ps, xs)]
        ps = [pp[:c] for pp in pps]
        xs = [x + pp[c:] for x, pp in zip(xs, pps)]
    xs = [x + mm(x, bd(p)) for p, x in zip(ps, xs)]
    sts = [st_ref[g] for g in gs]
    hs = [_nt(q["lhs_abs"], st.astype(BF16)) for q, st in zip(pre, sts)]
    vs = [mm(jnp.concatenate([sk[:c] * strict, sk[c:] * incl], axis=0), bd(q["v"])) for sk, q in zip(s_k, pre)]
    us = [mm(x, bd(h[:c] + w[:c])) for x, h, w in zip(xs, hs, vs)]
    for g in gs:
        o_ref[0, 0, :, sls[g]] = hs[g][c:] + vs[g][c:] - mm(s_b[g][c:] * incl, bd(us[g]))
    for g in gs:
        upd = _tn(jnp.concatenate([pre[g]["v"], -us[g]], axis=0).astype(BF16), pre[g]["kv_end"])
        st_ref[g] = sts[g] * jnp.exp(pre[g]["tot"]) + jnp.where(same_head, upd, 0.0)


def rwkv_scan(r, k, v, kk, kka, lw2, n_ctx):
    bsz, t, ch_w = r.shape
    groups = ch_w // RWKV_GW
    n_all = t // CHUNK
    cum, masks = _rwkv_consts()
    ch = functools.partial(_chunk_index, n_ctx=n_ctx, n_all=n_all)
    tok = pl.BlockSpec((1, CHUNK, ch_w), lambda d, b, c: (b, ch(d, c), 0))
    dir_tok = pl.BlockSpec((1, 1, CHUNK, ch_w), lambda d, b, c: (d, b, ch(d, c), 0))
    return pl.pallas_call(
        functools.partial(_rwkv_body, groups=groups),
        out_shape=jax.ShapeDtypeStruct((2, bsz, t, ch_w), F32),
        grid=(2, bsz, n_all),
        in_specs=[
            pl.BlockSpec((1,) + cum.shape[1:], lambda d, b, c: (d, 0, 0)),
            pl.BlockSpec((1,) + masks.shape[1:], lambda d, b, c: (d, 0, 0, 0)),
            tok, tok, tok, tok, tok, dir_tok,
        ],
        out_specs=dir_tok,
        scratch_shapes=[pltpu.VMEM((groups, RWKV_GW, RWKV_GW), F32)],
        compiler_params=_cparams(("parallel", "parallel", "arbitrary")),
        name="rwkv_scan",
    )(cum, masks, r, k, v, kk, kka, lw2)


def _split2(x):
    hi = x.astype(BF16)
    return hi, (x - hi.astype(F32)).astype(BF16)


ROUTER_LANES = 128


def _router_body(h_ref, w_ref, bias_ref, idx_ref, gate_ref, *, n_experts, top_k):
    a_hi, a_mid = _split2(h_ref[...])
    b_hi, b_mid = _split2(w_ref[...])
    dot = functools.partial(jnp.dot, preferred_element_type=F32)
    scores = jax.nn.sigmoid(dot(a_hi, b_hi) + (dot(a_hi, b_mid) + dot(a_mid, b_hi)))
    lane = lax.broadcasted_iota(jnp.int32, scores.shape, 1)
    cand = jnp.where(lane < n_experts, scores + bias_ref[...], -jnp.inf)
    idx_out = jnp.zeros(scores.shape, jnp.int32)
    val_out = jnp.zeros(scores.shape, F32)
    total = jnp.zeros((scores.shape[0], 1), F32)
    for j in range(top_k):
        best = jnp.max(cand, axis=1, keepdims=True)
        arg = jnp.min(jnp.where(cand == best, lane, ROUTER_LANES), axis=1, keepdims=True)
        pick = lane == arg
        val = jnp.sum(jnp.where(pick, scores, 0.0), axis=1, keepdims=True)
        cand = jnp.where(pick, -jnp.inf, cand)
        idx_out = jnp.where(lane == j, arg, idx_out)
        val_out = jnp.where(lane == j, val, val_out)
        total = total + val
    idx_ref[...] = idx_out
    gate_ref[...] = ROUTED_SCALE * val_out / total


def router(h, router_w, router_bias, layer):
    t, d = h.shape
    e = router_w.shape[-1]
    tm = _tile(t, 600)
    w = jnp.pad(router_w[layer].astype(F32), ((0, 0), (0, ROUTER_LANES - e)))
    bias = jnp.pad(router_bias[layer].astype(F32), (0, ROUTER_LANES - e))[None]
    idx, gates = pl.pallas_call(
        functools.partial(_router_body, n_experts=e, top_k=TOP_K),
        out_shape=(jax.ShapeDtypeStruct((t, ROUTER_LANES), jnp.int32), jax.ShapeDtypeStruct((t, ROUTER_LANES), F32)),
        grid=(t // tm,),
        in_specs=[pl.BlockSpec((tm, d), lambda i: (i, 0)), pl.BlockSpec((d, ROUTER_LANES), lambda i: (0, 0)),
                  pl.BlockSpec((1, ROUTER_LANES), lambda i: (0, 0))],
        out_specs=(pl.BlockSpec((tm, ROUTER_LANES), lambda i: (i, 0)), pl.BlockSpec((tm, ROUTER_LANES), lambda i: (i, 0))),
        compiler_params=_cparams(("parallel",)),
        name="router",
    )(h, w, bias)
    return idx[:, :TOP_K], gates[:, :TOP_K]


MOE_BM = 256


def _tile(m, cap, mult=16):
    return max(t for t in range(mult, cap + 1, mult) if m % t == 0)


def _pack_pairs(x):
    n = x.shape[-1] // 2
    bits = lax.bitcast_convert_type(x, jnp.uint32)
    bits = (bits + jnp.uint32(0x7FFF) + ((bits >> 16) & jnp.uint32(1))) & jnp.uint32(0xFFFF0000)
    return bits[..., :n] | (bits[..., n:] >> 16)


def _unpack_pairs(w):
    hi = lax.bitcast_convert_type(w & jnp.uint32(0xFFFF0000), F32)
    lo = lax.bitcast_convert_type(w << 16, F32)
    return jnp.concatenate([hi, lo], axis=-1)


def _swiglu(x, wg, wu, wd, gate=None):
    h1 = jnp.dot(x, wg, preferred_element_type=F32)
    h2 = jnp.dot(x, wu, preferred_element_type=F32)
    a = h1 * jax.nn.sigmoid(h1) * h2
    if gate is not None:
        a = a * gate
    return jnp.dot(a.astype(BF16), wd, preferred_element_type=F32)


def _cast_rows(src, dst, rows):
    def body(j, carry):
        r = pl.multiple_of(j * rows, rows)
        dst[pl.ds(r, rows), :] = src[pl.ds(r, rows), :].astype(BF16)
        return carry
    lax.fori_loop(0, src.shape[0] // rows, body, 0)


def _experts_body(be_ref, nx_ref, nv_ref, tok_ref, h_hbm, gate_ref, wg_hbm, wu_hbm, wd_hbm, o_ref,
                  xbuf0, xbuf1, st_g, st_u, st_d, wg_b, wu_b, wd_b, xsem, wsem, *, layer, bm):
    i = pl.program_id(0)
    n_live = nv_ref[0]
    slot = lax.rem(i, 2)

    def row_copy(base, r, buf, s):
        return pltpu.make_async_copy(h_hbm.at[pl.ds(tok_ref[base + r], 1)], buf.at[pl.ds(r, 1)], xsem.at[s])

    def rows_wait(buf, s):
        pltpu.make_async_copy(h_hbm.at[pl.ds(0, bm)], buf, xsem.at[s]).wait()

    def weight_copies(e):
        return (pltpu.make_async_copy(wg_hbm.at[layer, e], st_g, wsem.at[0]),
                pltpu.make_async_copy(wu_hbm.at[layer, e], st_u, wsem.at[1]),
                pltpu.make_async_copy(wd_hbm.at[layer, e], st_d, wsem.at[2]))

    @pl.when(i == 0)
    def _():
        def body(r, carry):
            row_copy(0, r, xbuf0, 0).start()
            return carry
        lax.fori_loop(0, bm, body, 0, unroll=8)
        for cp in weight_copies(be_ref[0]):
            cp.start(priority=1)

    @pl.when((i == 0) | (be_ref[i] != be_ref[jnp.maximum(i - 1, 0)]))
    def _():
        for cp in weight_copies(be_ref[i]):
            cp.wait()
        _cast_rows(st_g, wg_b, 256)
        _cast_rows(st_u, wu_b, 256)
        _cast_rows(st_d, wd_b, 32)

        @pl.when(nx_ref[i] >= 0)
        def _():
            for cp in weight_copies(nx_ref[i]):
                cp.start(priority=1)

    dot = functools.partial(jnp.dot, preferred_element_type=F32)
    half = xbuf0.shape[1]
    kc = 256
    n_k = half // kc
    n_issue = n_k // 2
    per = bm // n_issue

    def step(cur, cur_s, nxt, nxt_s):
        rows_wait(cur, cur_s)
        base = jnp.minimum(i + 1, n_live - 1) * bm
        h1 = jnp.zeros((bm, wg_b.shape[1]), F32)
        h2 = jnp.zeros((bm, wg_b.shape[1]), F32)
        for s in range(n_k):
            if s < n_issue:
                for r in range(s * per, (s + 1) * per):
                    row_copy(base, r, nxt, nxt_s).start()
            w = cur[:, s * kc:(s + 1) * kc]
            x_a = lax.bitcast_convert_type(w & jnp.uint32(0xFFFF0000), F32).astype(BF16)
            x_b = lax.bitcast_convert_type(w << 16, F32).astype(BF16)
            rows_a, rows_b = slice(s * kc, (s + 1) * kc), slice(half + s * kc, half + (s + 1) * kc)
            h1 = h1 + (dot(x_a, wg_b[rows_a, :]) + dot(x_b, wg_b[rows_b, :]))
            h2 = h2 + (dot(x_a, wu_b[rows_a, :]) + dot(x_b, wu_b[rows_b, :]))
        a = (h1 * jax.nn.sigmoid(h1) * h2 * gate_ref[...]).astype(BF16)
        for s in range(n_k):
            y = jnp.concatenate([dot(a, wd_b[:, s * kc:(s + 1) * kc]),
                                 dot(a, wd_b[:, half + s * kc:half + (s + 1) * kc])], axis=1)
            o_ref[:, s * kc:(s + 1) * kc] = _pack_pairs(y)

        @pl.when(i == n_live - 1)
        def _():
            rows_wait(nxt, nxt_s)

    @pl.when((i < n_live) & (slot == 0))
    def _():
        step(xbuf0, 0, xbuf1, 1)

    @pl.when((i < n_live) & (slot == 1))
    def _():
        step(xbuf1, 1, xbuf0, 0)

    @pl.when(i >= n_live)
    def _():
        o_ref[...] = jnp.zeros_like(o_ref)


def experts(h_pad, gate_rows, exp_gate, exp_up, exp_down, layer, block_expert, next_expert, n_live, buf_tok):
    bm = MOE_BM
    n = buf_tok.shape[0]
    d = exp_gate.shape[-2]
    f = exp_gate.shape[-1]
    dp = h_pad.shape[1]
    any_spec = pl.BlockSpec(memory_space=pl.ANY)
    grid_spec = pltpu.PrefetchScalarGridSpec(
        num_scalar_prefetch=4,
        grid=(n // bm,),
        in_specs=[any_spec, pl.BlockSpec((bm, 1), lambda i, *_: (i, 0)), any_spec, any_spec, any_spec],
        out_specs=pl.BlockSpec((bm, dp), lambda i, *_: (i, 0)),
        scratch_shapes=[
            pltpu.VMEM((bm, dp), jnp.uint32), pltpu.VMEM((bm, dp), jnp.uint32),
            pltpu.VMEM((d, f), F32), pltpu.VMEM((d, f), F32), pltpu.VMEM((f, d), F32),
            pltpu.VMEM((d, f), BF16), pltpu.VMEM((d, f), BF16), pltpu.VMEM((f, d), BF16),
            pltpu.SemaphoreType.DMA((2,)), pltpu.SemaphoreType.DMA((3,)),
        ],
    )
    return pl.pallas_call(
        functools.partial(_experts_body, layer=layer, bm=bm),
        out_shape=jax.ShapeDtypeStruct((n, dp), jnp.uint32),
        grid_spec=grid_spec,
        compiler_params=_cparams(("arbitrary",)),
        name="experts",
    )(block_expert, next_expert, n_live, buf_tok, h_pad, gate_rows, exp_gate, exp_up, exp_down)


def _shared_body(x_ref, wg_ref, wu_ref, wd_ref, o_ref, wg_b, wu_b, wd_b):
    @pl.when(pl.program_id(0) == 0)
    def _():
        _cast_rows(wg_ref, wg_b, 256)
        _cast_rows(wu_ref, wu_b, 256)
        _cast_rows(wd_ref, wd_b, 32)

    o_ref[...] = _swiglu(x_ref[...].astype(BF16), wg_b[...], wu_b[...], wd_b[...])


def shared_expert(h, sh_gate, sh_up, sh_down, layer):
    t, d = h.shape
    f = sh_gate.shape[-1]
    tm = _tile(t, 300)
    once = pl.Buffered(1)
    return pl.pallas_call(
        _shared_body,
        out_shape=jax.ShapeDtypeStruct((t, d), F32),
        grid=(t // tm,),
        in_specs=[pl.BlockSpec((tm, d), lambda i: (i, 0)),
                  pl.BlockSpec((None, d, f), lambda i: (layer, 0, 0), pipeline_mode=once),
                  pl.BlockSpec((None, d, f), lambda i: (layer, 0, 0), pipeline_mode=once),
                  pl.BlockSpec((None, f, d), lambda i: (layer, 0, 0), pipeline_mode=once)],
        out_specs=pl.BlockSpec((tm, d), lambda i: (i, 0)),
        scratch_shapes=[pltpu.VMEM((d, f), BF16), pltpu.VMEM((d, f), BF16), pltpu.VMEM((f, d), BF16)],
        compiler_params=_cparams(("arbitrary",)),
        name="shared",
    )(h, sh_gate, sh_up, sh_down)


COMBINE_TB = 64


def _combine_body(pos_ref, y_hbm, sh_ref, o_ref, buf, sem, *, tb, k):
    i = pl.program_id(0)
    n = pl.num_programs(0)
    slot = lax.rem(i, 2)

    def gather(blk, s):
        base = blk * tb * k

        def body(t, carry):
            for kk in range(k):
                p = pos_ref[base + t * k + kk]
                pltpu.make_async_copy(y_hbm.at[pl.ds(p, 1)], buf.at[s, pl.ds(kk * tb + t, 1)],
                                      sem.at[s]).start(priority=kk % 2)
            return carry
        lax.fori_loop(0, tb, body, 0, unroll=4)

    @pl.when(i == 0)
    def _():
        gather(0, 0)

    @pl.when(i + 1 < n)
    def _():
        gather(i + 1, 1 - slot)

    pltpu.make_async_copy(y_hbm.at[pl.ds(0, tb * k)], buf.at[slot], sem.at[slot]).wait()
    acc = sh_ref[...]
    for kk in range(k):
        acc = acc + _unpack_pairs(buf[slot, kk * tb:(kk + 1) * tb, :])
    o_ref[...] = acc


def combine(y, pos, shared):
    t, k = pos.shape
    d = shared.shape[1]
    dp = y.shape[1]
    tb = COMBINE_TB
    grid_spec = pltpu.PrefetchScalarGridSpec(
        num_scalar_prefetch=1,
        grid=(t // tb,),
        in_specs=[pl.BlockSpec(memory_space=pl.ANY), pl.BlockSpec((tb, d), lambda i, *_: (i, 0))],
        out_specs=pl.BlockSpec((tb, d), lambda i, *_: (i, 0)),
        scratch_shapes=[pltpu.VMEM((2, tb * k, dp), jnp.uint32), pltpu.SemaphoreType.DMA((2,))],
    )
    return pl.pallas_call(
        functools.partial(_combine_body, tb=tb, k=k),
        out_shape=jax.ShapeDtypeStruct((t, d), F32),
        grid_spec=grid_spec,
        compiler_params=_cparams(("arbitrary",)),
        name="combine",
    )(pos.reshape(-1), y, shared)


def moe_ffn(h, hp, layer, router_w, router_bias, exp_gate, exp_up, exp_down, sh_gate, sh_up, sh_down):
    t, d = h.shape
    e, k, bm = N_EXPERTS, TOP_K, MOE_BM
    i32 = jnp.int32
    idx, gates = router(h, router_w, router_bias, layer)
    flat_e = idx.reshape(-1)
    iota = jnp.arange(t * k, dtype=i32)
    e_sorted, order = lax.sort_key_val(flat_e, iota)
    _, rank = lax.sort_key_val(order, iota)
    bounds = jnp.searchsorted(e_sorted, jnp.arange(e + 1, dtype=i32), side='left',
                              method='compare_all').astype(i32)
    first, counts = bounds[:-1], bounds[1:] - bounds[:-1]
    padded = (counts + bm - 1) // bm * bm
    ends = jnp.cumsum(padded)
    starts = ends - padded
    pos = (iota + (starts - first)[e_sorted])[rank].reshape(t, k)
    n_blocks = -(-(t * k) // bm) + e
    blk_start = jnp.arange(n_blocks, dtype=i32) * bm
    block_valid = (blk_start < ends[-1]).astype(i32)
    last_e = jnp.max(jnp.where(counts > 0, jnp.arange(e, dtype=i32), 0))
    block_expert = jnp.where(block_valid > 0,
                             jnp.minimum(jnp.searchsorted(ends, blk_start, side='right',
                                                          method='compare_all').astype(i32), e - 1), last_e)
    present = jnp.where(counts > 0, jnp.arange(e, dtype=i32), e)
    nxt = lax.cummin(jnp.concatenate([present[1:], jnp.full((1,), e, i32)]), reverse=True)
    next_expert = jnp.where(nxt < e, nxt, -1)[block_expert]
    blk_off = blk_start - starts[block_expert]
    blk_live = jnp.where(block_valid > 0, jnp.clip(counts[block_expert] - blk_off, 0, bm), 0)
    lane = jnp.arange(bm, dtype=i32)[None, :]
    live = (lane < blk_live[:, None]).reshape(-1)
    src = order[jnp.clip((first[block_expert] + blk_off)[:, None] + lane, 0, t * k - 1).reshape(-1)]
    buf_tok = jnp.where(live, src // k, jnp.arange(n_blocks * bm, dtype=i32) % t).astype(i32)
    gate_rows = jnp.where(live, gates.reshape(-1)[src], 0.0)[:, None]

    n_live = (ends[-1] // bm).astype(i32).reshape(1)
    y = experts(hp, gate_rows, exp_gate, exp_up, exp_down, layer, block_expert, next_expert, n_live, buf_tok)
    return combine(y, pos, shared_expert(h, sh_gate, sh_up, sh_down, layer))


def _permute_cols(w):
    gla, ret, rw = w[..., :GLA_N], w[..., GLA_N:GLA_N + RET_N], w[..., GLA_N + RET_N:]
    rkv = 3 * RWKV_WIDTH
    lora = rkv + RWKV_DECAY_RANK + RWKV_ICLR_RANK

    def zeros(n):
        return jnp.zeros(w.shape[:-1] + (n,), w.dtype)

    parts = [gla[..., :GLA_N - GLA_RANK],
             rw[..., lora:], zeros(COL_RET_Q - COL_RW_G1 - RWKV_GATE_RANK),
             ret,
             rw[..., :lora],
             gla[..., GLA_N - GLA_RANK:], zeros(N_IN_PAD - COL_GLA_LR - GLA_RANK)]
    return jnp.concatenate(parts, -1)


RESNORM_TM = 128


def _resnorm_body(*refs, has_res, has_pre, outs):
    def rms(u):
        return u * lax.rsqrt(jnp.mean(u * u, axis=1, keepdims=True) + RMS_EPS)

    it = iter(refs)
    x = next(it)[0]
    if has_res:
        f, gate, post_w = next(it)[0], next(it)[0], next(it)[...]
        x = x + gate * (rms(f) * post_w)
    if has_pre:
        shift, scale, pre_w = next(it)[0], next(it)[0], next(it)[...]
        h = rms(x) * pre_w * (1.0 + scale) + shift
    for name in outs:
        out = next(it)
        if name == "x":
            out[0] = x
        elif name == "h32":
            out[0] = h
        elif name == "hb":
            out[0] = h.astype(BF16)
        else:
            out[0] = _pack_pairs(h)


def resnorm(xs, ctx_tiles, outs, res=None, pre=None, x_off=0):
    bsz, tx, d = xs.shape
    tm = RESNORM_TM
    t_out = tx - x_off * tm

    def mod_spec(col):
        return pl.BlockSpec((1, 1, d), lambda b, j: (jnp.where(j + x_off < ctx_tiles, bsz, b), 0, col))

    w_spec = pl.BlockSpec((1, d), lambda b, j: (0, 0))
    tok = pl.BlockSpec((1, tm, d), lambda b, j: (b, j, 0))
    args, in_specs = [xs], [pl.BlockSpec((1, tm, d), lambda b, j: (b, j + x_off, 0))]
    if res is not None:
        f, mod, col, w = res
        assert f.shape == (bsz, t_out, d)
        args += [f, mod, w[None]]
        in_specs += [tok, mod_spec(col), w_spec]
    if pre is not None:
        mod, shift_col, scale_col, w = pre
        args += [mod, mod, w[None]]
        in_specs += [mod_spec(shift_col), mod_spec(scale_col), w_spec]
    kinds = {"x": (d, F32), "h32": (d, F32), "hb": (d, BF16), "hp": (d // 2, jnp.uint32)}
    out_shape = tuple(jax.ShapeDtypeStruct((bsz, t_out, kinds[o][0]), kinds[o][1]) for o in outs)
    out_specs = tuple(pl.BlockSpec((1, tm, kinds[o][0]), lambda b, j: (b, j, 0)) for o in outs)
    return pl.pallas_call(
        functools.partial(_resnorm_body, has_res=res is not None, has_pre=pre is not None, outs=outs),
        out_shape=out_shape,
        grid=(bsz, t_out // tm),
        in_specs=in_specs,
        out_specs=out_specs,
        compiler_params=_cparams(("parallel", "arbitrary")),
        name="resnorm",
    )(*args)


def ret_mixer(z, decay_logit, n_ctx):
    return ret_scan(z, jax.nn.log_sigmoid(decay_logit.astype(F32)), n_ctx)


POST_TM = 128


def _mixer_post_body(og_ref, or_ref, ow_ref, gg_ref, gr_ref, gw_ref, bonus_ref, nw_ref, lw_ref, lb_ref,
                     seg_ref, segt_ref, y_ref):
    def silu(u):
        return u * jax.nn.sigmoid(u)

    def head_rms(x):
        return x * lax.rsqrt(jnp.mean(x * x, axis=1, keepdims=True) + RMS_EPS)

    og = og_ref[0, 0] + og_ref[1, 0]
    gg = gg_ref[0]
    for h in range(GLA_HEADS):
        sl = slice(h * GLA_DV, (h + 1) * GLA_DV)
        y_ref[0, :, sl] = (head_rms(og[:, sl]) * nw_ref[...] * silu(gg[:, sl])).astype(y_ref.dtype)
    orr = or_ref[0, 0] + or_ref[1, 0]
    gr = gr_ref[0]
    for h in range(RET_HEADS):
        sl = slice(h * RET_DV, (h + 1) * RET_DV)
        y_ref[0, :, GLA_W + h * RET_DV:GLA_W + (h + 1) * RET_DV] = (
            head_rms(orr[:, sl]) * silu(gr[:, sl])).astype(y_ref.dtype)
    dot = functools.partial(jnp.dot, preferred_element_type=F32)

    def head_mean(x):
        hi, mid = _split2(x)
        s = (dot(hi, seg_ref[...]) + dot(mid, seg_ref[...])) * (1.0 / RWKV_HEAD)
        hi, mid = _split2(s)
        return dot(hi, segt_ref[...]) + dot(mid, segt_ref[...])

    ow = ow_ref[0, 0] + ow_ref[1, 0]
    xc = ow - head_mean(ow)
    gn = xc * lax.rsqrt(head_mean(xc * xc) + GN_EPS) * lw_ref[...] + lb_ref[...]
    y_ref[0, :, GLA_W + RET_W:] = ((gn + bonus_ref[0]) * gw_ref[0]).astype(y_ref.dtype)


def mixer_post(z, og2, or2, ow2, g_rw, bonus, norm_w, lnx_w, lnx_b, out_from):
    bsz, t, _ = z.shape
    tm = POST_TM
    off = out_from // tm
    assert out_from % tm == 0 and t % tm == 0
    cw = RWKV_WIDTH
    seg = np.zeros((cw, 128), np.float32)
    seg[np.arange(cw), np.arange(cw) // RWKV_HEAD] = 1.0
    params = [norm_w[None], lnx_w[None], lnx_b[None], jnp.asarray(seg, BF16), jnp.asarray(seg.T, BF16)]

    def two(width):
        return pl.BlockSpec((2, 1, tm, width), lambda b, j: (0, b, j + off, 0))

    def tok(width, blk=0):
        return pl.BlockSpec((1, tm, width), lambda b, j: (b, j + off, blk))

    def whole(x):
        return pl.BlockSpec(x.shape, lambda b, j, nd=x.ndim: (0,) * nd)

    d = GLA_W + RET_W + cw
    return pl.pallas_call(
        _mixer_post_body,
        out_shape=jax.ShapeDtypeStruct((bsz, t - out_from, d), BF16),
        grid=(bsz, (t - out_from) // tm),
        in_specs=[two(GLA_W), two(RET_W), two(cw), tok(GLA_W, COL_GLA_G // GLA_W), tok(RET_W, COL_RET_G // RET_W),
                  tok(cw), tok(cw)] + [whole(x) for x in params],
        out_specs=pl.BlockSpec((1, tm, d), lambda b, j: (b, j, 0)),
        compiler_params=_cparams(("parallel", "arbitrary")),
        name="mixer_post",
    )(og2, or2, ow2, z, z, g_rw, bonus, *params)


RW_FIELDS = ((COL_RW_R, RWKV_WIDTH, 0), (COL_RW_K, RWKV_WIDTH, RWKV_WIDTH), (COL_RW_V, RWKV_WIDTH, 2 * RWKV_WIDTH),
             (COL_RW_W1, RWKV_DECAY_RANK + RWKV_ICLR_RANK, 3 * RWKV_WIDTH),
             (COL_RW_G1, COL_RET_Q - COL_RW_G1, 3 * RWKV_WIDTH + RWKV_DECAY_RANK + RWKV_ICLR_RANK))


def _rwkv_prep_body(*refs, n_ctx, n_all):
    (tabs, zs, (w0_ref, w2_ref, a0_ref, a2_ref, g2_ref, kk_ref_, ka_ref, rk_ref, seg_ref, segt_ref),
     (r_out, k_out, v_out, kk_out, kka_out, lw_out, g_out, bonus_out)) = (
        refs[0:5], refs[5:20], refs[20:30], refs[30:38])
    j = pl.program_id(1)
    c = CHUNK
    is_ctx = j < n_ctx
    one = lambda cond: jnp.where(cond, 1.0, 0.0).astype(F32)
    f_prev = one(is_ctx & (j > 0))
    f_next = one(is_ctx & (j < n_ctx - 1))
    f_up = one((j > n_ctx) & (~is_ctx))
    f_down = one((j < n_all - 1) & (~is_ctx))
    f_ctx = one(is_ctx)
    row = lax.broadcasted_iota(jnp.int32, (c, 1), 0)
    fields = []
    for f in range(5):
        prv, cur, nxt = zs[3 * f][0], zs[3 * f + 1][0], zs[3 * f + 2][0]
        tab = tabs[f]
        mu, m_l, m_r, m_u, m_d, m_p, m_n = (tab[i:i + 1] for i in range(7))
        before = jnp.where(row == 0, prv[c - 1:c] * f_prev, pltpu.roll(cur, 1, axis=0))
        after = jnp.where(row == c - 1, nxt[0:1] * f_next, pltpu.roll(cur, c - 1, axis=0))
        shifted = ((m_l + f_ctx * (m_p - m_l)) * before + (m_r + f_ctx * (m_n - m_r)) * after
                   + (m_u * f_up) * prv + (m_d * f_down) * nxt)
        fields.append(cur + mu * (shifted - cur))
    r, k, v, wa, xg = fields
    dot = functools.partial(jnp.dot, preferred_element_type=F32)
    txw = jnp.tanh(wa[:, :RWKV_DECAY_RANK]).astype(BF16)
    for d in range(2):
        lw_out[d, 0] = -jax.nn.sigmoid(w0_ref[d:d + 1] + dot(txw, w2_ref[d])) * float(np.exp(-0.5))
    a = jax.nn.sigmoid(a0_ref[...] + dot(wa[:, RWKV_DECAY_RANK:].astype(BF16), a2_ref[...]))
    g_out[0] = dot(jax.nn.sigmoid(xg).astype(BF16), g2_ref[...])

    def head_sum(x):
        hi, mid = _split2(x)
        s = dot(hi, seg_ref[...]) + dot(mid, seg_ref[...])
        hi, mid = _split2(s)
        return dot(hi, segt_ref[...]) + dot(mid, segt_ref[...])

    kraw = k * kk_ref_[...]
    kk = kraw * lax.rsqrt(head_sum(kraw * kraw) + 1e-12)
    k2 = k * (1.0 + (a - 1.0) * ka_ref[...])
    r_out[0], k_out[0], v_out[0], kk_out[0], kka_out[0] = r, k2, v, kk, kk * a
    bonus_out[0] = head_sum(r * k2 * rk_ref[...]) * v


def rwkv_prep(z, mu, w0, w2, a0, a2, g2, k_k, k_a, r_k, n_ctx):
    bsz, t, _ = z.shape
    cw = RWKV_WIDTH
    n_all = t // CHUNK
    assert CHUNK == GRID_W
    quarter, half = RWKV_N // 4, RWKV_N // 2
    tabs, z_specs = [], []
    for col, width, ch0 in RW_FIELDS:
        n_real = min(width, RWKV_N - ch0)
        ch = ch0 + np.arange(width)
        real = np.arange(width) < n_real
        tab = np.zeros((8, width), np.float32)
        tab[1], tab[2] = real & (ch < quarter), real & (ch >= quarter) & (ch < 2 * quarter)
        tab[3], tab[4] = real & (ch >= 2 * quarter) & (ch < 3 * quarter), real & (ch >= 3 * quarter)
        tab[5], tab[6] = real & (ch < half), real & (ch >= half)
        tab = jnp.asarray(tab).at[0, :n_real].set(mu[ch0:ch0 + n_real])
        tabs.append(tab)
        blk = col // width
        assert col % width == 0
        z_specs += [pl.BlockSpec((1, CHUNK, width), lambda b, j, blk=blk: (b, jnp.maximum(j - 1, 0), blk)),
                    pl.BlockSpec((1, CHUNK, width), lambda b, j, blk=blk: (b, j, blk)),
                    pl.BlockSpec((1, CHUNK, width), lambda b, j, blk=blk: (b, jnp.minimum(j + 1, n_all - 1), blk))]
    gpad = RW_FIELDS[4][1] - RWKV_GATE_RANK
    seg = np.zeros((cw, 128), np.float32)
    seg[np.arange(cw), np.arange(cw) // RWKV_HEAD] = 1.0
    params = [w0, w2.astype(BF16), a0[None], a2.astype(BF16), jnp.pad(g2, ((0, gpad), (0, 0))).astype(BF16),
              k_k[None], k_a[None], r_k.reshape(1, cw), jnp.asarray(seg, BF16), jnp.asarray(seg.T, BF16)]

    def whole(x):
        return pl.BlockSpec(x.shape, lambda b, j, nd=x.ndim: (0,) * nd)

    tok = pl.BlockSpec((1, CHUNK, cw), lambda b, j: (b, j, 0))
    tok_sds = jax.ShapeDtypeStruct((bsz, t, cw), F32)
    return pl.pallas_call(
        functools.partial(_rwkv_prep_body, n_ctx=n_ctx, n_all=n_all),
        out_shape=(tok_sds,) * 5 + (jax.ShapeDtypeStruct((2, bsz, t, cw), F32), tok_sds, tok_sds),
        grid=(bsz, n_all),
        in_specs=[whole(x) for x in tabs] + z_specs + [whole(x) for x in params],
        out_specs=(tok,) * 5 + (pl.BlockSpec((2, 1, CHUNK, cw), lambda b, j: (0, b, j, 0)), tok, tok),
        compiler_params=_cparams(("parallel", "arbitrary")),
        name="rwkv_prep",
    )(*tabs, *([z] * 15), *params)


def rwkv_mixer(z, mu, w0, w2, a0, a2, g2, k_k, k_a, r_k, n_ctx):
    r, k, v, kk, kka, lw2, g, bonus = rwkv_prep(z, mu, w0, w2, a0, a2, g2, k_k, k_a, r_k, n_ctx)
    return rwkv_scan(r, k, v, kk, kka, lw2, n_ctx), g, bonus


def kernel(x, c, ctx, c_ctx, ada_w, ada_b, pre_mix, post_mix, pre_ffn, post_ffn, w_in, w_out, gla_a2, gla_a_bias, gla_norm_w, ret_decay_logit, rwkv_mu, rwkv_w0, rwkv_w2, rwkv_a0, rwkv_a2, rwkv_g2, rwkv_k_k, rwkv_k_a, rwkv_r_k, rwkv_lnx_w, rwkv_lnx_b, router_w, router_bias, exp_gate, exp_up, exp_down, shared_gate, shared_up, shared_down):
    bsz, s, d = x.shape
    lc = ctx.shape[1]
    t = lc + s
    depth = ada_w.shape[0]
    n_ctx = lc // CHUNK
    ctx_tiles = lc // RESNORM_TM
    cvec = jnp.concatenate([jax.nn.silu(c), jax.nn.silu(c_ctx)[None], jnp.zeros((8 - bsz - 1, d), F32)], 0)
    mods = [(matmul(cvec, ada_w, tm=8, tn=1024, name="adaln", layer=l) + ada_b[l])[:, None, :] for l in range(depth)]
    xs = jnp.concatenate([ctx, x], 1)
    (hb,) = resnorm(xs, ctx_tiles, ("hb",), pre=(mods[0], 0, 1, pre_mix[0]))
    for l in range(depth):
        ctx_out = l < depth - 1
        out_from = 0 if ctx_out else lc
        wi = _permute_cols(w_in[l]).astype(BF16)
        z = matmul(hb.reshape(bsz * t, d), wi, tm=_tile(bsz * t, 1100), tn=1024,
                   name="w_in").reshape(bsz, t, N_IN_PAD)
        og2 = gla_scan(z, gla_a2[l], gla_a_bias[l], n_ctx)
        or2 = ret_mixer(z, ret_decay_logit[l], n_ctx)
        ow2, g_rw, bonus = rwkv_mixer(z, rwkv_mu[l], rwkv_w0[l], rwkv_w2[l], rwkv_a0[l], rwkv_a2[l],
                                      rwkv_g2[l], rwkv_k_k[l], rwkv_k_a[l], rwkv_r_k[l], n_ctx)
        y = mixer_post(z, og2, or2, ow2, g_rw, bonus, gla_norm_w[l], rwkv_lnx_w[l], rwkv_lnx_b[l], out_from)
        t_out = t - out_from
        y = matmul(y.reshape(bsz * t_out, d), w_out, tm=_tile(bsz * t_out, 1100), tn=512,
                   name="w_out", layer=l).reshape(bsz, t_out, d)
        xs, h32, hp = resnorm(xs, ctx_tiles, ("x", "h32", "hp"), res=(y, mods[l], 2, post_mix[l]),
                              pre=(mods[l], 3, 4, pre_ffn[l]), x_off=out_from // RESNORM_TM)
        if not ctx_out:
            ctx_tiles = 0
        f = moe_ffn(h32.reshape(-1, d), hp.reshape(-1, d // 2), l, router_w, router_bias, exp_gate, exp_up,
                    exp_down, shared_gate, shared_up, shared_down).reshape(xs.shape)
        if ctx_out:
            xs, hb = resnorm(xs, ctx_tiles, ("x", "hb"), res=(f, mods[l], 5, post_ffn[l]),
                             pre=(mods[l + 1], 0, 1, pre_mix[l + 1]))
        else:
            (xs,) = resnorm(xs, ctx_tiles, ("x",), res=(f, mods[l], 5, post_ffn[l]))
    return xs
```

```python
import functools

import numpy as np
import jax
import jax.numpy as jnp
from jax import lax
from jax.experimental import pallas as pl
from jax.experimental.pallas import tpu as pltpu

F32 = jnp.float32
BF16 = jnp.bfloat16

CHUNK = 64
LEVELS = (32, 16, 8, 4, 2, 1)

GLA_HEADS, GLA_DK, GLA_DV, GLA_RANK, GLA_TAU = 6, 128, 256, 16, 16.0
RET_HEADS, RET_DK, RET_DV = 5, 128, 256
ROPE_BASE = 10000.0
RWKV_HEADS, RWKV_HEAD = 20, 64
RWKV_WIDTH = RWKV_HEADS * RWKV_HEAD
RWKV_DECAY_RANK, RWKV_ICLR_RANK, RWKV_GATE_RANK = 128, 128, 480
GN_EPS = 64e-5
RMS_EPS = 1e-6
N_EXPERTS, TOP_K, D_EXPERT, ROUTED_SCALE = 64, 8, 384, 2.5
GRID_W = 64

GLA_QK, GLA_W = GLA_HEADS * GLA_DK, GLA_HEADS * GLA_DV
RET_QK, RET_W = RET_HEADS * RET_DK, RET_HEADS * RET_DV
GLA_N = 2 * GLA_QK + 2 * GLA_W + GLA_RANK
RET_N = 2 * RET_QK + 2 * RET_W
RWKV_N = 3 * RWKV_WIDTH + RWKV_DECAY_RANK + RWKV_ICLR_RANK + RWKV_GATE_RANK

COL_GLA_Q, COL_GLA_K, COL_GLA_V, COL_GLA_G = 0, 768, 1536, 3072
COL_RW_G1 = 4608
COL_RET_Q, COL_RET_K, COL_RET_V, COL_RET_G = 5120, 5760, 6400, 7680
COL_RW_R, COL_RW_K, COL_RW_V = 8960, 10240, 11520
COL_RW_W1, COL_RW_A1, COL_GLA_LR = 12800, 12928, 13056
N_IN_PAD = 13312

VMEM_LIMIT = 56 * 1024 * 1024


def _cparams(sem):
    return pltpu.CompilerParams(dimension_semantics=sem, vmem_limit_bytes=VMEM_LIMIT)


def _mm_body(a_ref, b_ref, o_ref):
    o_ref[...] = jnp.dot(a_ref[...].astype(BF16), b_ref[...].astype(BF16),
                         preferred_element_type=F32).astype(o_ref.dtype)


def matmul(a, b, *, tm, tn, out_dtype=F32, name="mm", layer=None):
    m, k = a.shape
    n = b.shape[-1]
    assert m % tm == 0 and n % tn == 0, (a.shape, b.shape, tm, tn)
    if layer is None:
        b_spec = pl.BlockSpec((k, tn), lambda i, j: (0, j))
    else:
        b_spec = pl.BlockSpec((None, k, tn), lambda i, j: (layer, 0, j))
    return pl.pallas_call(
        _mm_body,
        out_shape=jax.ShapeDtypeStruct((m, n), out_dtype),
        grid=(m // tm, n // tn),
        in_specs=[pl.BlockSpec((tm, k), lambda i, j: (i, 0)), b_spec],
        out_specs=pl.BlockSpec((tm, tn), lambda i, j: (i, j)),
        compiler_params=_cparams(("parallel", "arbitrary")),
        name=name,
    )(a, b)


def _scan_consts():
    c = CHUNK
    cum = np.zeros((2, 2 * c + 8, c), np.float32)
    masks = np.zeros((2, len(LEVELS), c, c), np.float32)
    tri = np.tril(np.ones((c, c), np.float32))
    cum[0, :c] = tri
    cum[0, c:2 * c] = tri
    cum[1, :c] = tri.T
    cum[1, c:2 * c] = np.triu(np.ones((c, c), np.float32), 1)
    cum[:, 2 * c:] = 1.0
    for li, s in enumerate(LEVELS):
        for i in range(c):
            for j in range(c):
                if (i & s) and not (j & s) and i // (2 * s) == j // (2 * s):
                    masks[0, li, i, j] = 1.0
        masks[1, li] = masks[0, li].T
    return jnp.asarray(cum, BF16), jnp.asarray(masks)


def _split3(x):
    hi = x.astype(BF16)
    r1 = x - hi.astype(F32)
    mid = r1.astype(BF16)
    lo = (r1 - mid.astype(F32)).astype(BF16)
    return hi, mid, lo


def _cumsums(cum, la):
    d = la.shape[1]
    r = jnp.dot(cum, jnp.concatenate(_split3(la), axis=1), preferred_element_type=F32)
    return r[:, :d] + r[:, d:2 * d] + r[:, 2 * d:]


def _boundary(cc, s):
    c, d = cc.shape
    if s >= 8:
        parts = [jnp.broadcast_to(cc[r0 + s - 1:r0 + s], (2 * s, d)) for r0 in range(0, c, 2 * s)]
        return parts[0] if len(parts) == 1 else jnp.concatenate(parts, axis=0)
    cc3 = cc.reshape(c // 8, 8, d)
    if s == 4:
        return jnp.broadcast_to(cc3[:, 3:4], cc3.shape).reshape(c, d)
    sub = lax.broadcasted_iota(jnp.int32, cc3.shape, 1)
    lo = jnp.broadcast_to(cc3[:, 1:2], cc3.shape)
    hi = jnp.broadcast_to(cc3[:, 5:6], cc3.shape)
    return jnp.where(sub < 4, lo, hi).reshape(c, d)


def _nt(a, b):
    return lax.dot_general(a, b, (((1,), (1,)), ((), ())), preferred_element_type=F32)


def _tn(a, b):
    return lax.dot_general(a, b, (((0,), (0,)), ((), ())), preferred_element_type=F32)


def _gla_body(cum_ref, mask_ref, q_ref, k_ref, v_ref, lr_ref, a2_ref, ab_ref, o_ref, st_ref, *,
              heads, dk, dv, scale):
    @pl.when(pl.program_id(2) == 0)
    def _():
        st_ref[...] = jnp.zeros_like(st_ref)

    cum = cum_ref[0]
    c = CHUNK
    gx = jnp.dot(lr_ref[0].astype(BF16), a2_ref[0], preferred_element_type=F32) + ab_ref[0]
    la_all = (jnp.minimum(gx, 0.0) - jnp.log(1.0 + jnp.exp(-jnp.abs(gx)))) * (1.0 / GLA_TAU)
    hs = range(heads)
    qs_, ks_, vs_, las, bqs, ccs, tots = [], [], [], [], [], [], []
    for h in hs:
        la = la_all[:, h * dk:(h + 1) * dk]
        rs = _cumsums(cum, la)
        qs_.append(q_ref[0, :, h * dk:(h + 1) * dk] * scale)
        ks_.append(k_ref[0, :, h * dk:(h + 1) * dk])
        vs_.append(v_ref[0, :, h * dv:(h + 1) * dv])
        las.append(la)
        bqs.append(rs[:c])
        ccs.append(rs[c:2 * c])
        tots.append(rs[2 * c:2 * c + 1])
    sc = [None] * heads
    for li, s in enumerate(LEVELS):
        for h in hs:
            q, k = qs_[h], ks_[h]
            if s == 1:
                qd, kd = q * jnp.exp(las[h]), k
            else:
                g = bqs[h] - _boundary(ccs[h], s)
                qd = q * jnp.exp(jnp.minimum(g, 0.0))
                kd = k * jnp.exp(jnp.minimum(-g, 0.0))
            p = _nt(qd.astype(BF16), kd.astype(BF16)) * mask_ref[0, li]
            sc[h] = p if sc[h] is None else sc[h] + p
    for h in hs:
        q, k, v, bq, tot = qs_[h], ks_[h], vs_[h], bqs[h], tots[h]
        vb = v.astype(BF16)
        dg = jnp.sum(q * k, axis=1, keepdims=True)
        st = st_ref[h]
        o = (jnp.dot(sc[h].astype(BF16), vb, preferred_element_type=F32) + dg * v
             + _nt((q * jnp.exp(bq)).astype(BF16), st.astype(BF16)))
        o_ref[0, 0, :, h * dv:(h + 1) * dv] = o
        kbar = k * jnp.exp(tot - bq)
        st_ref[h] = st * jnp.exp(tot) + _tn(vb, kbar.astype(BF16))


def _ret_body(lam_ref, q_ref, k_ref, v_ref, cos_ref, sin_ref, o_ref, st_ref, *, heads, dk, dv):
    d = pl.program_id(0)
    cos, sin = cos_ref[...], sin_ref[...]

    def rope(x):
        return x * cos + pltpu.roll(x, dk // 2, axis=1) * sin

    @pl.when(pl.program_id(2) == 0)
    def _():
        st_ref[...] = jnp.zeros_like(st_ref)

    c = CHUNK
    ri = lax.broadcasted_iota(jnp.int32, (c, c), 0)
    ci = lax.broadcasted_iota(jnp.int32, (c, c), 1)
    dist = jnp.where(d == 0, ri - ci, ci - ri)
    live = dist >= 0
    distf = jnp.maximum(dist, 0).astype(F32)
    row = lax.broadcasted_iota(jnp.int32, (c, 1), 0)
    pos = jnp.where(d == 0, row + 1, c - row).astype(F32)
    for h in range(heads):
        lam = lam_ref[d * heads + h]
        q = rope(q_ref[0, :, h * dk:(h + 1) * dk])
        k = rope(k_ref[0, :, h * dk:(h + 1) * dk]) * dk ** -0.5
        v = v_ref[0, :, h * dv:(h + 1) * dv]
        vb = v.astype(BF16)
        decay = jnp.where(live, jnp.exp(lam * distf), 0.0)
        sc = _nt(q.astype(BF16), k.astype(BF16)) * decay
        st = st_ref[h]
        o = (jnp.dot(sc.astype(BF16), vb, preferred_element_type=F32)
             + _nt((q * jnp.exp(lam * pos)).astype(BF16), st.astype(BF16)))
        o_ref[0, 0, :, h * dv:(h + 1) * dv] = o
        kbar = k * jnp.exp(lam * (c - pos))
        st_ref[h] = st * jnp.exp(lam * c) + _tn(vb, kbar.astype(BF16))


def _chunk_index(d, c, n_ctx, n_all):
    back = jnp.where(c < n_ctx, n_ctx - 1 - c, n_ctx + n_all - 1 - c)
    return jnp.where(d == 0, c, back)


def _col_spec(width, col, ch):
    assert col % width == 0
    return pl.BlockSpec((1, CHUNK, width), lambda d, b, c, *_: (b, ch(d, c), col // width))


def gla_scan(z, a2, a_bias, n_ctx):
    bsz, t, _ = z.shape
    heads, dk, dv = GLA_HEADS, GLA_DK, GLA_DV
    n_all = t // CHUNK
    cum, masks = _scan_consts()
    lr_w = 128
    a2p = jnp.pad(a2, ((0, 0), (0, lr_w - GLA_RANK), (0, 0))).astype(BF16)
    ch = functools.partial(_chunk_index, n_ctx=n_ctx, n_all=n_all)
    body = functools.partial(_gla_body, heads=heads, dk=dk, dv=dv, scale=dk ** -0.5)
    return pl.pallas_call(
        body,
        out_shape=jax.ShapeDtypeStruct((2, bsz, t, heads * dv), F32),
        grid=(2, bsz, n_all),
        in_specs=[
            pl.BlockSpec((1,) + cum.shape[1:], lambda d, b, c: (d, 0, 0)),
            pl.BlockSpec((1,) + masks.shape[1:], lambda d, b, c: (d, 0, 0, 0)),
            _col_spec(heads * dk, COL_GLA_Q, ch),
            _col_spec(heads * dk, COL_GLA_K, ch),
            _col_spec(heads * dv, COL_GLA_V, ch),
            _col_spec(lr_w, COL_GLA_LR, ch),
            pl.BlockSpec((1, lr_w, heads * dk), lambda d, b, c: (d, 0, 0)),
            pl.BlockSpec((1, 1, heads * dk), lambda d, b, c: (d, 0, 0)),
        ],
        out_specs=pl.BlockSpec((1, 1, CHUNK, heads * dv), lambda d, b, c: (d, b, ch(d, c), 0)),
        scratch_shapes=[pltpu.VMEM((heads, dv, dk), F32)],
        compiler_params=_cparams(("parallel", "parallel", "arbitrary")),
        name="gla_scan",
    )(cum, masks, z, z, z, z, a2p, a_bias[:, None, :])


def ret_scan(z, lam, n_ctx):
    bsz, t, _ = z.shape
    heads, dk, dv = RET_HEADS, RET_DK, RET_DV
    n_all = t // CHUNK
    ch = functools.partial(_chunk_index, n_ctx=n_ctx, n_all=n_all)
    body = functools.partial(_ret_body, heads=heads, dk=dk, dv=dv)
    inv = ROPE_BASE ** (-jnp.linspace(0.0, 1.0, dk // 2, dtype=F32))
    ang = jnp.arange(t, dtype=F32)[:, None] * inv[None, :]
    cos_t = jnp.concatenate([jnp.cos(ang), jnp.cos(ang)], -1)
    sin_t = jnp.concatenate([-jnp.sin(ang), jnp.sin(ang)], -1)
    table = pl.BlockSpec((CHUNK, dk), lambda d, b, c, lam: (ch(d, c), 0))
    grid_spec = pltpu.PrefetchScalarGridSpec(
        num_scalar_prefetch=1,
        grid=(2, bsz, n_all),
        in_specs=[
            _col_spec(heads * dk, COL_RET_Q, ch),
            _col_spec(heads * dk, COL_RET_K, ch),
            _col_spec(heads * dv, COL_RET_V, ch),
            table, table,
        ],
        out_specs=pl.BlockSpec((1, 1, CHUNK, heads * dv), lambda d, b, c, lam: (d, b, ch(d, c), 0)),
        scratch_shapes=[pltpu.VMEM((heads, dv, dk), F32)],
    )
    return pl.pallas_call(
        body,
        out_shape=jax.ShapeDtypeStruct((2, bsz, t, heads * dv), F32),
        grid_spec=grid_spec,
        compiler_params=_cparams(("parallel", "parallel", "arbitrary")),
        name="ret_scan",
    )(lam.reshape(-1).astype(F32), z, z, z, cos_t, sin_t)


RWKV_GROUP = 4
RWKV_GW = RWKV_GROUP * RWKV_HEAD


def _rwkv_consts():
    c = CHUNK
    gw = RWKV_GW
    cum = np.zeros((2, 2 * c + 16, c), np.float32)
    tri = np.tril(np.ones((c, c), np.float32))
    mid = c // 2
    cum[0, :c] = tri - tri[mid - 1][None, :]
    cum[0, c:2 * c] = 1.0 - tri
    cum[0, 2 * c + 8:] = tri[mid - 1][None, :]
    cum[1, :c] = tri.T - tri.T[mid][None, :]
    cum[1, c:2 * c] = 1.0 - tri.T
    cum[1, 2 * c + 8:] = tri.T[mid][None, :]
    cum[:, 2 * c:2 * c + 8] = 1.0
    t = np.arange(c)[:, None]
    s = np.tile(np.arange(c), RWKV_GROUP)[None, :]
    masks = np.zeros((2, 3, c, gw), np.float32)
    masks[0, 0], masks[0, 1] = s < t, s <= t
    masks[1, 0], masks[1, 1] = s > t, s >= t
    masks[:, 2] = s == t
    return jnp.asarray(cum, BF16), jnp.asarray(masks)


def _rwkv_body(cum_ref, mask_ref, r_ref, k_ref, v_ref, kk_ref, kka_ref, lw_ref, o_ref, st_ref, *, groups):
    @pl.when(pl.program_id(2) == 0)
    def _():
        st_ref[...] = jnp.zeros_like(st_ref)

    c, gw = CHUNK, RWKV_GW
    cum = cum_ref[0]
    strict, incl, eye = mask_ref[0, 0], mask_ref[0, 1], mask_ref[0, 2]
    rb = lax.broadcasted_iota(jnp.int32, (gw, gw), 0) // RWKV_HEAD
    cb = lax.broadcasted_iota(jnp.int32, (gw, gw), 1) // RWKV_HEAD
    same_head = rb == cb

    def bd(x):
        xb = x.astype(BF16)
        return jnp.where(same_head, jnp.concatenate([xb] * RWKV_GROUP, axis=0), jnp.zeros((), BF16))

    def mm(a, b):
        return jnp.dot(a.astype(BF16), b, preferred_element_type=F32)

    gs = range(groups)
    sls = [slice(g * gw, (g + 1) * gw) for g in gs]
    pre = []
    for sl in sls:
        r, k, v = r_ref[0, :, sl], k_ref[0, :, sl], v_ref[0, :, sl]
        kk, kka, lw = kk_ref[0, :, sl], kka_ref[0, :, sl], lw_ref[0, 0, :, sl]
        rs = _cumsums(cum, lw)
        gm, ge, tot, gmid = rs[:c], rs[c:2 * c], rs[2 * c:2 * c + 1], rs[2 * c + 8:2 * c + 9]
        e_in, e_out, e_end = jnp.exp(gm), jnp.exp(-gm), jnp.exp(ge)
        kt_rel = kk * (e_in * jnp.exp(-lw))
        lhs = jnp.concatenate([kt_rel, r * e_in], axis=0)
        pre.append(dict(lhs=lhs.astype(BF16), lhs_abs=(lhs * jnp.exp(gmid)).astype(BF16),
                        kka_o=kka * e_out, k_o=k * e_out, v=v, tot=tot,
                        kv_end=jnp.concatenate([k * e_end, kka * e_end], axis=0).astype(BF16)))
    s_b = [_nt(q["lhs"], bd(q["kka_o"])) for q in pre]
    s_k = [_nt(q["lhs"], bd(q["k_o"])) for q in pre]
    ns = [-(s[:c] * strict) for s in s_b]
    xs = [eye + n for n in ns]
    ps = [mm(n, bd(n)) for n in ns]
    for _ in range(4):
        pps = [mm(jnp.concatenate([p, x], axis=0), bd(p)) for p, x in zip(ps, xs)]
        ps = [pp[:c] for pp in pps]
        xs = [x + pp[c:] for x, pp in zip(xs, pps)]
    xs = [x + mm(x, bd(p)) for p, x in zip(ps, xs)]
    sts = [st_ref[g] for g in gs]
    hs = [_nt(q["lhs_abs"], st.astype(BF16)) for q, st in zip(pre, sts)]
    vs = [mm(jnp.concatenate([sk[:c] * strict, sk[c:] * incl], axis=0), bd(q["v"])) for sk, q in zip(s_k, pre)]
    us = [mm(x, bd(h[:c] + w[:c])) for x, h, w in zip(xs, hs, vs)]
    for g in gs:
        o_ref[0, 0, :, sls[g]] = hs[g][c:] + vs[g][c:] - mm(s_b[g][c:] * incl, bd(us[g]))
    for g in gs:
        upd = _tn(jnp.concatenate([pre[g]["v"], -us[g]], axis=0).astype(BF16), pre[g]["kv_end"])
        st_ref[g] = sts[g] * jnp.exp(pre[g]["tot"]) + jnp.where(same_head, upd, 0.0)


def rwkv_scan(r, k, v, kk, kka, lw2, n_ctx):
    bsz, t, ch_w = r.shape
    groups = ch_w // RWKV_GW
    n_all = t // CHUNK
    cum, masks = _rwkv_consts()
    ch = functools.partial(_chunk_index, n_ctx=n_ctx, n_all=n_all)
    tok = pl.BlockSpec((1, CHUNK, ch_w), lambda d, b, c: (b, ch(d, c), 0))
    dir_tok = pl.BlockSpec((1, 1, CHUNK, ch_w), lambda d, b, c: (d, b, ch(d, c), 0))
    return pl.pallas_call(
        functools.partial(_rwkv_body, groups=groups),
        out_shape=jax.ShapeDtypeStruct((2, bsz, t, ch_w), F32),
        grid=(2, bsz, n_all),
        in_specs=[
            pl.BlockSpec((1,) + cum.shape[1:], lambda d, b, c: (d, 0, 0)),
            pl.BlockSpec((1,) + masks.shape[1:], lambda d, b, c: (d, 0, 0, 0)),
            tok, tok, tok, tok, tok, dir_tok,
        ],
        out_specs=dir_tok,
        scratch_shapes=[pltpu.VMEM((groups, RWKV_GW, RWKV_GW), F32)],
        compiler_params=_cparams(("parallel", "parallel", "arbitrary")),
        name="rwkv_scan",
    )(cum, masks, r, k, v, kk, kka, lw2)


def _split2(x):
    hi = x.astype(BF16)
    return hi, (x - hi.astype(F32)).astype(BF16)


ROUTER_LANES = 128


def _router_body(h_ref, w_ref, bias_ref, idx_ref, gate_ref, *, n_experts, top_k):
    a_hi, a_mid = _split2(h_ref[...])
    b_hi, b_mid = _split2(w_ref[...])
    dot = functools.partial(jnp.dot, preferred_element_type=F32)
    scores = jax.nn.sigmoid(dot(a_hi, b_hi) + (dot(a_hi, b_mid) + dot(a_mid, b_hi)))
    lane = lax.broadcasted_iota(jnp.int32, scores.shape, 1)
    cand = jnp.where(lane < n_experts, scores + bias_ref[...], -jnp.inf)
    idx_out = jnp.zeros(scores.shape, jnp.int32)
    val_out = jnp.zeros(scores.shape, F32)
    total = jnp.zeros((scores.shape[0], 1), F32)
    for j in range(top_k):
        best = jnp.max(cand, axis=1, keepdims=True)
        arg = jnp.min(jnp.where(cand == best, lane, ROUTER_LANES), axis=1, keepdims=True)
        pick = lane == arg
        val = jnp.sum(jnp.where(pick, scores, 0.0), axis=1, keepdims=True)
        cand = jnp.where(pick, -jnp.inf, cand)
        idx_out = jnp.where(lane == j, arg, idx_out)
        val_out = jnp.where(lane == j, val, val_out)
        total = total + val
    idx_ref[...] = idx_out
    gate_ref[...] = ROUTED_SCALE * val_out / total


def router(h, router_w, router_bias, layer):
    t, d = h.shape
    e = router_w.shape[-1]
    tm = _tile(t, 600)
    w = jnp.pad(router_w[layer].astype(F32), ((0, 0), (0, ROUTER_LANES - e)))
    bias = jnp.pad(router_bias[layer].astype(F32), (0, ROUTER_LANES - e))[None]
    idx, gates = pl.pallas_call(
        functools.partial(_router_body, n_experts=e, top_k=TOP_K),
        out_shape=(jax.ShapeDtypeStruct((t, ROUTER_LANES), jnp.int32), jax.ShapeDtypeStruct((t, ROUTER_LANES), F32)),
        grid=(t // tm,),
        in_specs=[pl.BlockSpec((tm, d), lambda i: (i, 0)), pl.BlockSpec((d, ROUTER_LANES), lambda i: (0, 0)),
                  pl.BlockSpec((1, ROUTER_LANES), lambda i: (0, 0))],
        out_specs=(pl.BlockSpec((tm, ROUTER_LANES), lambda i: (i, 0)), pl.BlockSpec((tm, ROUTER_LANES), lambda i: (i, 0))),
        compiler_params=_cparams(("parallel",)),
        name="router",
    )(h, w, bias)
    return idx[:, :TOP_K], gates[:, :TOP_K]


MOE_BM = 256


def _tile(m, cap, mult=16):
    return max(t for t in range(mult, cap + 1, mult) if m % t == 0)


def _pack_pairs(x):
    n = x.shape[-1] // 2
    bits = lax.bitcast_convert_type(x, jnp.uint32)
    bits = (bits + jnp.uint32(0x7FFF) + ((bits >> 16) & jnp.uint32(1))) & jnp.uint32(0xFFFF0000)
    return bits[..., :n] | (bits[..., n:] >> 16)


def _unpack_pairs(w):
    hi = lax.bitcast_convert_type(w & jnp.uint32(0xFFFF0000), F32)
    lo = lax.bitcast_convert_type(w << 16, F32)
    return jnp.concatenate([hi, lo], axis=-1)


def _swiglu(x, wg, wu, wd, gate=None):
    h1 = jnp.dot(x, wg, preferred_element_type=F32)
    h2 = jnp.dot(x, wu, preferred_element_type=F32)
    a = h1 * jax.nn.sigmoid(h1) * h2
    if gate is not None:
        a = a * gate
    return jnp.dot(a.astype(BF16), wd, preferred_element_type=F32)


def _cast_rows(src, dst, rows):
    def body(j, carry):
        r = pl.multiple_of(j * rows, rows)
        dst[pl.ds(r, rows), :] = src[pl.ds(r, rows), :].astype(BF16)
        return carry
    lax.fori_loop(0, src.shape[0] // rows, body, 0)


def _experts_body(be_ref, nx_ref, nv_ref, tok_ref, h_hbm, gate_ref, wg_hbm, wu_hbm, wd_hbm, o_ref,
                  xbuf0, xbuf1, st_g, st_u, st_d, wg_b, wu_b, wd_b, xsem, wsem, *, layer, bm):
    i = pl.program_id(0)
    n_live = nv_ref[0]
    slot = lax.rem(i, 2)

    def row_copy(base, r, buf, s):
        return pltpu.make_async_copy(h_hbm.at[pl.ds(tok_ref[base + r], 1)], buf.at[pl.ds(r, 1)], xsem.at[s])

    def rows_wait(buf, s):
        pltpu.make_async_copy(h_hbm.at[pl.ds(0, bm)], buf, xsem.at[s]).wait()

    def weight_copies(e):
        return (pltpu.make_async_copy(wg_hbm.at[layer, e], st_g, wsem.at[0]),
                pltpu.make_async_copy(wu_hbm.at[layer, e], st_u, wsem.at[1]),
                pltpu.make_async_copy(wd_hbm.at[layer, e], st_d, wsem.at[2]))

    @pl.when(i == 0)
    def _():
        def body(r, carry):
            row_copy(0, r, xbuf0, 0).start()
            return carry
        lax.fori_loop(0, bm, body, 0, unroll=8)
        for cp in weight_copies(be_ref[0]):
            cp.start(priority=1)

    @pl.when((i == 0) | (be_ref[i] != be_ref[jnp.maximum(i - 1, 0)]))
    def _():
        for cp in weight_copies(be_ref[i]):
            cp.wait()
        _cast_rows(st_g, wg_b, 256)
        _cast_rows(st_u, wu_b, 256)
        _cast_rows(st_d, wd_b, 32)

        @pl.when(nx_ref[i] >= 0)
        def _():
            for cp in weight_copies(nx_ref[i]):
                cp.start(priority=1)

    dot = functools.partial(jnp.dot, preferred_element_type=F32)
    half = xbuf0.shape[1]
    kc = 256
    n_k = half // kc
    n_issue = n_k // 2
    per = bm // n_issue

    def step(cur, cur_s, nxt, nxt_s):
        rows_wait(cur, cur_s)
        base = jnp.minimum(i + 1, n_live - 1) * bm
        h1 = jnp.zeros((bm, wg_b.shape[1]), F32)
        h2 = jnp.zeros((bm, wg_b.shape[1]), F32)
        for s in range(n_k):
            if s < n_issue:
                for r in range(s * per, (s + 1) * per):
                    row_copy(base, r, nxt, nxt_s).start()
            w = cur[:, s * kc:(s + 1) * kc]
            x_a = lax.bitcast_convert_type(w & jnp.uint32(0xFFFF0000), F32).astype(BF16)
            x_b = lax.bitcast_convert_type(w << 16, F32).astype(BF16)
            rows_a, rows_b = slice(s * kc, (s + 1) * kc), slice(half + s * kc, half + (s + 1) * kc)
            h1 = h1 + (dot(x_a, wg_b[rows_a, :]) + dot(x_b, wg_b[rows_b, :]))
            h2 = h2 + (dot(x_a, wu_b[rows_a, :]) + dot(x_b, wu_b[rows_b, :]))
        a = (h1 * jax.nn.sigmoid(h1) * h2 * gate_ref[...]).astype(BF16)
        for s in range(n_k):
            y = jnp.concatenate([dot(a, wd_b[:, s * kc:(s + 1) * kc]),
                                 dot(a, wd_b[:, half + s * kc:half + (s + 1) * kc])], axis=1)
            o_ref[:, s * kc:(s + 1) * kc] = _pack_pairs(y)

        @pl.when(i == n_live - 1)
        def _():
            rows_wait(nxt, nxt_s)

    @pl.when((i < n_live) & (slot == 0))
    def _():
        step(xbuf0, 0, xbuf1, 1)

    @pl.when((i < n_live) & (slot == 1))
    def _():
        step(xbuf1, 1, xbuf0, 0)

    @pl.when(i >= n_live)
    def _():
        o_ref[...] = jnp.zeros_like(o_ref)


def experts(h_pad, gate_rows, exp_gate, exp_up, exp_down, layer, block_expert, next_expert, n_live, buf_tok):
    bm = MOE_BM
    n = buf_tok.shape[0]
    d = exp_gate.shape[-2]
    f = exp_gate.shape[-1]
    dp = h_pad.shape[1]
    any_spec = pl.BlockSpec(memory_space=pl.ANY)
    grid_spec = pltpu.PrefetchScalarGridSpec(
        num_scalar_prefetch=4,
        grid=(n // bm,),
        in_specs=[any_spec, pl.BlockSpec((bm, 1), lambda i, *_: (i, 0)), any_spec, any_spec, any_spec],
        out_specs=pl.BlockSpec((bm, dp), lambda i, *_: (i, 0)),
        scratch_shapes=[
            pltpu.VMEM((bm, dp), jnp.uint32), pltpu.VMEM((bm, dp), jnp.uint32),
            pltpu.VMEM((d, f), F32), pltpu.VMEM((d, f), F32), pltpu.VMEM((f, d), F32),
            pltpu.VMEM((d, f), BF16), pltpu.VMEM((d, f), BF16), pltpu.VMEM((f, d), BF16),
            pltpu.SemaphoreType.DMA((2,)), pltpu.SemaphoreType.DMA((3,)),
        ],
    )
    return pl.pallas_call(
        functools.partial(_experts_body, layer=layer, bm=bm),
        out_shape=jax.ShapeDtypeStruct((n, dp), jnp.uint32),
        grid_spec=grid_spec,
        compiler_params=_cparams(("arbitrary",)),
        name="experts",
    )(block_expert, next_expert, n_live, buf_tok, h_pad, gate_rows, exp_gate, exp_up, exp_down)


def _shared_body(x_ref, wg_ref, wu_ref, wd_ref, o_ref, wg_b, wu_b, wd_b):
    @pl.when(pl.program_id(0) == 0)
    def _():
        _cast_rows(wg_ref, wg_b, 256)
        _cast_rows(wu_ref, wu_b, 256)
        _cast_rows(wd_ref, wd_b, 32)

    o_ref[...] = _swiglu(x_ref[...].astype(BF16), wg_b[...], wu_b[...], wd_b[...])


def shared_expert(h, sh_gate, sh_up, sh_down, layer):
    t, d = h.shape
    f = sh_gate.shape[-1]
    tm = _tile(t, 300)
    once = pl.Buffered(1)
    return pl.pallas_call(
        _shared_body,
        out_shape=jax.ShapeDtypeStruct((t, d), F32),
        grid=(t // tm,),
        in_specs=[pl.BlockSpec((tm, d), lambda i: (i, 0)),
                  pl.BlockSpec((None, d, f), lambda i: (layer, 0, 0), pipeline_mode=once),
                  pl.BlockSpec((None, d, f), lambda i: (layer, 0, 0), pipeline_mode=once),
                  pl.BlockSpec((None, f, d), lambda i: (layer, 0, 0), pipeline_mode=once)],
        out_specs=pl.BlockSpec((tm, d), lambda i: (i, 0)),
        scratch_shapes=[pltpu.VMEM((d, f), BF16), pltpu.VMEM((d, f), BF16), pltpu.VMEM((f, d), BF16)],
        compiler_params=_cparams(("arbitrary",)),
        name="shared",
    )(h, sh_gate, sh_up, sh_down)


COMBINE_TB = 64


def _combine_body(pos_ref, y_hbm, sh_ref, o_ref, buf, sem, *, tb, k):
    i = pl.program_id(0)
    n = pl.num_programs(0)
    slot = lax.rem(i, 2)

    def gather(blk, s):
        base = blk * tb * k

        def body(t, carry):
            for kk in range(k):
                p = pos_ref[base + t * k + kk]
                pltpu.make_async_copy(y_hbm.at[pl.ds(p, 1)], buf.at[s, pl.ds(kk * tb + t, 1)],
                                      sem.at[s]).start(priority=kk % 2)
            return carry
        for t in range(tb):
            body(t, 0)

    @pl.when(i == 0)
    def _():
        gather(0, 0)

    @pl.when(i + 1 < n)
    def _():
        gather(i + 1, 1 - slot)

    pltpu.make_async_copy(y_hbm.at[pl.ds(0, tb * k)], buf.at[slot], sem.at[slot]).wait()
    acc = sh_ref[...]
    for kk in range(k):
        acc = acc + _unpack_pairs(buf[slot, kk * tb:(kk + 1) * tb, :])
    o_ref[...] = acc


def combine(y, pos, shared):
    t, k = pos.shape
    d = shared.shape[1]
    dp = y.shape[1]
    tb = COMBINE_TB
    grid_spec = pltpu.PrefetchScalarGridSpec(
        num_scalar_prefetch=1,
        grid=(t // tb,),
        in_specs=[pl.BlockSpec(memory_space=pl.ANY), pl.BlockSpec((tb, d), lambda i, *_: (i, 0))],
        out_specs=pl.BlockSpec((tb, d), lambda i, *_: (i, 0)),
        scratch_shapes=[pltpu.VMEM((2, tb * k, dp), jnp.uint32), pltpu.SemaphoreType.DMA((2,))],
    )
    return pl.pallas_call(
        functools.partial(_combine_body, tb=tb, k=k),
        out_shape=jax.ShapeDtypeStruct((t, d), F32),
        grid_spec=grid_spec,
        compiler_params=_cparams(("arbitrary",)),
        name="combine",
    )(pos.reshape(-1), y, shared)


def moe_ffn(h, hp, layer, router_w, router_bias, exp_gate, exp_up, exp_down, sh_gate, sh_up, sh_down):
    t, d = h.shape
    e, k, bm = N_EXPERTS, TOP_K, MOE_BM
    i32 = jnp.int32
    idx, gates = router(h, router_w, router_bias, layer)
    flat_e = idx.reshape(-1)
    iota = jnp.arange(t * k, dtype=i32)
    e_sorted, order = lax.sort_key_val(flat_e, iota)
    _, rank = lax.sort_key_val(order, iota)
    bounds = jnp.searchsorted(e_sorted, jnp.arange(e + 1, dtype=i32), side='left',
                              method='compare_all').astype(i32)
    first, counts = bounds[:-1], bounds[1:] - bounds[:-1]
    padded = (counts + bm - 1) // bm * bm
    ends = jnp.cumsum(padded)
    starts = ends - padded
    pos = (iota + (starts - first)[e_sorted])[rank].reshape(t, k)
    n_blocks = -(-(t * k) // bm) + e
    blk_start = jnp.arange(n_blocks, dtype=i32) * bm
    block_valid = (blk_start < ends[-1]).astype(i32)
    last_e = jnp.max(jnp.where(counts > 0, jnp.arange(e, dtype=i32), 0))
    block_expert = jnp.where(block_valid > 0,
                             jnp.minimum(jnp.searchsorted(ends, blk_start, side='right',
                                                          method='compare_all').astype(i32), e - 1), last_e)
    present = jnp.where(counts > 0, jnp.arange(e, dtype=i32), e)
    nxt = lax.cummin(jnp.concatenate([present[1:], jnp.full((1,), e, i32)]), reverse=True)
    next_expert = jnp.where(nxt < e, nxt, -1)[block_expert]
    blk_off = blk_start - starts[block_expert]
    blk_live = jnp.where(block_valid > 0, jnp.clip(counts[block_expert] - blk_off, 0, bm), 0)
    lane = jnp.arange(bm, dtype=i32)[None, :]
    live = (lane < blk_live[:, None]).reshape(-1)
    src = order[jnp.clip((first[block_expert] + blk_off)[:, None] + lane, 0, t * k - 1).reshape(-1)]
    buf_tok = jnp.where(live, src // k, jnp.arange(n_blocks * bm, dtype=i32) % t).astype(i32)
    gate_rows = jnp.where(live, gates.reshape(-1)[src], 0.0)[:, None]

    n_live = (ends[-1] // bm).astype(i32).reshape(1)
    y = experts(hp, gate_rows, exp_gate, exp_up, exp_down, layer, block_expert, next_expert, n_live, buf_tok)
    return combine(y, pos, shared_expert(h, sh_gate, sh_up, sh_down, layer))


def _permute_cols(w):
    gla, ret, rw = w[..., :GLA_N], w[..., GLA_N:GLA_N + RET_N], w[..., GLA_N + RET_N:]
    rkv = 3 * RWKV_WIDTH
    lora = rkv + RWKV_DECAY_RANK + RWKV_ICLR_RANK

    def zeros(n):
        return jnp.zeros(w.shape[:-1] + (n,), w.dtype)

    parts = [gla[..., :GLA_N - GLA_RANK],
             rw[..., lora:], zeros(COL_RET_Q - COL_RW_G1 - RWKV_GATE_RANK),
             ret,
             rw[..., :lora],
             gla[..., GLA_N - GLA_RANK:], zeros(N_IN_PAD - COL_GLA_LR - GLA_RANK)]
    return jnp.concatenate(parts, -1)


RESNORM_TM = 128


def _resnorm_body(*refs, has_res, has_pre, outs):
    def rms(u):
        return u * lax.rsqrt(jnp.mean(u * u, axis=1, keepdims=True) + RMS_EPS)

    it = iter(refs)
    x = next(it)[0]
    if has_res:
        f, gate, post_w = next(it)[0], next(it)[0], next(it)[...]
        x = x + gate * (rms(f) * post_w)
    if has_pre:
        shift, scale, pre_w = next(it)[0], next(it)[0], next(it)[...]
        h = rms(x) * pre_w * (1.0 + scale) + shift
    for name in outs:
        out = next(it)
        if name == "x":
            out[0] = x
        elif name == "h32":
            out[0] = h
        elif name == "hb":
            out[0] = h.astype(BF16)
        else:
            out[0] = _pack_pairs(h)


def resnorm(xs, ctx_tiles, outs, res=None, pre=None, x_off=0):
    bsz, tx, d = xs.shape
    tm = RESNORM_TM
    t_out = tx - x_off * tm

    def mod_spec(col):
        return pl.BlockSpec((1, 1, d), lambda b, j: (jnp.where(j + x_off < ctx_tiles, bsz, b), 0, col))

    w_spec = pl.BlockSpec((1, d), lambda b, j: (0, 0))
    tok = pl.BlockSpec((1, tm, d), lambda b, j: (b, j, 0))
    args, in_specs = [xs], [pl.BlockSpec((1, tm, d), lambda b, j: (b, j + x_off, 0))]
    if res is not None:
        f, mod, col, w = res
        assert f.shape == (bsz, t_out, d)
        args += [f, mod, w[None]]
        in_specs += [tok, mod_spec(col), w_spec]
    if pre is not None:
        mod, shift_col, scale_col, w = pre
        args += [mod, mod, w[None]]
        in_specs += [mod_spec(shift_col), mod_spec(scale_col), w_spec]
    kinds = {"x": (d, F32), "h32": (d, F32), "hb": (d, BF16), "hp": (d // 2, jnp.uint32)}
    out_shape = tuple(jax.ShapeDtypeStruct((bsz, t_out, kinds[o][0]), kinds[o][1]) for o in outs)
    out_specs = tuple(pl.BlockSpec((1, tm, kinds[o][0]), lambda b, j: (b, j, 0)) for o in outs)
    return pl.pallas_call(
        functools.partial(_resnorm_body, has_res=res is not None, has_pre=pre is not None, outs=outs),
        out_shape=out_shape,
        grid=(bsz, t_out // tm),
        in_specs=in_specs,
        out_specs=out_specs,
        compiler_params=_cparams(("parallel", "arbitrary")),
        name="resnorm",
    )(*args)


def ret_mixer(z, decay_logit, n_ctx):
    return ret_scan(z, jax.nn.log_sigmoid(decay_logit.astype(F32)), n_ctx)


POST_TM = 128


def _mixer_post_body(og_ref, or_ref, ow_ref, gg_ref, gr_ref, gw_ref, bonus_ref, nw_ref, lw_ref, lb_ref,
                     seg_ref, segt_ref, y_ref):
    def silu(u):
        return u * jax.nn.sigmoid(u)

    def head_rms(x):
        return x * lax.rsqrt(jnp.mean(x * x, axis=1, keepdims=True) + RMS_EPS)

    og = og_ref[0, 0] + og_ref[1, 0]
    gg = gg_ref[0]
    for h in range(GLA_HEADS):
        sl = slice(h * GLA_DV, (h + 1) * GLA_DV)
        y_ref[0, :, sl] = (head_rms(og[:, sl]) * nw_ref[...] * silu(gg[:, sl])).astype(y_ref.dtype)
    orr = or_ref[0, 0] + or_ref[1, 0]
    gr = gr_ref[0]
    for h in range(RET_HEADS):
        sl = slice(h * RET_DV, (h + 1) * RET_DV)
        y_ref[0, :, GLA_W + h * RET_DV:GLA_W + (h + 1) * RET_DV] = (
            head_rms(orr[:, sl]) * silu(gr[:, sl])).astype(y_ref.dtype)
    dot = functools.partial(jnp.dot, preferred_element_type=F32)

    def head_mean(x):
        hi, mid = _split2(x)
        s = (dot(hi, seg_ref[...]) + dot(mid, seg_ref[...])) * (1.0 / RWKV_HEAD)
        hi, mid = _split2(s)
        return dot(hi, segt_ref[...]) + dot(mid, segt_ref[...])

    ow = ow_ref[0, 0] + ow_ref[1, 0]
    xc = ow - head_mean(ow)
    gn = xc * lax.rsqrt(head_mean(xc * xc) + GN_EPS) * lw_ref[...] + lb_ref[...]
    y_ref[0, :, GLA_W + RET_W:] = ((gn + bonus_ref[0]) * gw_ref[0]).astype(y_ref.dtype)


def mixer_post(z, og2, or2, ow2, g_rw, bonus, norm_w, lnx_w, lnx_b, out_from):
    bsz, t, _ = z.shape
    tm = POST_TM
    off = out_from // tm
    assert out_from % tm == 0 and t % tm == 0
    cw = RWKV_WIDTH
    seg = np.zeros((cw, 128), np.float32)
    seg[np.arange(cw), np.arange(cw) // RWKV_HEAD] = 1.0
    params = [norm_w[None], lnx_w[None], lnx_b[None], jnp.asarray(seg, BF16), jnp.asarray(seg.T, BF16)]

    def two(width):
        return pl.BlockSpec((2, 1, tm, width), lambda b, j: (0, b, j + off, 0))

    def tok(width, blk=0):
        return pl.BlockSpec((1, tm, width), lambda b, j: (b, j + off, blk))

    def whole(x):
        return pl.BlockSpec(x.shape, lambda b, j, nd=x.ndim: (0,) * nd)

    d = GLA_W + RET_W + cw
    return pl.pallas_call(
        _mixer_post_body,
        out_shape=jax.ShapeDtypeStruct((bsz, t - out_from, d), BF16),
        grid=(bsz, (t - out_from) // tm),
        in_specs=[two(GLA_W), two(RET_W), two(cw), tok(GLA_W, COL_GLA_G // GLA_W), tok(RET_W, COL_RET_G // RET_W),
                  tok(cw), tok(cw)] + [whole(x) for x in params],
        out_specs=pl.BlockSpec((1, tm, d), lambda b, j: (b, j, 0)),
        compiler_params=_cparams(("parallel", "arbitrary")),
        name="mixer_post",
    )(og2, or2, ow2, z, z, g_rw, bonus, *params)


RW_FIELDS = ((COL_RW_R, RWKV_WIDTH, 0), (COL_RW_K, RWKV_WIDTH, RWKV_WIDTH), (COL_RW_V, RWKV_WIDTH, 2 * RWKV_WIDTH),
             (COL_RW_W1, RWKV_DECAY_RANK + RWKV_ICLR_RANK, 3 * RWKV_WIDTH),
             (COL_RW_G1, COL_RET_Q - COL_RW_G1, 3 * RWKV_WIDTH + RWKV_DECAY_RANK + RWKV_ICLR_RANK))


def _rwkv_prep_body(*refs, n_ctx, n_all):
    (tabs, zs, (w0_ref, w2_ref, a0_ref, a2_ref, g2_ref, kk_ref_, ka_ref, rk_ref, seg_ref, segt_ref),
     (r_out, k_out, v_out, kk_out, kka_out, lw_out, g_out, bonus_out)) = (
        refs[0:5], refs[5:20], refs[20:30], refs[30:38])
    j = pl.program_id(1)
    c = CHUNK
    is_ctx = j < n_ctx
    one = lambda cond: jnp.where(cond, 1.0, 0.0).astype(F32)
    f_prev = one(is_ctx & (j > 0))
    f_next = one(is_ctx & (j < n_ctx - 1))
    f_up = one((j > n_ctx) & (~is_ctx))
    f_down = one((j < n_all - 1) & (~is_ctx))
    f_ctx = one(is_ctx)
    row = lax.broadcasted_iota(jnp.int32, (c, 1), 0)
    fields = []
    for f in range(5):
        prv, cur, nxt = zs[3 * f][0], zs[3 * f + 1][0], zs[3 * f + 2][0]
        tab = tabs[f]
        mu, m_l, m_r, m_u, m_d, m_p, m_n = (tab[i:i + 1] for i in range(7))
        before = jnp.where(row == 0, prv[c - 1:c] * f_prev, pltpu.roll(cur, 1, axis=0))
        after = jnp.where(row == c - 1, nxt[0:1] * f_next, pltpu.roll(cur, c - 1, axis=0))
        shifted = ((m_l + f_ctx * (m_p - m_l)) * before + (m_r + f_ctx * (m_n - m_r)) * after
                   + (m_u * f_up) * prv + (m_d * f_down) * nxt)
        fields.append(cur + mu * (shifted - cur))
    r, k, v, wa, xg = fields
    dot = functools.partial(jnp.dot, preferred_element_type=F32)
    txw = jnp.tanh(wa[:, :RWKV_DECAY_RANK]).astype(BF16)
    for d in range(2):
        lw_out[d, 0] = -jax.nn.sigmoid(w0_ref[d:d + 1] + dot(txw, w2_ref[d])) * float(np.exp(-0.5))
    a = jax.nn.sigmoid(a0_ref[...] + dot(wa[:, RWKV_DECAY_RANK:].astype(BF16), a2_ref[...]))
    g_out[0] = dot(jax.nn.sigmoid(xg).astype(BF16), g2_ref[...])

    def head_sum(x):
        hi, mid = _split2(x)
        s = dot(hi, seg_ref[...]) + dot(mid, seg_ref[...])
        hi, mid = _split2(s)
        return dot(hi, segt_ref[...]) + dot(mid, segt_ref[...])

    kraw = k * kk_ref_[...]
    kk = kraw * lax.rsqrt(head_sum(kraw * kraw) + 1e-12)
    k2 = k * (1.0 + (a - 1.0) * ka_ref[...])
    r_out[0], k_out[0], v_out[0], kk_out[0], kka_out[0] = r, k2, v, kk, kk * a
    bonus_out[0] = head_sum(r * k2 * rk_ref[...]) * v


def rwkv_prep(z, mu, w0, w2, a0, a2, g2, k_k, k_a, r_k, n_ctx):
    bsz, t, _ = z.shape
    cw = RWKV_WIDTH
    n_all = t // CHUNK
    assert CHUNK == GRID_W
    quarter, half = RWKV_N // 4, RWKV_N // 2
    tabs, z_specs = [], []
    for col, width, ch0 in RW_FIELDS:
        n_real = min(width, RWKV_N - ch0)
        ch = ch0 + np.arange(width)
        real = np.arange(width) < n_real
        tab = np.zeros((8, width), np.float32)
        tab[1], tab[2] = real & (ch < quarter), real & (ch >= quarter) & (ch < 2 * quarter)
        tab[3], tab[4] = real & (ch >= 2 * quarter) & (ch < 3 * quarter), real & (ch >= 3 * quarter)
        tab[5], tab[6] = real & (ch < half), real & (ch >= half)
        tab = jnp.asarray(tab).at[0, :n_real].set(mu[ch0:ch0 + n_real])
        tabs.append(tab)
        blk = col // width
        assert col % width == 0
        z_specs += [pl.BlockSpec((1, CHUNK, width), lambda b, j, blk=blk: (b, jnp.maximum(j - 1, 0), blk)),
                    pl.BlockSpec((1, CHUNK, width), lambda b, j, blk=blk: (b, j, blk)),
                    pl.BlockSpec((1, CHUNK, width), lambda b, j, blk=blk: (b, jnp.minimum(j + 1, n_all - 1), blk))]
    gpad = RW_FIELDS[4][1] - RWKV_GATE_RANK
    seg = np.zeros((cw, 128), np.float32)
    seg[np.arange(cw), np.arange(cw) // RWKV_HEAD] = 1.0
    params = [w0, w2.astype(BF16), a0[None], a2.astype(BF16), jnp.pad(g2, ((0, gpad), (0, 0))).astype(BF16),
              k_k[None], k_a[None], r_k.reshape(1, cw), jnp.asarray(seg, BF16), jnp.asarray(seg.T, BF16)]

    def whole(x):
        return pl.BlockSpec(x.shape, lambda b, j, nd=x.ndim: (0,) * nd)

    tok = pl.BlockSpec((1, CHUNK, cw), lambda b, j: (b, j, 0))
    tok_sds = jax.ShapeDtypeStruct((bsz, t, cw), F32)
    return pl.pallas_call(
        functools.partial(_rwkv_prep_body, n_ctx=n_ctx, n_all=n_all),
        out_shape=(tok_sds,) * 5 + (jax.ShapeDtypeStruct((2, bsz, t, cw), F32), tok_sds, tok_sds),
        grid=(bsz, n_all),
        in_specs=[whole(x) for x in tabs] + z_specs + [whole(x) for x in params],
        out_specs=(tok,) * 5 + (pl.BlockSpec((2, 1, CHUNK, cw), lambda b, j: (0, b, j, 0)), tok, tok),
        compiler_params=_cparams(("parallel", "arbitrary")),
        name="rwkv_prep",
    )(*tabs, *([z] * 15), *params)


def rwkv_mixer(z, mu, w0, w2, a0, a2, g2, k_k, k_a, r_k, n_ctx):
    r, k, v, kk, kka, lw2, g, bonus = rwkv_prep(z, mu, w0, w2, a0, a2, g2, k_k, k_a, r_k, n_ctx)
    return rwkv_scan(r, k, v, kk, kka, lw2, n_ctx), g, bonus


def kernel(x, c, ctx, c_ctx, ada_w, ada_b, pre_mix, post_mix, pre_ffn, post_ffn, w_in, w_out, gla_a2, gla_a_bias, gla_norm_w, ret_decay_logit, rwkv_mu, rwkv_w0, rwkv_w2, rwkv_a0, rwkv_a2, rwkv_g2, rwkv_k_k, rwkv_k_a, rwkv_r_k, rwkv_lnx_w, rwkv_lnx_b, router_w, router_bias, exp_gate, exp_up, exp_down, shared_gate, shared_up, shared_down):
    bsz, s, d = x.shape
    lc = ctx.shape[1]
    t = lc + s
    depth = ada_w.shape[0]
    n_ctx = lc // CHUNK
    ctx_tiles = lc // RESNORM_TM
    cvec = jnp.concatenate([jax.nn.silu(c), jax.nn.silu(c_ctx)[None], jnp.zeros((8 - bsz - 1, d), F32)], 0)
    mods = [(matmul(cvec, ada_w, tm=8, tn=1024, name="adaln", layer=l) + ada_b[l])[:, None, :] for l in range(depth)]
    xs = jnp.concatenate([ctx, x], 1)
    (hb,) = resnorm(xs, ctx_tiles, ("hb",), pre=(mods[0], 0, 1, pre_mix[0]))
    for l in range(depth):
        ctx_out = l < depth - 1
        out_from = 0 if ctx_out else lc
        wi = _permute_cols(w_in[l]).astype(BF16)
        z = matmul(hb.reshape(bsz * t, d), wi, tm=_tile(bsz * t, 1100), tn=1024,
                   name="w_in").reshape(bsz, t, N_IN_PAD)
        og2 = gla_scan(z, gla_a2[l], gla_a_bias[l], n_ctx)
        or2 = ret_mixer(z, ret_decay_logit[l], n_ctx)
        ow2, g_rw, bonus = rwkv_mixer(z, rwkv_mu[l], rwkv_w0[l], rwkv_w2[l], rwkv_a0[l], rwkv_a2[l],
                                      rwkv_g2[l], rwkv_k_k[l], rwkv_k_a[l], rwkv_r_k[l], n_ctx)
        y = mixer_post(z, og2, or2, ow2, g_rw, bonus, gla_norm_w[l], rwkv_lnx_w[l], rwkv_lnx_b[l], out_from)
        t_out = t - out_from
        y = matmul(y.reshape(bsz * t_out, d), w_out, tm=_tile(bsz * t_out, 1100), tn=512,
                   name="w_out", layer=l).reshape(bsz, t_out, d)
        xs, h32, hp = resnorm(xs, ctx_tiles, ("x", "h32", "hp"), res=(y, mods[l], 2, post_mix[l]),
                              pre=(mods[l], 3, 4, pre_ffn[l]), x_off=out_from // RESNORM_TM)
        if not ctx_out:
            ctx_tiles = 0
        f = moe_ffn(h32.reshape(-1, d), hp.reshape(-1, d // 2), l, router_w, router_bias, exp_gate, exp_up,
                    exp_down, shared_gate, shared_up, shared_down).reshape(xs.shape)
        if ctx_out:
            xs, hb = resnorm(xs, ctx_tiles, ("x", "hb"), res=(f, mods[l], 5, post_ffn[l]),
                             pre=(mods[l + 1], 0, 1, pre_mix[l + 1]))
        else:
            (xs,) = resnorm(xs, ctx_tiles, ("x",), res=(f, mods[l], 5, post_ffn[l]))
    return xs
```
